```python
import math
import jax, jax.numpy as jnp
from jax import lax
import numpy as np

D_MODEL = 2048
BATCH = 8
SEQ = 2048
DEPTH = 4
DEC_BATCH = 16
DEC_SEQ = 64
PAST_LEN = 4096

CHUNK = 64
N_MIXERS = 3
N_SSM_LAYERS = (DEPTH + 2) // 3
N_FOX_LAYERS = (DEPTH + 1) // 3
N_BAND_LAYERS = DEPTH // 3
D_FF = 5632
EPS = 1e-6
SSM_GROUP = 16
SSM_GROUPS = D_MODEL // SSM_GROUP
SSM_STATE = 64
N_HEADS = 16
HEAD_DIM = D_MODEL // N_HEADS
ATTN_SCALE = HEAD_DIM ** -0.5
Q_BLOCK = 128
FORGET_BIAS_INIT = 3.0
BAND_PREV = 8
BAND_WINDOW = BAND_PREV * CHUNK
REL_CLIP = 256
N_REL = 2 * REL_CLIP + 1
NEG_INF = -1e30

kernel_name = 'hybrid_s5_fox_chunkband_streaming_step'


def rmsnorm(x, g):
    xf = x.astype(jnp.float32)
    y = xf * lax.rsqrt(jnp.mean(xf * xf, axis=-1, keepdims=True) + EPS)
    return (y * g.astype(jnp.float32)).astype(x.dtype)


def swiglu(x, w_in, w_out):
    g, u = jnp.split(x @ w_in, 2, axis=-1)
    return (jax.nn.silu(g) * u) @ w_out


def _split_heads(t):
    return t.reshape(t.shape[0], t.shape[1], N_HEADS, HEAD_DIM)


def ssm_discretize(a_re, a_im, log_step, b_re, b_im):
    f32 = jnp.float32
    a_re, a_im = a_re.astype(f32), a_im.astype(f32)
    dt = jnp.exp(log_step.astype(f32))[:, None]
    mag = jnp.exp(a_re * dt)
    ab_re, ab_im = mag * jnp.cos(a_im * dt), mag * jnp.sin(a_im * dt)
    den = a_re * a_re + a_im * a_im
    z_re = ((ab_re - 1.0) * a_re + ab_im * a_im) / den
    z_im = (ab_im * a_re - (ab_re - 1.0) * a_im) / den
    b_re, b_im = b_re.astype(f32), b_im.astype(f32)
    bb_re = z_re[..., None] * b_re - z_im[..., None] * b_im
    bb_im = z_re[..., None] * b_im + z_im[..., None] * b_re
    return dt, a_re, a_im, ab_re, ab_im, bb_re, bb_im


def _ssm_combine(e1, e2):
    a1r, a1i, b1r, b1i = e1
    a2r, a2i, b2r, b2i = e2
    return (a1r * a2r - a1i * a2i, a1r * a2i + a1i * a2r,
            a2r * b1r - a2i * b1i + b2r, a2r * b1i + a2i * b1r + b2i)


def ssm_mixer(h, s0_re, s0_im, disc, c_re, c_im, d_skip, w_glu):
    dt, a_re, a_im, ab_re, ab_im, bb_re, bb_im = disc
    B, L, _ = h.shape
    f32 = jnp.float32
    hf = h.astype(f32)
    u = hf.reshape(B, L, SSM_GROUPS, SSM_GROUP)
    bu_re = jnp.einsum('blgc,gpc->blgp', u, bb_re)
    bu_im = jnp.einsum('blgc,gpc->blgp', u, bb_im)
    shp = bu_re.shape
    elems = (jnp.broadcast_to(ab_re, shp), jnp.broadcast_to(ab_im, shp), bu_re, bu_im)
    _, _, s_re, s_im = lax.associative_scan(_ssm_combine, elems, axis=1)
    if s0_re is not None:
        t = jnp.arange(1, L + 1, dtype=f32)[:, None, None]
        mag = jnp.exp(a_re * dt * t)
        ang = a_im * dt * t
        p_re, p_im = mag * jnp.cos(ang), mag * jnp.sin(ang)
        s0r = s0_re.astype(f32)[:, None]
        s0i = s0_im.astype(f32)[:, None]
        s_re = s_re + p_re * s0r - p_im * s0i
        s_im = s_im + p_re * s0i + p_im * s0r
    y = (jnp.einsum('blgp,gcp->blgc', s_re, c_re.astype(f32))
         - jnp.einsum('blgp,gcp->blgc', s_im, c_im.astype(f32)))
    y = y.reshape(B, L, D_MODEL) + d_skip.astype(f32) * hf
    g = jax.nn.gelu(y).astype(h.dtype)
    val, gate = jnp.split(g @ w_glu, 2, axis=-1)
    return val * jax.nn.sigmoid(gate), s_re[:, -1], s_im[:, -1]


def fox_project(h, w_in, b_f):
    proj = h @ w_in
    q = _split_heads(proj[..., :D_MODEL])
    k = _split_heads(proj[..., D_MODEL:2 * D_MODEL])
    v = _split_heads(proj[..., 2 * D_MODEL:3 * D_MODEL])
    logf = jax.nn.log_sigmoid(proj[..., 3 * D_MODEL:].astype(jnp.float32) + b_f.astype(jnp.float32))
    return q, k, v, logf


def fox_prompt(h, w_in, b_f, w_out):
    B, L, _ = h.shape
    q, k, v, logf = fox_project(h, w_in, b_f)
    c = jnp.cumsum(logf, axis=1).transpose(0, 2, 1)
    nb = L // Q_BLOCK
    q_blocks = q.reshape(B, nb, Q_BLOCK, N_HEADS, HEAD_DIM).transpose(1, 0, 2, 3, 4)
    k_pos = jnp.arange(L)

    def attend_block(args):
        n, q_n = args
        start = n * Q_BLOCK
        q_pos = start + jnp.arange(Q_BLOCK)
        c_q = lax.dynamic_slice_in_dim(c, start, Q_BLOCK, axis=2)
        s = jnp.einsum('bqhd,bkhd->bhqk', q_n, k, preferred_element_type=jnp.float32) * ATTN_SCALE
        s = s + c_q[..., :, None] - c[..., None, :]
        s = jnp.where(k_pos[None, :] <= q_pos[:, None], s, NEG_INF)
        p = jax.nn.softmax(s, axis=-1).astype(v.dtype)
        return jnp.einsum('bhqk,bkhd->bqhd', p, v)

    o = lax.map(attend_block, (jnp.arange(nb), q_blocks))
    o = o.transpose(1, 0, 2, 3, 4).reshape(B, L, D_MODEL)
    return o @ w_out, k, v, logf


def fox_sample(h, k_cache, v_cache, logf_cache, w_in, b_f, w_out):
    B, L, _ = h.shape
    P = k_cache.shape[1]
    q, k, v, logf = fox_project(h, w_in, b_f)
    k_all = jnp.concatenate([k_cache.astype(k.dtype), k], axis=1)
    v_all = jnp.concatenate([v_cache.astype(v.dtype), v], axis=1)
    c = jnp.cumsum(jnp.concatenate([logf_cache.astype(jnp.float32), logf], axis=1), axis=1)
    c = c.transpose(0, 2, 1)
    s = jnp.einsum('bqhd,bkhd->bhqk', q, k_all, preferred_element_type=jnp.float32) * ATTN_SCALE
    s = s + c[..., P:, None] - c[..., None, :]
    mask = jnp.arange(P + L)[None, :] <= (P + jnp.arange(L))[:, None]
    s = jnp.where(mask, s, NEG_INF)
    p = jax.nn.softmax(s, axis=-1).astype(v.dtype)
    o = jnp.einsum('bhqk,bkhd->bqhd', p, v_all).reshape(B, L, D_MODEL)
    return o @ w_out, k, v, logf


def band_project(h, w_in):
    proj = h @ w_in
    return (_split_heads(proj[..., :D_MODEL]), _split_heads(proj[..., D_MODEL:2 * D_MODEL]),
            _split_heads(proj[..., 2 * D_MODEL:]))


def rel_bias_block(rel_bias, q_pos, k_pos):
    idx = jnp.clip(q_pos[:, None] - k_pos[None, :], -REL_CLIP, REL_CLIP) + REL_CLIP
    return rel_bias.astype(jnp.float32)[:, idx]


def band_prompt(h, w_in, rel_bias, w_out):
    B, L, _ = h.shape
    nc = L // CHUNK
    q, k, v = band_project(h, w_in)
    shp = (B, nc, CHUNK, N_HEADS, HEAD_DIM)
    qc = q.reshape(shp)
    pad = jnp.zeros((B, BAND_PREV, CHUNK, N_HEADS, HEAD_DIM), k.dtype)
    kp = jnp.concatenate([pad, k.reshape(shp)], axis=1)
    vp = jnp.concatenate([pad, v.reshape(shp)], axis=1)
    band_idx = jnp.arange(nc)[:, None] + jnp.arange(BAND_PREV + 1)[None, :]
    n_band = (BAND_PREV + 1) * CHUNK
    kb = kp[:, band_idx].reshape(B, nc, n_band, N_HEADS, HEAD_DIM)
    vb = vp[:, band_idx].reshape(B, nc, n_band, N_HEADS, HEAD_DIM)
    s = jnp.einsum('bnqhd,bnkhd->bnhqk', qc, kb, preferred_element_type=jnp.float32) * ATTN_SCALE
    bias = rel_bias_block(rel_bias, jnp.arange(CHUNK), jnp.arange(n_band) - BAND_PREV * CHUNK)
    valid = jnp.repeat(band_idx >= BAND_PREV, CHUNK, axis=1)
    s = jnp.where(valid[None, :, None, None, :], s + bias[None, None], NEG_INF)
    p = jax.nn.softmax(s, axis=-1).astype(v.dtype)
    o = jnp.einsum('bnhqk,bnkhd->bnqhd', p, vb).reshape(B, L, D_MODEL)
    keep = min(BAND_WINDOW, L)
    return o @ w_out, k[:, L - keep:], v[:, L - keep:]


def band_sample(h, k_cache, v_cache, w_in, rel_bias, w_out):
    B, L, _ = h.shape
    Lc = k_cache.shape[1]
    q, k, v = band_project(h, w_in)
    k_all = jnp.concatenate([k_cache.astype(k.dtype), k], axis=1)
    v_all = jnp.concatenate([v_cache.astype(v.dtype), v], axis=1)
    s = jnp.einsum('bqhd,bkhd->bhqk', q, k_all, preferred_element_type=jnp.float32) * ATTN_SCALE
    s = s + rel_bias_block(rel_bias, Lc + jnp.arange(L), jnp.arange(Lc + L))[None]
    p = jax.nn.softmax(s, axis=-1).astype(v.dtype)
    o = jnp.einsum('bhqk,bkhd->bqhd', p, v_all).reshape(B, L, D_MODEL)
    return o @ w_out, k, v


def setup_inputs(seed: int = 0) -> dict:
    key = jax.random.key(seed)
    ks = iter(jax.random.split(key, 40))
    f32 = jnp.float32

    def nrm(shape, std):
        return std * jax.random.normal(next(ks), shape, f32)

    D, H, HD = D_MODEL, N_HEADS, HEAD_DIM
    band_rows = min(BAND_WINDOW, PAST_LEN)
    inp = {}
    inp['x_prompt'] = nrm((BATCH, SEQ, D), 1.0)
    inp['x_sample'] = nrm((DEC_BATCH, DEC_SEQ, D), 1.0)
    inp['state_ssm_re'] = nrm((N_SSM_LAYERS, DEC_BATCH, SSM_GROUPS, SSM_STATE), 0.3)
    inp['state_ssm_im'] = nrm((N_SSM_LAYERS, DEC_BATCH, SSM_GROUPS, SSM_STATE), 0.3)
    inp['cache_fox_k'] = nrm((N_FOX_LAYERS, DEC_BATCH, PAST_LEN, H, HD), 1.0)
    inp['cache_fox_v'] = nrm((N_FOX_LAYERS, DEC_BATCH, PAST_LEN, H, HD), 1.0)
    inp['cache_fox_logf'] = jax.nn.log_sigmoid(FORGET_BIAS_INIT + nrm((N_FOX_LAYERS, DEC_BATCH, PAST_LEN, H), 1.0))
    inp['cache_band_k'] = nrm((N_BAND_LAYERS, DEC_BATCH, band_rows, H, HD), 1.0)
    inp['cache_band_v'] = nrm((N_BAND_LAYERS, DEC_BATCH, band_rows, H, HD), 1.0)
    inp['ffn1_norm'] = 1.0 + nrm((DEPTH, D), 0.05)
    inp['ffn1_w_in'] = nrm((DEPTH, D, 2 * D_FF), D ** -0.5)
    inp['ffn1_w_out'] = nrm((DEPTH, D_FF, D), D_FF ** -0.5)
    inp['mix_norm'] = 1.0 + nrm((DEPTH, D), 0.05)
    inp['ffn2_norm'] = 1.0 + nrm((DEPTH, D), 0.05)
    inp['ffn2_w_in'] = nrm((DEPTH, D, 2 * D_FF), D ** -0.5)
    inp['ffn2_w_out'] = nrm((DEPTH, D_FF, D), D_FF ** -0.5)
    inp['ssm_a_re'] = -0.5 + nrm((N_SSM_LAYERS, SSM_GROUPS, SSM_STATE), 0.01)
    inp['ssm_a_im'] = math.pi * jnp.arange(SSM_STATE, dtype=f32) + nrm((N_SSM_LAYERS, SSM_GROUPS, SSM_STATE), 0.01)
    inp['ssm_log_step'] = jax.random.uniform(next(ks), (N_SSM_LAYERS, SSM_GROUPS), f32,
                                             math.log(1e-3), math.log(1e-1))
    inp['ssm_b_re'] = nrm((N_SSM_LAYERS, SSM_GROUPS, SSM_STATE, SSM_GROUP), SSM_GROUP ** -0.5)
    inp['ssm_b_im'] = nrm((N_SSM_LAYERS, SSM_GROUPS, SSM_STATE, SSM_GROUP), SSM_GROUP ** -0.5)
    inp['ssm_c_re'] = nrm((N_SSM_LAYERS, SSM_GROUPS, SSM_GROUP, SSM_STATE), 0.5)
    inp['ssm_c_im'] = nrm((N_SSM_LAYERS, SSM_GROUPS, SSM_GROUP, SSM_STATE), 0.5)
    inp['ssm_d'] = nrm((N_SSM_LAYERS, D), 1.0)
    inp['ssm_w_glu'] = nrm((N_SSM_LAYERS, D, 2 * D), D ** -0.5)
    inp['fox_w_in'] = nrm((N_FOX_LAYERS, D, 3 * D + H), D ** -0.5)
    inp['fox_b_f'] = FORGET_BIAS_INIT + nrm((N_FOX_LAYERS, H), 0.5)
    inp['fox_w_out'] = nrm((N_FOX_LAYERS, D, D), D ** -0.5)
    inp['band_w_in'] = nrm((N_BAND_LAYERS, D, 3 * D), D ** -0.5)
    inp['band_rel_bias'] = nrm((N_BAND_LAYERS, H, N_REL), 0.5)
    inp['band_w_out'] = nrm((N_BAND_LAYERS, D, D), D ** -0.5)
    inp['final_norm'] = 1.0 + nrm((D,), 0.05)
    return inp


def reference(x_prompt, x_sample, state_ssm_re, state_ssm_im, cache_fox_k, cache_fox_v, cache_fox_logf,
              cache_band_k, cache_band_v, ffn1_norm, ffn1_w_in, ffn1_w_out, mix_norm, ffn2_norm, ffn2_w_in,
              ffn2_w_out, ssm_a_re, ssm_a_im, ssm_log_step, ssm_b_re, ssm_b_im, ssm_c_re, ssm_c_im, ssm_d,
              ssm_w_glu, fox_w_in, fox_b_f, fox_w_out, band_w_in, band_rel_bias, band_w_out, final_norm):
    xp, xs = x_prompt, x_sample
    p_ssm_re, p_ssm_im, s_ssm_re, s_ssm_im = [], [], [], []
    p_fox_k, p_fox_v, p_fox_logf, s_fox_k, s_fox_v, s_fox_logf = [], [], [], [], [], []
    p_band_k, p_band_v, s_band_k, s_band_v = [], [], [], []
    for i in range(DEPTH):
        kind, j = i % N_MIXERS, i // N_MIXERS
        xp = xp + 0.5 * swiglu(rmsnorm(xp, ffn1_norm[i]), ffn1_w_in[i], ffn1_w_out[i])
        xs = xs + 0.5 * swiglu(rmsnorm(xs, ffn1_norm[i]), ffn1_w_in[i], ffn1_w_out[i])
        hp, hs = rmsnorm(xp, mix_norm[i]), rmsnorm(xs, mix_norm[i])
        if kind == 0:
            disc = ssm_discretize(ssm_a_re[j], ssm_a_im[j], ssm_log_step[j], ssm_b_re[j], ssm_b_im[j])
            mp, pr, pi = ssm_mixer(hp, None, None, disc, ssm_c_re[j], ssm_c_im[j], ssm_d[j], ssm_w_glu[j])
            ms, sr, si = ssm_mixer(hs, state_ssm_re[j], state_ssm_im[j], disc,
                                   ssm_c_re[j], ssm_c_im[j], ssm_d[j], ssm_w_glu[j])
            p_ssm_re.append(pr); p_ssm_im.append(pi); s_ssm_re.append(sr); s_ssm_im.append(si)
        elif kind == 1:
            mp, kp, vp, lp = fox_prompt(hp, fox_w_in[j], fox_b_f[j], fox_w_out[j])
            ms, kn, vn, ln = fox_sample(hs, cache_fox_k[j], cache_fox_v[j], cache_fox_logf[j],
                                        fox_w_in[j], fox_b_f[j], fox_w_out[j])
            p_fox_k.append(kp); p_fox_v.append(vp); p_fox_logf.append(lp)
            s_fox_k.append(kn); s_fox_v.append(vn); s_fox_logf.append(ln)
        else:
            mp, kp, vp = band_prompt(hp, band_w_in[j], band_rel_bias[j], band_w_out[j])
            ms, kn, vn = band_sample(hs, cache_band_k[j], cache_band_v[j],
                                     band_w_in[j], band_rel_bias[j], band_w_out[j])
            p_band_k.append(kp); p_band_v.append(vp); s_band_k.append(kn); s_band_v.append(vn)
        xp = xp + mp
        xs = xs + ms
        xp = xp + 0.5 * swiglu(rmsnorm(xp, ffn2_norm[i]), ffn2_w_in[i], ffn2_w_out[i])
        xs = xs + 0.5 * swiglu(rmsnorm(xs, ffn2_norm[i]), ffn2_w_in[i], ffn2_w_out[i])
    y_prompt = rmsnorm(xp, final_norm)
    y_sample = rmsnorm(xs, final_norm)
    return (y_prompt, y_sample,
            jnp.stack(p_ssm_re), jnp.stack(p_ssm_im),
            jnp.stack(p_fox_k), jnp.stack(p_fox_v), jnp.stack(p_fox_logf),
            jnp.stack(p_band_k), jnp.stack(p_band_v),
            jnp.stack(s_ssm_re), jnp.stack(s_ssm_im),
            jnp.stack(s_fox_k), jnp.stack(s_fox_v), jnp.stack(s_fox_logf),
            jnp.stack(s_band_k), jnp.stack(s_band_v))
```

```python
import functools

import jax
import jax.numpy as jnp
from jax import lax
from jax.experimental import pallas as pl
from jax.experimental.pallas import tpu as pltpu

F32 = jnp.float32
BF16 = jnp.bfloat16

EPS = 1e-6
NEG_INF = -1e30
CHUNK = 64
BAND_PREV = 8
REL_CLIP = 256
SSM_GROUP = 16
SSM_T = 16
LANES = 128
VMEM_LIMIT = 56 * 1024 * 1024

TM = 512
TF = 512
TN = 512
BAND_TQ = 256
FOX_TQ = 256


def _params(*sem):
    return pltpu.CompilerParams(dimension_semantics=sem, vmem_limit_bytes=VMEM_LIMIT)


def _rms(x, g):
    return x * lax.rsqrt(jnp.mean(x * x, axis=-1, keepdims=True) + EPS) * g


def _dot(a, b):
    return jnp.dot(a, b, preferred_element_type=F32)


def _dot_nt(a, b):
    return lax.dot_general(a, b, (((1,), (1,)), ((), ())), preferred_element_type=F32)


def _ffn_body(x_ref, g_ref, wg_ref, wu_ref, wo_ref, o_ref, h_ref, acc_ref):
    f = pl.program_id(1)

    @pl.when(f == 0)
    def _():
        h_ref[...] = _rms(x_ref[...], g_ref[...]).astype(BF16)
        acc_ref[...] = jnp.zeros_like(acc_ref)

    h = h_ref[...]
    gate = _dot(h, wg_ref[...])
    up = _dot(h, wu_ref[...])
    act = (gate * jax.nn.sigmoid(gate) * up).astype(BF16)
    acc_ref[...] += _dot(act, wo_ref[...])

    @pl.when(f == pl.num_programs(1) - 1)
    def _():
        o_ref[...] = x_ref[...] + 0.5 * acc_ref[...]


def ffn_half_step(x, g, w_in, w_out, tm=TM, tf=TF):
    m, d = x.shape
    f = w_out.shape[0]
    tm = min(tm, m)
    tf = min(tf, f)
    nf = f // tf
    return pl.pallas_call(
        _ffn_body,
        grid=(m // tm, nf),
        in_specs=[
            pl.BlockSpec((tm, d), lambda i, j: (i, 0)),
            pl.BlockSpec((1, d), lambda i, j: (0, 0)),
            pl.BlockSpec((d, tf), lambda i, j: (0, j)),
            pl.BlockSpec((d, tf), lambda i, j: (0, j + nf)),
            pl.BlockSpec((tf, d), lambda i, j: (j, 0)),
        ],
        out_specs=pl.BlockSpec((tm, d), lambda i, j: (i, 0)),
        out_shape=jax.ShapeDtypeStruct((m, d), F32),
        scratch_shapes=[pltpu.VMEM((tm, d), BF16), pltpu.VMEM((tm, d), F32)],
        compiler_params=_params("parallel", "arbitrary"),
        name="ffn_half_step",
    )(x, g.reshape(1, d), w_in, w_in, w_out)


def _rmsnorm_body(x_ref, g_ref, o_ref):
    o_ref[...] = _rms(x_ref[...], g_ref[...]).astype(o_ref.dtype)


def rmsnorm_rows(x, g, row0, rows, out_dtype, tm=TM):
    d = x.shape[1]
    tm = min(tm, rows)
    off = row0 // tm
    return pl.pallas_call(
        _rmsnorm_body,
        grid=(rows // tm,),
        in_specs=[pl.BlockSpec((tm, d), lambda i: (i + off, 0)),
                  pl.BlockSpec((1, d), lambda i: (0, 0))],
        out_specs=pl.BlockSpec((tm, d), lambda i: (i, 0)),
        out_shape=jax.ShapeDtypeStruct((rows, d), out_dtype),
        compiler_params=_params("parallel"),
        name="rmsnorm",
    )(x, g.reshape(1, d))


def _qkv_body(scale, has_forget, *refs):
    if has_forget:
        (x_ref, g_ref, wq_ref, wk_ref, wv_ref, wf_ref, bf_ref,
         q_ref, k32_ref, v32_ref, kb_ref, vb_ref, lf_ref, h_ref) = refs
    else:
        (x_ref, g_ref, wq_ref, wk_ref, wv_ref,
         q_ref, k32_ref, v32_ref, kb_ref, vb_ref, h_ref) = refs

    @pl.when(pl.program_id(1) == 0)
    def _():
        h = _rms(x_ref[...], g_ref[...]).astype(BF16)
        h_ref[...] = h
        if has_forget:
            z = _dot_nt(wf_ref[...], h) + bf_ref[...]
            lf_ref[...] = jnp.minimum(z, 0.0) - jnp.log(1.0 + jnp.exp(-jnp.abs(z)))

    h = h_ref[...]
    q_ref[...] = (_dot(h, wq_ref[...]) * scale).astype(BF16)
    k = _dot(h, wk_ref[...])
    k32_ref[...] = k
    kb_ref[...] = k.astype(BF16)
    v = _dot(h, wv_ref[...])
    v32_ref[...] = v
    vb_ref[...] = v.astype(BF16)


def qkv_project(x, g, w, row0, rows, scale, wf_t=None, b_f=None, tm=TM, tn=TN):
    d = x.shape[1]
    tm = min(tm, rows)
    tn = min(tn, d)
    off = row0 // tm
    nq = d // tn
    has_forget = wf_t is not None
    in_specs = [
        pl.BlockSpec((tm, d), lambda i, n: (i + off, 0)),
        pl.BlockSpec((1, d), lambda i, n: (0, 0)),
        pl.BlockSpec((d, tn), lambda i, n: (0, n)),
        pl.BlockSpec((d, tn), lambda i, n: (0, n + nq)),
        pl.BlockSpec((d, tn), lambda i, n: (0, n + 2 * nq)),
    ]
    args = [x, g.reshape(1, d), w, w, w]
    tile = pl.BlockSpec((tm, tn), lambda i, n: (i, n))
    out_specs = [tile] * 5
    out_shape = [jax.ShapeDtypeStruct((rows, d), BF16),
                 jax.ShapeDtypeStruct((rows, d), F32), jax.ShapeDtypeStruct((rows, d), F32),
                 jax.ShapeDtypeStruct((rows, d), BF16), jax.ShapeDtypeStruct((rows, d), BF16)]
    if has_forget:
        nh = wf_t.shape[0]
        in_specs += [pl.BlockSpec((nh, d), lambda i, n: (0, 0)),
                     pl.BlockSpec((nh, 1), lambda i, n: (0, 0))]
        args += [wf_t, b_f.reshape(nh, 1)]
        out_specs = out_specs + [pl.BlockSpec((nh, tm), lambda i, n: (0, i))]
        out_shape = out_shape + [jax.ShapeDtypeStruct((nh, rows), F32)]
    return pl.pallas_call(
        functools.partial(_qkv_body, scale, has_forget),
        grid=(rows // tm, nq),
        in_specs=in_specs,
        out_specs=out_specs,
        out_shape=out_shape,
        scratch_shapes=[pltpu.VMEM((tm, d), BF16)],
        compiler_params=_params("parallel", "arbitrary"),
        name="qkv_project",
    )(*args)


def _out_proj_body(x_ref, o_ref, w_ref, y_ref):
    y_ref[...] = x_ref[...] + _dot(o_ref[...], w_ref[...])


def out_project(x, o, w, tm=TM):
    m, d = x.shape
    tm = min(tm, m)
    return pl.pallas_call(
        _out_proj_body,
        grid=(m // tm,),
        in_specs=[pl.BlockSpec((tm, d), lambda i: (i, 0)),
                  pl.BlockSpec((tm, d), lambda i: (i, 0)),
                  pl.BlockSpec((d, d), lambda i: (0, 0))],
        out_specs=pl.BlockSpec((tm, d), lambda i: (i, 0)),
        out_shape=jax.ShapeDtypeStruct((m, d), F32),
        compiler_params=_params("parallel"),
        name="out_project",
    )(x, o, w)


def _glu_body(x_ref, y_ref, g_ref, d_ref, xt_ref, wv_ref, wg_ref, o_ref, a_ref):
    @pl.when(pl.program_id(1) == 0)
    def _():
        h = _rms(x_ref[...], g_ref[...])
        a_ref[...] = jax.nn.gelu(y_ref[...] + d_ref[...] * h).astype(BF16)

    a = a_ref[...]
    val = _dot(a, wv_ref[...])
    gate = _dot(a, wg_ref[...])
    o_ref[...] = xt_ref[...] + val * jax.nn.sigmoid(gate)


def ssm_glu_out(x, y, g, d_skip, w_glu, tm=TM, tn=TN):
    m, d = x.shape
    tm = min(tm, m)
    tn = min(tn, d)
    nq = d // tn
    return pl.pallas_call(
        _glu_body,
        grid=(m // tm, nq),
        in_specs=[
            pl.BlockSpec((tm, d), lambda i, n: (i, 0)),
            pl.BlockSpec((tm, d), lambda i, n: (i, 0)),
            pl.BlockSpec((1, d), lambda i, n: (0, 0)),
            pl.BlockSpec((1, d), lambda i, n: (0, 0)),
            pl.BlockSpec((tm, tn), lambda i, n: (i, n)),
            pl.BlockSpec((d, tn), lambda i, n: (0, n)),
            pl.BlockSpec((d, tn), lambda i, n: (0, n + nq)),
        ],
        out_specs=pl.BlockSpec((tm, tn), lambda i, n: (i, n)),
        out_shape=jax.ShapeDtypeStruct((m, d), F32),
        scratch_shapes=[pltpu.VMEM((tm, d), BF16)],
        compiler_params=_params("parallel", "arbitrary"),
        name="ssm_glu_out",
    )(x, y, g.reshape(1, d), d_skip.reshape(1, d), x, w_glu, w_glu)


def _ssm_kernel_body(ca_ref, bb_ref, k_ref):
    k_ref[0] = jnp.dot(ca_ref[0], bb_ref[0], preferred_element_type=F32,
                       precision=lax.Precision.HIGHEST)


def ssm_lag_kernels(ca, bb):
    g, r, p2 = ca.shape
    c = bb.shape[2]
    return pl.pallas_call(
        _ssm_kernel_body,
        grid=(g,),
        in_specs=[pl.BlockSpec((1, r, p2), lambda i: (i, 0, 0)),
                  pl.BlockSpec((1, p2, c), lambda i: (i, 0, 0))],
        out_specs=pl.BlockSpec((1, r, c), lambda i: (i, 0, 0)),
        out_shape=jax.ShapeDtypeStruct((g, r, c), F32),
        compiler_params=_params("parallel"),
        name="ssm_lag_kernels",
    )(ca, bb)


def _ssm_core_body(nb, nch, u_ref, toep_ref, wsr_ref, wsi_ref, wcr_ref, wci_ref, at_ref,
                   s0r_ref, s0i_ref, y_ref, fr_ref, fi_ref, str_ref, sti_ref, dr_ref, di_ref):
    u = u_ref[0]
    dr_ref[...] = _dot(u, wsr_ref[0])
    di_ref[...] = _dot(u, wsi_ref[0])
    p = at_ref.shape[2]
    ar = jnp.broadcast_to(at_ref[0, 0:1, :], (nb, p))
    ai = jnp.broadcast_to(at_ref[0, 1:2, :], (nb, p))

    def step(k, carry):
        sr, si = carry
        rows = pl.ds(pl.multiple_of(k * nb, nb), nb)
        str_ref[rows, :] = sr
        sti_ref[rows, :] = si
        nr = ar * sr - ai * si + dr_ref[rows, :]
        ni = ar * si + ai * sr + di_ref[rows, :]
        return nr, ni

    sr, si = lax.fori_loop(0, nch, step, (s0r_ref[0], s0i_ref[0]))
    fr_ref[0] = sr
    fi_ref[0] = si
    y_ref[0] = (_dot(u, toep_ref[0])
                + _dot(str_ref[...].astype(BF16), wcr_ref[0])
                + _dot(sti_ref[...].astype(BF16), wci_ref[0]))


def ssm_core(u, mats, s0r, s0i, nb, nch):
    toep, wsr, wsi, wcr, wci, at = mats
    g, m, tc = u.shape
    p = wsr.shape[2]
    blk = lambda *s: pl.BlockSpec((1,) + s, lambda i: (i, 0, 0))
    return pl.pallas_call(
        functools.partial(_ssm_core_body, nb, nch),
        grid=(g,),
        in_specs=[blk(m, tc), blk(tc, tc), blk(tc, p), blk(tc, p), blk(p, tc), blk(p, tc),
                  blk(2, p), blk(nb, p), blk(nb, p)],
        out_specs=[blk(m, tc), blk(nb, p), blk(nb, p)],
        out_shape=[jax.ShapeDtypeStruct((g, m, tc), F32),
                   jax.ShapeDtypeStruct((g, nb, p), F32),
                   jax.ShapeDtypeStruct((g, nb, p), F32)],
        scratch_shapes=[pltpu.VMEM((m, p), F32)] * 4,
        compiler_params=_params("parallel"),
        name="ssm_core",
    )(u, toep, wsr, wsi, wcr, wci, at, s0r, s0i)


def ssm_matrices(a_re, a_im, log_step, b_re, b_im, c_re, c_im):
    t = SSM_T
    g, p = a_re.shape
    c = b_re.shape[2]
    dt = jnp.exp(log_step)[:, None]
    lam_re, lam_im = a_re * dt, a_im * dt
    j = jnp.arange(t + 1, dtype=F32)[:, None, None]
    mag = jnp.exp(lam_re * j)
    pw_re, pw_im = mag * jnp.cos(lam_im * j), mag * jnp.sin(lam_im * j)
    ab_re, ab_im = pw_re[1], pw_im[1]
    den = a_re * a_re + a_im * a_im
    z_re = ((ab_re - 1.0) * a_re + ab_im * a_im) / den
    z_im = (ab_im * a_re - (ab_re - 1.0) * a_im) / den
    bb_re = z_re[..., None] * b_re - z_im[..., None] * b_im
    bb_im = z_re[..., None] * b_im + z_im[..., None] * b_re
    rp_re, rp_im = pw_re[t - 1::-1][:t], pw_im[t - 1::-1][:t]
    ws_re = rp_re[:, :, :, None] * bb_re[None] - rp_im[:, :, :, None] * bb_im[None]
    ws_im = rp_re[:, :, :, None] * bb_im[None] + rp_im[:, :, :, None] * bb_re[None]
    ws_re = ws_re.transpose(1, 0, 3, 2).reshape(g, t * c, p)
    ws_im = ws_im.transpose(1, 0, 3, 2).reshape(g, t * c, p)

    def out_weights(q_re, q_im):
        cr, ci = c_re[None], c_im[None]
        qr, qi = q_re[:, :, None, :], q_im[:, :, None, :]
        return cr * qr - ci * qi, -cr * qi - ci * qr

    wc_re, wc_im = out_weights(pw_re[1:], pw_im[1:])
    wc_re = wc_re.transpose(1, 3, 0, 2).reshape(g, p, t * c)
    wc_im = wc_im.transpose(1, 3, 0, 2).reshape(g, p, t * c)
    ca_re, ca_im = out_weights(pw_re[:t], pw_im[:t])
    ca = jnp.concatenate([ca_re, ca_im], axis=-1).transpose(1, 0, 2, 3).reshape(g, t * c, 2 * p)
    bb = jnp.concatenate([bb_re, bb_im], axis=1)
    kmat = ssm_lag_kernels(ca, bb).reshape(g, t, c, c)
    lag = jnp.arange(t)[None, :] - jnp.arange(t)[:, None]
    kpad = jnp.concatenate([kmat, jnp.zeros((g, 1, c, c), F32)], axis=1)
    toep = kpad[:, jnp.where(lag >= 0, lag, t)]
    toep = toep.transpose(0, 1, 4, 2, 3).reshape(g, t * c, t * c)
    at = jnp.stack([pw_re[t], pw_im[t]], axis=1)
    return (toep.astype(BF16), ws_re.astype(BF16), ws_im.astype(BF16),
            wc_re.astype(BF16), wc_im.astype(BF16), at)


def ssm_mix(h, nb, length, mats, s0r, s0i):
    d = h.shape[1]
    g = d // SSM_GROUP
    nch = length // SSM_T
    u = h.reshape(nb, nch, SSM_T, g, SSM_GROUP).transpose(3, 1, 0, 2, 4)
    u = u.reshape(g, nch * nb, SSM_T * SSM_GROUP)
    y, fr, fi = ssm_core(u, mats, s0r, s0i, nb, nch)
    y = y.reshape(g, nch, nb, SSM_T, SSM_GROUP).transpose(2, 1, 3, 0, 4).reshape(nb * length, d)
    return y, fr.transpose(1, 0, 2), fi.transpose(1, 0, 2)


def _split3(x):
    hi = x.astype(BF16)
    r = x - hi.astype(F32)
    mid = r.astype(BF16)
    lo = (r - mid.astype(F32)).astype(BF16)
    return hi, mid, lo


def _cumsum_body(x_ref, o_ref):
    nblk = x_ref.shape[2] // LANES
    rows = x_ref.shape[1]
    row = lax.broadcasted_iota(jnp.int32, (LANES, LANES), 0)
    col = lax.broadcasted_iota(jnp.int32, (LANES, LANES), 1)
    tri = jnp.where(row <= col, 1.0, 0.0).astype(BF16)
    carry = jnp.zeros((rows, 1), F32)
    for j in range(nblk):
        sl = slice(j * LANES, (j + 1) * LANES)
        hi, mid, lo = _split3(x_ref[0, :, sl])
        w = _dot(hi, tri) + _dot(mid, tri) + _dot(lo, tri)
        o_ref[0, :, sl] = w + carry
        carry = carry + jnp.sum(x_ref[0, :, sl], axis=1, keepdims=True)


def cumsum_time(x):
    n, h, length = x.shape
    return pl.pallas_call(
        _cumsum_body,
        grid=(n,),
        in_specs=[pl.BlockSpec((1, h, length), lambda i: (i, 0, 0))],
        out_specs=pl.BlockSpec((1, h, length), lambda i: (i, 0, 0)),
        out_shape=jax.ShapeDtypeStruct((n, h, length), F32),
        compiler_params=_params("parallel"),
        name="cumsum_time",
    )(x)


def _softmax_pv(parts):
    m = functools.reduce(jnp.maximum, [jnp.max(s, axis=-1, keepdims=True) for s, _ in parts])
    acc, den = None, None
    for s, v in parts:
        e = jnp.exp(s - m)
        l = jnp.sum(e, axis=-1, keepdims=True)
        o = _dot(e.astype(BF16), v)
        acc = o if acc is None else acc + o
        den = l if den is None else den + l
    return acc / den


def _fox_prompt_body(tq, q_ref, k_ref, v_ref, c_ref, o_ref):
    length = q_ref.shape[0]
    for i in range(length // tq):
        nk = (i + 1) * tq
        q = q_ref[i * tq:(i + 1) * tq, :]
        s = _dot_nt(q, k_ref[0:nk, :]) - c_ref[0, :, 0:nk]
        rpos = lax.broadcasted_iota(jnp.int32, (tq, nk), 0) + i * tq
        cpos = lax.broadcasted_iota(jnp.int32, (tq, nk), 1)
        s = jnp.where(cpos <= rpos, s, NEG_INF)
        o_ref[i * tq:(i + 1) * tq, :] = _softmax_pv([(s, v_ref[0:nk, :])]).astype(o_ref.dtype)


def fox_prompt_attention(q, kb, vb, c, nb, nh, tq=FOX_TQ):
    m, d = q.shape
    length = m // nb
    hd = d // nh
    tq = min(tq, length)
    seq = pl.BlockSpec((length, hd), lambda b, h: (b, h))
    return pl.pallas_call(
        functools.partial(_fox_prompt_body, tq),
        grid=(nb, nh),
        in_specs=[seq, seq, seq, pl.BlockSpec((1, 1, length), lambda b, h: (b * nh + h, 0, 0))],
        out_specs=seq,
        out_shape=jax.ShapeDtypeStruct((m, d), BF16),
        compiler_params=_params("parallel", "parallel"),
        name="fox_prompt_attention",
    )(q, kb, vb, c)


def _fox_sample_body(past, q_ref, kc_ref, vc_ref, kn_ref, vn_ref, c_ref, o_ref):
    q = q_ref[...]
    lq = q.shape[0]
    s_c = _dot_nt(q, kc_ref[0].astype(BF16)) - c_ref[0, :, 0:past]
    s_n = _dot_nt(q, kn_ref[...]) - c_ref[0, :, past:past + lq]
    rpos = lax.broadcasted_iota(jnp.int32, (lq, lq), 0)
    cpos = lax.broadcasted_iota(jnp.int32, (lq, lq), 1)
    s_n = jnp.where(cpos <= rpos, s_n, NEG_INF)
    o = _softmax_pv([(s_c, vc_ref[0].astype(BF16)), (s_n, vn_ref[...])])
    o_ref[...] = o.astype(o_ref.dtype)


def fox_sample_attention(q, k_cache, v_cache, kb, vb, c, nb, nh):
    m, d = q.shape
    lq = m // nb
    hd = d // nh
    past = k_cache.shape[1]
    new = pl.BlockSpec((lq, hd), lambda b, h: (b, h))
    cache = pl.BlockSpec((1, past, hd), lambda b, h: (b, 0, h))
    return pl.pallas_call(
        functools.partial(_fox_sample_body, past),
        grid=(nb, nh),
        in_specs=[new, cache, cache, new, new,
                  pl.BlockSpec((1, 1, c.shape[2]), lambda b, h: (b * nh + h, 0, 0))],
        out_specs=new,
        out_shape=jax.ShapeDtypeStruct((m, d), BF16),
        compiler_params=_params("parallel", "parallel"),
        name="fox_sample_attention",
    )(q, k_cache, v_cache, kb, vb, c)


def _band_prompt_body(tq, q_ref, k_ref, v_ref, bias_ref, o_ref):
    length = q_ref.shape[0]
    span = bias_ref.shape[2]
    for i in range(length // tq):
        k0 = max((i + 1) * tq - span, 0)
        k1 = (i + 1) * tq
        q = q_ref[i * tq:(i + 1) * tq, :]
        s = _dot_nt(q, k_ref[k0:k1, :]) + bias_ref[0, :, span - (k1 - k0):span]
        o_ref[i * tq:(i + 1) * tq, :] = _softmax_pv([(s, v_ref[k0:k1, :])]).astype(o_ref.dtype)


def band_prompt_attention(q, kb, vb, bias, nb, nh):
    m, d = q.shape
    length = m // nb
    hd = d // nh
    tq = bias.shape[1]
    seq = pl.BlockSpec((length, hd), lambda b, h: (b, h))
    return pl.pallas_call(
        functools.partial(_band_prompt_body, tq),
        grid=(nb, nh),
        in_specs=[seq, seq, seq, pl.BlockSpec((1,) + bias.shape[1:], lambda b, h: (h, 0, 0))],
        out_specs=seq,
        out_shape=jax.ShapeDtypeStruct((m, d), BF16),
        compiler_params=_params("parallel", "parallel"),
        name="band_prompt_attention",
    )(q, kb, vb, bias)


def _band_sample_body(q_ref, kc_ref, vc_ref, kn_ref, vn_ref, bias_ref, o_ref):
    q = q_ref[...]
    lc = kc_ref.shape[1]
    lq = q.shape[0]
    s_c = _dot_nt(q, kc_ref[0].astype(BF16)) + bias_ref[0, :, 0:lc]
    s_n = _dot_nt(q, kn_ref[...]) + bias_ref[0, :, lc:lc + lq]
    o = _softmax_pv([(s_c, vc_ref[0].astype(BF16)), (s_n, vn_ref[...])])
    o_ref[...] = o.astype(o_ref.dtype)


def band_sample_attention(q, k_cache, v_cache, kb, vb, bias, nb, nh):
    m, d = q.shape
    lq = m // nb
    hd = d // nh
    lc = k_cache.shape[1]
    new = pl.BlockSpec((lq, hd), lambda b, h: (b, h))
    cache = pl.BlockSpec((1, lc, hd), lambda b, h: (b, 0, h))
    return pl.pallas_call(
        _band_sample_body,
        grid=(nb, nh),
        in_specs=[new, cache, cache, new, new,
                  pl.BlockSpec((1,) + bias.shape[1:], lambda b, h: (h, 0, 0))],
        out_specs=new,
        out_shape=jax.ShapeDtypeStruct((m, d), BF16),
        compiler_params=_params("parallel", "parallel"),
        name="band_sample_attention",
    )(q, k_cache, v_cache, kb, vb, bias)


def band_bias_tile(rel_bias, tq, window):
    qpos = jnp.arange(tq)[:, None]
    kpos = jnp.arange(window + tq)[None, :] - window
    idx = jnp.clip(qpos - kpos, -REL_CLIP, REL_CLIP) + REL_CLIP
    qc = qpos // CHUNK
    kc = jnp.floor_divide(kpos, CHUNK)
    visible = (kc <= qc) & (kc >= qc - BAND_PREV)
    return jnp.where(visible[None], rel_bias.astype(F32)[:, idx], NEG_INF)


def kernel(x_prompt, x_sample, state_ssm_re, state_ssm_im, cache_fox_k, cache_fox_v, cache_fox_logf,
           cache_band_k, cache_band_v, ffn1_norm, ffn1_w_in, ffn1_w_out, mix_norm, ffn2_norm, ffn2_w_in,
           ffn2_w_out, ssm_a_re, ssm_a_im, ssm_log_step, ssm_b_re, ssm_b_im, ssm_c_re, ssm_c_im, ssm_d,
           ssm_w_glu, fox_w_in, fox_b_f, fox_w_out, band_w_in, band_rel_bias, band_w_out, final_norm):
    bp, lp, d = x_prompt.shape
    bs, ls, _ = x_sample.shape
    mp, ms = bp * lp, bs * ls
    depth = ffn1_norm.shape[0]
    nh = fox_b_f.shape[1]
    hd = d // nh
    scale = hd ** -0.5
    past = cache_fox_k.shape[2]
    window = cache_band_k.shape[2]

    x = jnp.concatenate([x_prompt.reshape(mp, d), x_sample.reshape(ms, d)], axis=0)
    outs = {name: [] for name in (
        "p_ssm_re", "p_ssm_im", "p_fox_k", "p_fox_v", "p_fox_logf", "p_band_k", "p_band_v",
        "s_ssm_re", "s_ssm_im", "s_fox_k", "s_fox_v", "s_fox_logf", "s_band_k", "s_band_v")}

    for i in range(depth):
        kind, j = i % 3, i // 3
        x = ffn_half_step(x, ffn1_norm[i], ffn1_w_in[i].astype(BF16), ffn1_w_out[i].astype(BF16))
        if kind == 0:
            mats = ssm_matrices(ssm_a_re[j], ssm_a_im[j], ssm_log_step[j], ssm_b_re[j], ssm_b_im[j],
                                ssm_c_re[j], ssm_c_im[j])
            g, p = ssm_a_re.shape[1:]
            h = rmsnorm_rows(x, mix_norm[i], 0, mp + ms, BF16)
            zeros = jnp.zeros((g, bp, p), F32)
            yp, pr, pi = ssm_mix(h[:mp], bp, lp, mats, zeros, zeros)
            ys, sr, si = ssm_mix(h[mp:], bs, ls, mats, state_ssm_re[j].transpose(1, 0, 2),
                                 state_ssm_im[j].transpose(1, 0, 2))
            outs["p_ssm_re"].append(pr); outs["p_ssm_im"].append(pi)
            outs["s_ssm_re"].append(sr); outs["s_ssm_im"].append(si)
            x = ssm_glu_out(x, jnp.concatenate([yp, ys], axis=0), mix_norm[i], ssm_d[j],
                            ssm_w_glu[j].astype(BF16))
        elif kind == 1:
            w = fox_w_in[j]
            w_qkv = w[:, :3 * d].astype(BF16)
            wf_t = w[:, 3 * d:].T.astype(BF16)
            qp, kp, vp, kbp, vbp, lfp = qkv_project(x, mix_norm[i], w_qkv, 0, mp, scale, wf_t, fox_b_f[j])
            qs, ks, vs, kbs, vbs, lfs = qkv_project(x, mix_norm[i], w_qkv, mp, ms, scale, wf_t, fox_b_f[j])
            lfp = lfp.reshape(nh, bp, lp).transpose(1, 0, 2)
            lfs = lfs.reshape(nh, bs, ls).transpose(1, 0, 2)
            cp = cumsum_time(lfp).reshape(bp * nh, 1, lp)
            pad = (-(past + ls)) % LANES
            lf_all = jnp.concatenate([cache_fox_logf[j].astype(F32).transpose(0, 2, 1), lfs,
                                      jnp.zeros((bs, nh, pad), F32)], axis=2)
            cs = cumsum_time(lf_all).reshape(bs * nh, 1, past + ls + pad)
            op = fox_prompt_attention(qp, kbp, vbp, cp, bp, nh)
            os_ = fox_sample_attention(qs, cache_fox_k[j].reshape(bs, past, d),
                                       cache_fox_v[j].reshape(bs, past, d), kbs, vbs, cs, bs, nh)
            x = out_project(x, jnp.concatenate([op, os_], axis=0), fox_w_out[j].astype(BF16))
            outs["p_fox_k"].append(kp.reshape(bp, lp, nh, hd)); outs["p_fox_v"].append(vp.reshape(bp, lp, nh, hd))
            outs["p_fox_logf"].append(lfp.transpose(0, 2, 1))
            outs["s_fox_k"].append(ks.reshape(bs, ls, nh, hd)); outs["s_fox_v"].append(vs.reshape(bs, ls, nh, hd))
            outs["s_fox_logf"].append(lfs.transpose(0, 2, 1))
        else:
            w_qkv = band_w_in[j].astype(BF16)
            qp, kp, vp, kbp, vbp = qkv_project(x, mix_norm[i], w_qkv, 0, mp, scale)
            qs, ks, vs, kbs, vbs = qkv_project(x, mix_norm[i], w_qkv, mp, ms, scale)
            bw = BAND_PREV * CHUNK
            bias = band_bias_tile(band_rel_bias[j], BAND_TQ, bw)
            op = band_prompt_attention(qp, kbp, vbp, bias, bp, nh)
            os_ = band_sample_attention(qs, cache_band_k[j].reshape(bs, window, d),
                                        cache_band_v[j].reshape(bs, window, d), kbs, vbs,
                                        bias[:, :ls, bw - window:bw + ls], bs, nh)
            x = out_project(x, jnp.concatenate([op, os_], axis=0), band_w_out[j].astype(BF16))
            keep = min(BAND_PREV * CHUNK, lp)
            kp4, vp4 = kp.reshape(bp, lp, nh, hd), vp.reshape(bp, lp, nh, hd)
            outs["p_band_k"].append(kp4[:, lp - keep:]); outs["p_band_v"].append(vp4[:, lp - keep:])
            outs["s_band_k"].append(ks.reshape(bs, ls, nh, hd)); outs["s_band_v"].append(vs.reshape(bs, ls, nh, hd))
        x = ffn_half_step(x, ffn2_norm[i], ffn2_w_in[i].astype(BF16), ffn2_w_out[i].astype(BF16))

    y_prompt = rmsnorm_rows(x, final_norm, 0, mp, F32).reshape(bp, lp, d)
    y_sample = rmsnorm_rows(x, final_norm, mp, ms, F32).reshape(bs, ls, d)
    st = {k: jnp.stack(v) for k, v in outs.items()}
    return (y_prompt, y_sample, st["p_ssm_re"], st["p_ssm_im"], st["p_fox_k"], st["p_fox_v"],
            st["p_fox_logf"], st["p_band_k"], st["p_band_v"], st["s_ssm_re"], st["s_ssm_im"],
            st["s_fox_k"], st["s_fox_v"], st["s_fox_logf"], st["s_band_k"], st["s_band_v"])
```

```python
import functools

import jax
import jax.numpy as jnp
from jax import lax
from jax.experimental import pallas as pl
from jax.experimental.pallas import tpu as pltpu

F32 = jnp.float32
BF16 = jnp.bfloat16

EPS = 1e-6
NEG_INF = -1e30
CHUNK = 64
BAND_PREV = 8
REL_CLIP = 256
SSM_GROUP = 16
SSM_T = 16
LANES = 128
VMEM_LIMIT = 56 * 1024 * 1024

TM = 512
TF = 512
TN = 512
BAND_TQ = 256
FOX_TQ = 256
FOX_TK = 512
SSM_ROWS = 256


def _params(*sem):
    return pltpu.CompilerParams(dimension_semantics=sem, vmem_limit_bytes=VMEM_LIMIT)


def _rms(x, g):
    return x * lax.rsqrt(jnp.mean(x * x, axis=-1, keepdims=True) + EPS) * g


def _dot(a, b):
    return jnp.dot(a, b, preferred_element_type=F32)


def _dot_nt(a, b):
    return lax.dot_general(a, b, (((1,), (1,)), ((), ())), preferred_element_type=F32)


def _ffn_body(x_ref, g_ref, wg_ref, wu_ref, wo_ref, o_ref, h_ref, acc_ref):
    f = pl.program_id(1)

    @pl.when(f == 0)
    def _():
        h_ref[...] = _rms(x_ref[...], g_ref[...]).astype(BF16)
        acc_ref[...] = jnp.zeros_like(acc_ref)

    h = h_ref[...]
    gate = _dot(h, wg_ref[...])
    up = _dot(h, wu_ref[...])
    act = (gate * jax.nn.sigmoid(gate) * up).astype(BF16)
    acc_ref[...] += _dot(act, wo_ref[...])

    @pl.when(f == pl.num_programs(1) - 1)
    def _():
        o_ref[...] = x_ref[...] + 0.5 * acc_ref[...]


def ffn_half_step(x, g, w_in, w_out):
    m, d = x.shape
    f = w_out.shape[0]
    tm = min(TM, m)
    tf = min(TF, f)
    nf = f // tf
    return pl.pallas_call(
        _ffn_body,
        grid=(m // tm, nf),
        in_specs=[
            pl.BlockSpec((tm, d), lambda i, j: (i, 0)),
            pl.BlockSpec((1, d), lambda i, j: (0, 0)),
            pl.BlockSpec((d, tf), lambda i, j: (0, j)),
            pl.BlockSpec((d, tf), lambda i, j: (0, j + nf)),
            pl.BlockSpec((tf, d), lambda i, j: (j, 0)),
        ],
        out_specs=pl.BlockSpec((tm, d), lambda i, j: (i, 0)),
        out_shape=jax.ShapeDtypeStruct((m, d), F32),
        scratch_shapes=[pltpu.VMEM((tm, d), BF16), pltpu.VMEM((tm, d), F32)],
        compiler_params=_params("parallel", "arbitrary"),
        name="ffn_half_step",
    )(x, g.reshape(1, d), w_in, w_in, w_out)


def _rmsnorm_body(x_ref, g_ref, o_ref):
    o_ref[...] = _rms(x_ref[...], g_ref[...]).astype(o_ref.dtype)


def rmsnorm(x, g):
    m, d = x.shape
    tm = min(TM, m)
    return pl.pallas_call(
        _rmsnorm_body,
        grid=(m // tm,),
        in_specs=[pl.BlockSpec((tm, d), lambda i: (i, 0)),
                  pl.BlockSpec((1, d), lambda i: (0, 0))],
        out_specs=pl.BlockSpec((tm, d), lambda i: (i, 0)),
        out_shape=jax.ShapeDtypeStruct((m, d), F32),
        compiler_params=_params("parallel"),
        name="rmsnorm",
    )(x, g.reshape(1, d))


def rmsnorm_time_major(x, g, nb, length):
    d = x.shape[1]
    tm = min(TM, length)
    nlt = length // tm
    return pl.pallas_call(
        _rmsnorm_body,
        grid=(nb, nlt),
        in_specs=[pl.BlockSpec((tm, d), lambda b, l: (b * nlt + l, 0)),
                  pl.BlockSpec((1, d), lambda b, l: (0, 0))],
        out_specs=pl.BlockSpec((tm, d), lambda b, l: (l, b)),
        out_shape=jax.ShapeDtypeStruct((length, nb * d), F32),
        compiler_params=_params("parallel", "parallel"),
        name="rmsnorm_time_major",
    )(x, g.reshape(1, d))


def _qkv_body(scale, has_forget, *refs):
    if has_forget:
        (x_ref, g_ref, wq_ref, wk_ref, wv_ref, wf_ref, bf_ref,
         q_ref, k32_ref, v32_ref, kb_ref, vb_ref, lf_ref, h_ref) = refs
    else:
        (x_ref, g_ref, wq_ref, wk_ref, wv_ref,
         q_ref, k32_ref, v32_ref, kb_ref, vb_ref, h_ref) = refs

    @pl.when(pl.program_id(1) == 0)
    def _():
        h = _rms(x_ref[...], g_ref[...]).astype(BF16)
        h_ref[...] = h
        if has_forget:
            z = _dot_nt(wf_ref[...], h) + bf_ref[...]
            lf_ref[...] = jnp.minimum(z, 0.0) - jnp.log(1.0 + jnp.exp(-jnp.abs(z)))

    h = h_ref[...]
    q_ref[...] = (_dot(h, wq_ref[...]) * scale).astype(BF16)
    k = _dot(h, wk_ref[...])
    k32_ref[...] = k
    kb_ref[...] = k.astype(BF16)
    v = _dot(h, wv_ref[...])
    v32_ref[...] = v
    vb_ref[...] = v.astype(BF16)


def qkv_project(x, g, w, scale, wf_t=None, b_f=None):
    m, d = x.shape
    tm = min(TM, m)
    tn = min(TN, d)
    nq = d // tn
    has_forget = wf_t is not None
    in_specs = [
        pl.BlockSpec((tm, d), lambda i, n: (i, 0)),
        pl.BlockSpec((1, d), lambda i, n: (0, 0)),
        pl.BlockSpec((d, tn), lambda i, n: (0, n)),
        pl.BlockSpec((d, tn), lambda i, n: (0, n + nq)),
        pl.BlockSpec((d, tn), lambda i, n: (0, n + 2 * nq)),
    ]
    args = [x, g.reshape(1, d), w, w, w]
    tile = pl.BlockSpec((tm, tn), lambda i, n: (i, n))
    out_specs = [tile] * 5
    out_shape = [jax.ShapeDtypeStruct((m, d), BF16),
                 jax.ShapeDtypeStruct((m, d), F32), jax.ShapeDtypeStruct((m, d), F32),
                 jax.ShapeDtypeStruct((m, d), BF16), jax.ShapeDtypeStruct((m, d), BF16)]
    if has_forget:
        nh = wf_t.shape[0]
        in_specs += [pl.BlockSpec((nh, d), lambda i, n: (0, 0)),
                     pl.BlockSpec((nh, 1), lambda i, n: (0, 0))]
        args += [wf_t, b_f.reshape(nh, 1)]
        out_specs = out_specs + [pl.BlockSpec((nh, tm), lambda i, n: (0, i))]
        out_shape = out_shape + [jax.ShapeDtypeStruct((nh, m), F32)]
    return pl.pallas_call(
        functools.partial(_qkv_body, scale, has_forget),
        grid=(m // tm, nq),
        in_specs=in_specs,
        out_specs=out_specs,
        out_shape=out_shape,
        scratch_shapes=[pltpu.VMEM((tm, d), BF16)],
        compiler_params=_params("parallel", "arbitrary"),
        name="qkv_project",
    )(*args)


def _out_proj_body(x_ref, o_ref, w_ref, y_ref):
    y_ref[...] = x_ref[...] + _dot(o_ref[...], w_ref[...])


def out_project(x, o, w):
    m, d = x.shape
    tm = min(TM, m)
    return pl.pallas_call(
        _out_proj_body,
        grid=(m // tm,),
        in_specs=[pl.BlockSpec((tm, d), lambda i: (i, 0)),
                  pl.BlockSpec((tm, d), lambda i: (i, 0)),
                  pl.BlockSpec((d, d), lambda i: (0, 0))],
        out_specs=pl.BlockSpec((tm, d), lambda i: (i, 0)),
        out_shape=jax.ShapeDtypeStruct((m, d), F32),
        compiler_params=_params("parallel"),
        name="out_project",
    )(x, o, w)


def _glu_body(x_ref, y_ref, g_ref, d_ref, xt_ref, wv_ref, wg_ref, o_ref, a_ref):
    @pl.when(pl.program_id(2) == 0)
    def _():
        h = _rms(x_ref[...], g_ref[...])
        a_ref[...] = jax.nn.gelu(y_ref[...] + d_ref[...] * h).astype(BF16)

    a = a_ref[...]
    val = _dot(a, wv_ref[...])
    gate = _dot(a, wg_ref[...])
    o_ref[...] = xt_ref[...] + val * jax.nn.sigmoid(gate)


def ssm_glu_out(x, y_tm, g, d_skip, w_glu, nb, length):
    d = x.shape[1]
    tm = min(TM, length)
    nlt = length // tm
    tn = min(TN, d)
    nq = d // tn
    row = lambda b, l, n: (b * nlt + l, 0)
    return pl.pallas_call(
        _glu_body,
        grid=(nb, nlt, nq),
        in_specs=[
            pl.BlockSpec((tm, d), row),
            pl.BlockSpec((tm, d), lambda b, l, n: (l, b)),
            pl.BlockSpec((1, d), lambda b, l, n: (0, 0)),
            pl.BlockSpec((1, d), lambda b, l, n: (0, 0)),
            pl.BlockSpec((tm, tn), lambda b, l, n: (b * nlt + l, n)),
            pl.BlockSpec((d, tn), lambda b, l, n: (0, n)),
            pl.BlockSpec((d, tn), lambda b, l, n: (0, n + nq)),
        ],
        out_specs=pl.BlockSpec((tm, tn), lambda b, l, n: (b * nlt + l, n)),
        out_shape=jax.ShapeDtypeStruct((nb * length, d), F32),
        scratch_shapes=[pltpu.VMEM((tm, d), BF16)],
        compiler_params=_params("parallel", "parallel", "arbitrary"),
        name="ssm_glu_out",
    )(x, y_tm, g.reshape(1, d), d_skip.reshape(1, d), x, w_glu, w_glu)


def _ssm_lag_body(ca_ref, bb_ref, k_ref):
    for i in range(ca_ref.shape[0]):
        k_ref[i] = jnp.dot(ca_ref[i], bb_ref[i], preferred_element_type=F32,
                           precision=lax.Precision.HIGHEST)


def ssm_lag_kernels(ca, bb, gpb):
    g, r, p2 = ca.shape
    c = bb.shape[2]
    return pl.pallas_call(
        _ssm_lag_body,
        grid=(g // gpb,),
        in_specs=[pl.BlockSpec((gpb, r, p2), lambda i: (i, 0, 0)),
                  pl.BlockSpec((gpb, p2, c), lambda i: (i, 0, 0))],
        out_specs=pl.BlockSpec((gpb, r, c), lambda i: (i, 0, 0)),
        out_shape=jax.ShapeDtypeStruct((g, r, c), F32),
        compiler_params=_params("parallel"),
        name="ssm_lag_kernels",
    )(ca, bb)


def _ssm_core_body(nb, x_ref, bdv_ref, ws_ref, wc_ref, at_ref, s0_ref, y_ref, sf_ref,
                   ds_ref, sst_ref, st_ref):
    nchp, t = x_ref.shape[0], x_ref.shape[1]
    m = nchp * nb
    half = at_ref.shape[2]

    @pl.when(pl.program_id(1) == 0)
    def _():
        st_ref[...] = s0_ref[0]

    xcat = jnp.concatenate(
        [x_ref[:, tau].reshape(m, LANES).astype(BF16) for tau in range(t)], axis=1)
    ds_ref[...] = _dot(xcat, ws_ref[0])
    ar = jnp.broadcast_to(at_ref[0, 0:1, :], (nb, half))
    ai = jnp.broadcast_to(at_ref[0, 1:2, :], (nb, half))

    def step(k, carry):
        sr, si = carry
        rows = pl.ds(pl.multiple_of(k * nb, nb), nb)
        sst_ref[rows, 0:half] = sr
        sst_ref[rows, half:] = si
        nr = ar * sr - ai * si + ds_ref[rows, 0:half]
        ni = ar * si + ai * sr + ds_ref[rows, half:]
        return nr, ni

    sr, si = lax.fori_loop(0, nchp, step, (st_ref[:, 0:half], st_ref[:, half:]))
    st_ref[:, 0:half] = sr
    st_ref[:, half:] = si
    sf_ref[0] = st_ref[...]
    y_in = _dot(sst_ref[...].astype(BF16), wc_ref[0])
    for tt in range(t):
        yt = y_in[:, tt * LANES:(tt + 1) * LANES] + _dot(
            xcat[:, :(tt + 1) * LANES], bdv_ref[0, (t - 1 - tt) * LANES:, :])
        y_ref[:, tt] = yt.reshape(nchp, nb, LANES)


def ssm_core(h_tm, mats, s0, nb, length):
    bdv, ws, wc, at = mats
    d = h_tm.shape[1] // nb
    t = SSM_T
    nch = length // t
    nblk = d // LANES
    nchp = max(min(SSM_ROWS // nb, nch), 1)
    m = nchp * nb
    ncol = ws.shape[2]
    x4 = h_tm.reshape(nch, t, nb, d)
    wblk = lambda *s: pl.BlockSpec((1,) + s, lambda c, k: (c, 0, 0))
    seq = pl.BlockSpec((nchp, t, nb, LANES), lambda c, k: (k, 0, 0, c))
    y, sf = pl.pallas_call(
        functools.partial(_ssm_core_body, nb),
        grid=(nblk, nch // nchp),
        in_specs=[seq, wblk(t * LANES, LANES), wblk(t * LANES, ncol), wblk(ncol, t * LANES),
                  wblk(2, ncol // 2), wblk(nb, ncol)],
        out_specs=[seq, wblk(nb, ncol)],
        out_shape=[jax.ShapeDtypeStruct((nch, t, nb, d), F32),
                   jax.ShapeDtypeStruct((nblk, nb, ncol), F32)],
        scratch_shapes=[pltpu.VMEM((m, ncol), F32), pltpu.VMEM((m, ncol), F32),
                        pltpu.VMEM((nb, ncol), F32)],
        compiler_params=_params("parallel", "arbitrary"),
        name="ssm_core",
    )(x4, bdv, ws, wc, at, s0)
    return y.reshape(length, nb * d), sf


def ssm_matrices(a_re, a_im, log_step, b_re, b_im, c_re, c_im):
    t = SSM_T
    g, p = a_re.shape
    c = b_re.shape[2]
    gpb = LANES // c
    nblk = g // gpb
    dt = jnp.exp(log_step)[:, None]
    lam_re, lam_im = a_re * dt, a_im * dt
    j = jnp.arange(t + 1, dtype=F32)[:, None, None]
    mag = jnp.exp(lam_re * j)
    pw_re, pw_im = mag * jnp.cos(lam_im * j), mag * jnp.sin(lam_im * j)
    ab_re, ab_im = pw_re[1], pw_im[1]
    den = a_re * a_re + a_im * a_im
    z_re = ((ab_re - 1.0) * a_re + ab_im * a_im) / den
    z_im = (ab_im * a_re - (ab_re - 1.0) * a_im) / den
    bb_re = z_re[..., None] * b_re - z_im[..., None] * b_im
    bb_im = z_re[..., None] * b_im + z_im[..., None] * b_re
    eye = jnp.eye(gpb, dtype=F32)

    jr = (t - 1) - jnp.arange(t, dtype=F32)[:, None, None]
    mag_r = jnp.exp(lam_re * jr)
    rp_re, rp_im = mag_r * jnp.cos(lam_im * jr), mag_r * jnp.sin(lam_im * jr)
    ws_re = rp_re[:, :, :, None] * bb_re[None] - rp_im[:, :, :, None] * bb_im[None]
    ws_im = rp_re[:, :, :, None] * bb_im[None] + rp_im[:, :, :, None] * bb_re[None]

    def ws_expand(w):
        w = w.reshape(t, nblk, gpb, p, c).transpose(1, 0, 2, 4, 3)
        w = w[:, :, :, :, None, :] * eye[None, None, :, None, :, None]
        return w.reshape(nblk, t * LANES, gpb * p)

    ws = jnp.concatenate([ws_expand(ws_re), ws_expand(ws_im)], axis=2)

    def out_weights(q_re, q_im):
        cr, ci = c_re[None], c_im[None]
        qr, qi = q_re[:, :, None, :], q_im[:, :, None, :]
        return cr * qr - ci * qi, -cr * qi - ci * qr

    wc_re, wc_im = out_weights(pw_re[1:], pw_im[1:])

    def wc_expand(w):
        w = w.reshape(t, nblk, gpb, c, p).transpose(1, 2, 4, 0, 3)
        w = w[:, :, :, :, None, :] * eye[None, :, None, None, :, None]
        return w.reshape(nblk, gpb * p, t * LANES)

    wc = jnp.concatenate([wc_expand(wc_re), wc_expand(wc_im)], axis=1)

    ca_re, ca_im = out_weights(rp_re, rp_im)
    ca = jnp.concatenate([ca_re, ca_im], axis=-1).transpose(1, 0, 2, 3).reshape(g, t * c, 2 * p)
    bb = jnp.concatenate([bb_re, bb_im], axis=1)
    kmat = ssm_lag_kernels(ca, bb, gpb).reshape(nblk, gpb, t, c, c)
    bd = kmat.transpose(0, 2, 1, 4, 3)
    bd = bd[:, :, :, :, None, :] * eye[None, None, :, None, :, None]
    bdv = bd.reshape(nblk, t * LANES, LANES)

    at = jnp.stack([pw_re[t], pw_im[t]], axis=1)
    at = at.reshape(nblk, gpb, 2, p).transpose(0, 2, 1, 3).reshape(nblk, 2, gpb * p)
    return bdv.astype(BF16), ws.astype(BF16), wc.astype(BF16), at


def ssm_pack_state(s_re, s_im, nblk):
    nb = s_re.shape[0]
    f = lambda s: s.reshape(nb, nblk, -1).transpose(1, 0, 2)
    return jnp.concatenate([f(s_re), f(s_im)], axis=2)


def ssm_unpack_state(sf, g, p):
    nblk, nb, ncol = sf.shape
    f = lambda s: s.transpose(1, 0, 2).reshape(nb, g, p)
    return f(sf[:, :, :ncol // 2]), f(sf[:, :, ncol // 2:])


def _split3(x):
    hi = x.astype(BF16)
    r = x - hi.astype(F32)
    mid = r.astype(BF16)
    lo = (r - mid.astype(F32)).astype(BF16)
    return hi, mid, lo


def _cumsum_body(x_ref, o_ref):
    nblk = x_ref.shape[2] // LANES
    rows = x_ref.shape[1]
    row = lax.broadcasted_iota(jnp.int32, (LANES, LANES), 0)
    col = lax.broadcasted_iota(jnp.int32, (LANES, LANES), 1)
    tri = jnp.where(row <= col, 1.0, 0.0).astype(BF16)
    carry = jnp.zeros((rows, 1), F32)
    for j in range(nblk):
        sl = slice(j * LANES, (j + 1) * LANES)
        hi, mid, lo = _split3(x_ref[0, :, sl])
        w = _dot(hi, tri) + _dot(mid, tri) + _dot(lo, tri)
        o_ref[0, :, sl] = w + carry
        carry = carry + jnp.sum(x_ref[0, :, sl], axis=1, keepdims=True)


def cumsum_time(x):
    n, h, length = x.shape
    return pl.pallas_call(
        _cumsum_body,
        grid=(n,),
        in_specs=[pl.BlockSpec((1, h, length), lambda i: (i, 0, 0))],
        out_specs=pl.BlockSpec((1, h, length), lambda i: (i, 0, 0)),
        out_shape=jax.ShapeDtypeStruct((n, h, length), F32),
        compiler_params=_params("parallel"),
        name="cumsum_time",
    )(x)


def _softmax_pv(parts):
    m = functools.reduce(jnp.maximum, [jnp.max(s, axis=-1, keepdims=True) for s, _ in parts])
    acc, den = None, None
    for s, v in parts:
        e = jnp.exp(s - m)
        l = jnp.sum(e, axis=-1, keepdims=True)
        o = _dot(e.astype(BF16), v)
        acc = o if acc is None else acc + o
        den = l if den is None else den + l
    return acc / den


def _causal(n):
    rpos = lax.broadcasted_iota(jnp.int32, (n, n), 0)
    cpos = lax.broadcasted_iota(jnp.int32, (n, n), 1)
    return cpos <= rpos


def _fox_prompt_body(tq, q_ref, k_ref, v_ref, c_ref, o_ref):
    length = q_ref.shape[0]
    keep = _causal(tq)
    for i in range(length // tq):
        lo, hi = i * tq, (i + 1) * tq
        q = q_ref[lo:hi, :]
        parts = []
        if i > 0:
            parts.append((_dot_nt(q, k_ref[0:lo, :]) - c_ref[0, :, 0:lo], v_ref[0:lo, :]))
        s = _dot_nt(q, k_ref[lo:hi, :]) - c_ref[0, :, lo:hi]
        parts.append((jnp.where(keep, s, NEG_INF), v_ref[lo:hi, :]))
        o_ref[lo:hi, :] = _softmax_pv(parts).astype(o_ref.dtype)


def fox_prompt_attention(q, kb, vb, c, nb, nh):
    m, d = q.shape
    length = m // nb
    hd = d // nh
    tq = min(FOX_TQ, length)
    seq = pl.BlockSpec((length, hd), lambda b, h: (b, h))
    return pl.pallas_call(
        functools.partial(_fox_prompt_body, tq),
        grid=(nb, nh),
        in_specs=[seq, seq, seq, pl.BlockSpec((1, 1, length), lambda b, h: (b * nh + h, 0, 0))],
        out_specs=seq,
        out_shape=jax.ShapeDtypeStruct((m, d), BF16),
        compiler_params=_params("parallel", "parallel"),
        name="fox_prompt_attention",
    )(q, kb, vb, c)


def _fox_sample_body(nh, q_ref, kc_ref, vc_ref, kn_ref, vn_ref, cc_ref, cn_ref, o_ref,
                     m_ref, l_ref, acc_ref):
    j = pl.program_id(1)
    lq, d = q_ref.shape
    hd = d // nh

    @pl.when(j == 0)
    def _():
        m_ref[...] = jnp.full_like(m_ref, NEG_INF)
        l_ref[...] = jnp.zeros_like(l_ref)
        acc_ref[...] = jnp.zeros_like(acc_ref)

    def update(h, s, v):
        cols = slice(h * hd, (h + 1) * hd)
        m_old = m_ref[h]
        m_new = jnp.maximum(m_old, jnp.max(s, axis=-1, keepdims=True))
        alpha = jnp.exp(m_old - m_new)
        e = jnp.exp(s - m_new)
        l_ref[h] = alpha * l_ref[h] + jnp.sum(e, axis=-1, keepdims=True)
        acc_ref[:, cols] = alpha * acc_ref[:, cols] + _dot(e.astype(BF16), v)
        m_ref[h] = m_new

    for h in range(nh):
        cols = slice(h * hd, (h + 1) * hd)
        s = _dot_nt(q_ref[:, cols], kc_ref[0, :, cols].astype(BF16)) - cc_ref[0, h:h + 1, :]
        update(h, s, vc_ref[0, :, cols].astype(BF16))

    @pl.when(j == pl.num_programs(1) - 1)
    def _():
        keep = _causal(lq)
        for h in range(nh):
            cols = slice(h * hd, (h + 1) * hd)
            s = _dot_nt(q_ref[:, cols], kn_ref[:, cols]) - cn_ref[0, h:h + 1, 0:lq]
            update(h, jnp.where(keep, s, NEG_INF), vn_ref[:, cols])
            o_ref[:, cols] = (acc_ref[:, cols] / l_ref[h]).astype(o_ref.dtype)


def fox_sample_attention(q, k_cache, v_cache, kb, vb, c, nb, nh):
    m, d = q.shape
    lq = m // nb
    past = k_cache.shape[1]
    tk = min(FOX_TK, past)
    nkv = past // tk
    new = pl.BlockSpec((lq, d), lambda b, j: (b, 0))
    cache = pl.BlockSpec((1, tk, d), lambda b, j: (b, j, 0))
    return pl.pallas_call(
        functools.partial(_fox_sample_body, nh),
        grid=(nb, nkv),
        in_specs=[new, cache, cache, new, new,
                  pl.BlockSpec((1, nh, tk), lambda b, j: (b, 0, j)),
                  pl.BlockSpec((1, nh, LANES), lambda b, j: (b, 0, past // LANES))],
        out_specs=new,
        out_shape=jax.ShapeDtypeStruct((m, d), BF16),
        scratch_shapes=[pltpu.VMEM((nh, lq, 1), F32), pltpu.VMEM((nh, lq, 1), F32),
                        pltpu.VMEM((lq, d), F32)],
        compiler_params=_params("parallel", "arbitrary"),
        name="fox_sample_attention",
    )(q, k_cache, v_cache, kb, vb, c, c)


def _band_prompt_body(tq, q_ref, k_ref, v_ref, bias_ref, o_ref):
    length = q_ref.shape[0]
    span = bias_ref.shape[2]
    for i in range(length // tq):
        k0 = max((i + 1) * tq - span, 0)
        k1 = (i + 1) * tq
        q = q_ref[i * tq:(i + 1) * tq, :]
        s = _dot_nt(q, k_ref[k0:k1, :]) + bias_ref[0, :, span - (k1 - k0):span]
        o_ref[i * tq:(i + 1) * tq, :] = _softmax_pv([(s, v_ref[k0:k1, :])]).astype(o_ref.dtype)


def band_prompt_attention(q, kb, vb, bias, nb, nh):
    m, d = q.shape
    length = m // nb
    hd = d // nh
    tq = bias.shape[1]
    seq = pl.BlockSpec((length, hd), lambda b, h: (b, h))
    return pl.pallas_call(
        functools.partial(_band_prompt_body, tq),
        grid=(nb, nh),
        in_specs=[seq, seq, seq, pl.BlockSpec((1,) + bias.shape[1:], lambda b, h: (h, 0, 0))],
        out_specs=seq,
        out_shape=jax.ShapeDtypeStruct((m, d), BF16),
        compiler_params=_params("parallel", "parallel"),
        name="band_prompt_attention",
    )(q, kb, vb, bias)


def _band_sample_body(nh, q_ref, kc_ref, vc_ref, kn_ref, vn_ref, bias_ref, o_ref):
    lq, d = q_ref.shape
    hd = d // nh
    lc = kc_ref.shape[1]
    for h in range(nh):
        cols = slice(h * hd, (h + 1) * hd)
        q = q_ref[:, cols]
        s_c = _dot_nt(q, kc_ref[0, :, cols].astype(BF16)) + bias_ref[h, :, 0:lc]
        s_n = _dot_nt(q, kn_ref[:, cols]) + bias_ref[h, :, lc:lc + lq]
        o = _softmax_pv([(s_c, vc_ref[0, :, cols].astype(BF16)), (s_n, vn_ref[:, cols])])
        o_ref[:, cols] = o.astype(o_ref.dtype)


def band_sample_attention(q, k_cache, v_cache, kb, vb, bias, nb, nh):
    m, d = q.shape
    lq = m // nb
    lc = k_cache.shape[1]
    new = pl.BlockSpec((lq, d), lambda b: (b, 0))
    cache = pl.BlockSpec((1, lc, d), lambda b: (b, 0, 0))
    return pl.pallas_call(
        functools.partial(_band_sample_body, nh),
        grid=(nb,),
        in_specs=[new, cache, cache, new, new, pl.BlockSpec(bias.shape, lambda b: (0, 0, 0))],
        out_specs=new,
        out_shape=jax.ShapeDtypeStruct((m, d), BF16),
        compiler_params=_params("parallel"),
        name="band_sample_attention",
    )(q, k_cache, v_cache, kb, vb, bias)


def band_bias_tile(rel_bias, tq, window):
    qpos = jnp.arange(tq)[:, None]
    kpos = jnp.arange(window + tq)[None, :] - window
    idx = jnp.clip(qpos - kpos, -REL_CLIP, REL_CLIP) + REL_CLIP
    qc = qpos // CHUNK
    kc = jnp.floor_divide(kpos, CHUNK)
    visible = (kc <= qc) & (kc >= qc - BAND_PREV)
    return jnp.where(visible[None], rel_bias.astype(F32)[:, idx], NEG_INF)


def kernel(x_prompt, x_sample, state_ssm_re, state_ssm_im, cache_fox_k, cache_fox_v, cache_fox_logf,
           cache_band_k, cache_band_v, ffn1_norm, ffn1_w_in, ffn1_w_out, mix_norm, ffn2_norm, ffn2_w_in,
           ffn2_w_out, ssm_a_re, ssm_a_im, ssm_log_step, ssm_b_re, ssm_b_im, ssm_c_re, ssm_c_im, ssm_d,
           ssm_w_glu, fox_w_in, fox_b_f, fox_w_out, band_w_in, band_rel_bias, band_w_out, final_norm):
    bp, lp, d = x_prompt.shape
    bs, ls, _ = x_sample.shape
    depth = ffn1_norm.shape[0]
    nh = fox_b_f.shape[1]
    hd = d // nh
    scale = hd ** -0.5
    past = cache_fox_k.shape[2]
    window = cache_band_k.shape[2]
    heads = lambda t, nb, length: t.reshape(nb, length, nh, hd)

    xp = x_prompt.reshape(bp * lp, d)
    xs = x_sample.reshape(bs * ls, d)
    outs = {name: [] for name in (
        "p_ssm_re", "p_ssm_im", "p_fox_k", "p_fox_v", "p_fox_logf", "p_band_k", "p_band_v",
        "s_ssm_re", "s_ssm_im", "s_fox_k", "s_fox_v", "s_fox_logf", "s_band_k", "s_band_v")}

    for i in range(depth):
        kind, j = i % 3, i // 3
        w_in, w_out = ffn1_w_in[i].astype(BF16), ffn1_w_out[i].astype(BF16)
        xp = ffn_half_step(xp, ffn1_norm[i], w_in, w_out)
        xs = ffn_half_step(xs, ffn1_norm[i], w_in, w_out)
        if kind == 0:
            mats = ssm_matrices(ssm_a_re[j], ssm_a_im[j], ssm_log_step[j], ssm_b_re[j], ssm_b_im[j],
                                ssm_c_re[j], ssm_c_im[j])
            g, p = ssm_a_re.shape[1:]
            nblk = d // LANES
            w_glu = ssm_w_glu[j].astype(BF16)
            zeros = jnp.zeros((bp, g, p), F32)
            streams = ((xp, bp, lp, zeros, zeros, "p"), (xs, bs, ls, state_ssm_re[j], state_ssm_im[j], "s"))
            new_x = []
            for x, nb, length, s_re, s_im, tag in streams:
                h_tm = rmsnorm_time_major(x, mix_norm[i], nb, length)
                y_tm, sf = ssm_core(h_tm, mats, ssm_pack_state(s_re, s_im, nblk), nb, length)
                f_re, f_im = ssm_unpack_state(sf, g, p)
                outs[tag + "_ssm_re"].append(f_re); outs[tag + "_ssm_im"].append(f_im)
                new_x.append(ssm_glu_out(x, y_tm, mix_norm[i], ssm_d[j], w_glu, nb, length))
            xp, xs = new_x
        elif kind == 1:
            w = fox_w_in[j]
            w_qkv = w[:, :3 * d].astype(BF16)
            wf_t = w[:, 3 * d:].T.astype(BF16)
            w_o = fox_w_out[j].astype(BF16)
            qp, kp, vp, kbp, vbp, lfp = qkv_project(xp, mix_norm[i], w_qkv, scale, wf_t, fox_b_f[j])
            qs, ks, vs, kbs, vbs, lfs = qkv_project(xs, mix_norm[i], w_qkv, scale, wf_t, fox_b_f[j])
            lfp = lfp.reshape(nh, bp, lp).transpose(1, 0, 2)
            lfs = lfs.reshape(nh, bs, ls).transpose(1, 0, 2)
            cp = cumsum_time(lfp).reshape(bp * nh, 1, lp)
            lf_all = jnp.concatenate([cache_fox_logf[j].astype(F32).transpose(0, 2, 1), lfs,
                                      jnp.zeros((bs, nh, LANES - ls), F32)], axis=2)
            cs = cumsum_time(lf_all)
            op = fox_prompt_attention(qp, kbp, vbp, cp, bp, nh)
            os_ = fox_sample_attention(qs, cache_fox_k[j].reshape(bs, past, d),
                                       cache_fox_v[j].reshape(bs, past, d), kbs, vbs, cs, bs, nh)
            xp = out_project(xp, op, w_o)
            xs = out_project(xs, os_, w_o)
            outs["p_fox_k"].append(heads(kp, bp, lp)); outs["p_fox_v"].append(heads(vp, bp, lp))
            outs["p_fox_logf"].append(lfp.transpose(0, 2, 1))
            outs["s_fox_k"].append(heads(ks, bs, ls)); outs["s_fox_v"].append(heads(vs, bs, ls))
            outs["s_fox_logf"].append(lfs.transpose(0, 2, 1))
        else:
            w_qkv = band_w_in[j].astype(BF16)
            w_o = band_w_out[j].astype(BF16)
            qp, kp, vp, kbp, vbp = qkv_project(xp, mix_norm[i], w_qkv, scale)
            qs, ks, vs, kbs, vbs = qkv_project(xs, mix_norm[i], w_qkv, scale)
            bw = BAND_PREV * CHUNK
            bias = band_bias_tile(band_rel_bias[j], BAND_TQ, bw)
            op = band_prompt_attention(qp, kbp, vbp, bias, bp, nh)
            os_ = band_sample_attention(qs, cache_band_k[j].reshape(bs, window, d),
                                        cache_band_v[j].reshape(bs, window, d), kbs, vbs,
                                        bias[:, :ls, bw - window:bw + ls], bs, nh)
            xp = out_project(xp, op, w_o)
            xs = out_project(xs, os_, w_o)
            keep = min(bw, lp)
            outs["p_band_k"].append(heads(kp, bp, lp)[:, lp - keep:])
            outs["p_band_v"].append(heads(vp, bp, lp)[:, lp - keep:])
            outs["s_band_k"].append(heads(ks, bs, ls)); outs["s_band_v"].append(heads(vs, bs, ls))
        w_in, w_out = ffn2_w_in[i].astype(BF16), ffn2_w_out[i].astype(BF16)
        xp = ffn_half_step(xp, ffn2_norm[i], w_in, w_out)
        xs = ffn_half_step(xs, ffn2_norm[i], w_in, w_out)

    y_prompt = rmsnorm(xp, final_norm).reshape(bp, lp, d)
    y_sample = rmsnorm(xs, final_norm).reshape(bs, ls, d)
    st = {k: jnp.stack(v) for k, v in outs.items()}
    return (y_prompt, y_sample, st["p_ssm_re"], st["p_ssm_im"], st["p_fox_k"], st["p_fox_v"],
            st["p_fox_logf"], st["p_band_k"], st["p_band_v"], st["s_ssm_re"], st["s_ssm_im"],
            st["s_fox_k"], st["s_fox_v"], st["s_fox_logf"], st["s_band_k"], st["s_band_v"])
```

```python
import functools

import jax
import jax.numpy as jnp
import numpy as np
from jax import lax
from jax.experimental import pallas as pl
from jax.experimental.pallas import tpu as pltpu

F32 = jnp.float32
BF16 = jnp.bfloat16

EPS = 1e-6
NEG_INF = -1e30
CHUNK = 64
BAND_PREV = 8
REL_CLIP = 256
SSM_GROUP = 16
SSM_T = 16
LANES = 128
VMEM_LIMIT = 56 * 1024 * 1024

TM = 512
TM_FFN = 1024
TF = 512
TN = 512
BAND_TQ = 256
FOX_TQ = 256
FOX_TK = 1024
SSM_ROWS = 256


def _params(*sem):
    return pltpu.CompilerParams(dimension_semantics=sem, vmem_limit_bytes=VMEM_LIMIT)


def _rms(x, g):
    return x * lax.rsqrt(jnp.mean(x * x, axis=-1, keepdims=True) + EPS) * g


def _dot(a, b):
    return jnp.dot(a, b, preferred_element_type=F32)


def _dot_nt(a, b):
    return lax.dot_general(a, b, (((1,), (1,)), ((), ())), preferred_element_type=F32)


def _ffn_body(x_ref, g_ref, wg_ref, wu_ref, wo_ref, o_ref, h_ref):
    f = pl.program_id(1)

    @pl.when(f == 0)
    def _():
        h_ref[...] = _rms(x_ref[...], g_ref[...]).astype(BF16)
        o_ref[...] = jnp.zeros_like(o_ref)

    h = h_ref[...]
    gate = _dot(h, wg_ref[...])
    up = _dot(h, wu_ref[...])
    act = (gate * jax.nn.sigmoid(gate) * up).astype(BF16)
    o_ref[...] += _dot(act, wo_ref[...])

    @pl.when(f == pl.num_programs(1) - 1)
    def _():
        o_ref[...] = x_ref[...] + 0.5 * o_ref[...]


def ffn_half_step(x, g, w_in, w_out, layer):
    m, d = x.shape
    f = w_out.shape[1]
    tm = min(TM_FFN, m)
    tf = min(TF, f)
    nf = f // tf
    return pl.pallas_call(
        _ffn_body,
        grid=(m // tm, nf),
        in_specs=[
            pl.BlockSpec((tm, d), lambda i, j: (i, 0)),
            pl.BlockSpec((1, d), lambda i, j: (0, 0)),
            pl.BlockSpec((None, d, tf), lambda i, j: (layer, 0, j)),
            pl.BlockSpec((None, d, tf), lambda i, j: (layer, 0, j + nf)),
            pl.BlockSpec((None, tf, d), lambda i, j: (layer, j, 0)),
        ],
        out_specs=pl.BlockSpec((tm, d), lambda i, j: (i, 0)),
        out_shape=jax.ShapeDtypeStruct((m, d), F32),
        scratch_shapes=[pltpu.VMEM((tm, d), BF16)],
        compiler_params=_params("parallel", "arbitrary"),
        name="ffn_half_step",
    )(x, g.reshape(1, d), w_in, w_in, w_out)


def _rmsnorm_body(x_ref, g_ref, o_ref):
    o_ref[...] = _rms(x_ref[...], g_ref[...]).astype(o_ref.dtype)


def rmsnorm(x, g):
    m, d = x.shape
    tm = min(TM, m)
    return pl.pallas_call(
        _rmsnorm_body,
        grid=(m // tm,),
        in_specs=[pl.BlockSpec((tm, d), lambda i: (i, 0)),
                  pl.BlockSpec((1, d), lambda i: (0, 0))],
        out_specs=pl.BlockSpec((tm, d), lambda i: (i, 0)),
        out_shape=jax.ShapeDtypeStruct((m, d), F32),
        compiler_params=_params("parallel"),
        name="rmsnorm",
    )(x, g.reshape(1, d))


def rmsnorm_time_major(x, g, nb, length):
    d = x.shape[1]
    tm = min(TM, length)
    nlt = length // tm
    return pl.pallas_call(
        _rmsnorm_body,
        grid=(nb, nlt),
        in_specs=[pl.BlockSpec((tm, d), lambda b, l: (b * nlt + l, 0)),
                  pl.BlockSpec((1, d), lambda b, l: (0, 0))],
        out_specs=pl.BlockSpec((tm, d), lambda b, l: (l, b)),
        out_shape=jax.ShapeDtypeStruct((length, nb * d), F32),
        compiler_params=_params("parallel", "parallel"),
        name="rmsnorm_time_major",
    )(x, g.reshape(1, d))


def _qkv_body(scale, has_forget, *refs):
    if has_forget:
        (x_ref, g_ref, wq_ref, wk_ref, wv_ref, wf_ref, bf_ref,
         q_ref, k32_ref, v32_ref, kb_ref, vb_ref, lf_ref, h_ref) = refs
    else:
        (x_ref, g_ref, wq_ref, wk_ref, wv_ref,
         q_ref, k32_ref, v32_ref, kb_ref, vb_ref, h_ref) = refs

    @pl.when(pl.program_id(1) == 0)
    def _():
        h = _rms(x_ref[...], g_ref[...]).astype(BF16)
        h_ref[...] = h
        if has_forget:
            z = _dot_nt(wf_ref[...], h) + bf_ref[...]
            lf_ref[...] = jnp.minimum(z, 0.0) - jnp.log(1.0 + jnp.exp(-jnp.abs(z)))

    h = h_ref[...]
    q_ref[...] = (_dot(h, wq_ref[...]) * scale).astype(BF16)
    k = _dot(h, wk_ref[...])
    k32_ref[...] = k
    kb_ref[...] = k.astype(BF16)
    v = _dot(h, wv_ref[...])
    v32_ref[...] = v
    vb_ref[...] = v.astype(BF16)


def qkv_project(x, g, w, layer, scale, wf_t=None, b_f=None):
    m, d = x.shape
    tm = min(TM, m)
    tn = min(TN, d)
    nq = d // tn
    has_forget = wf_t is not None
    in_specs = [
        pl.BlockSpec((tm, d), lambda i, n: (i, 0)),
        pl.BlockSpec((1, d), lambda i, n: (0, 0)),
        pl.BlockSpec((None, d, tn), lambda i, n: (layer, 0, n)),
        pl.BlockSpec((None, d, tn), lambda i, n: (layer, 0, n + nq)),
        pl.BlockSpec((None, d, tn), lambda i, n: (layer, 0, n + 2 * nq)),
    ]
    args = [x, g.reshape(1, d), w, w, w]
    tile = pl.BlockSpec((tm, tn), lambda i, n: (i, n))
    out_specs = [tile] * 5
    out_shape = [jax.ShapeDtypeStruct((m, d), BF16),
                 jax.ShapeDtypeStruct((m, d), F32), jax.ShapeDtypeStruct((m, d), F32),
                 jax.ShapeDtypeStruct((m, d), BF16), jax.ShapeDtypeStruct((m, d), BF16)]
    if has_forget:
        nh = wf_t.shape[0]
        in_specs += [pl.BlockSpec((nh, d), lambda i, n: (0, 0)),
                     pl.BlockSpec((nh, 1), lambda i, n: (0, 0))]
        args += [wf_t, b_f.reshape(nh, 1)]
        out_specs = out_specs + [pl.BlockSpec((nh, tm), lambda i, n: (0, i))]
        out_shape = out_shape + [jax.ShapeDtypeStruct((nh, m), F32)]
    return pl.pallas_call(
        functools.partial(_qkv_body, scale, has_forget),
        grid=(m // tm, nq),
        in_specs=in_specs,
        out_specs=out_specs,
        out_shape=out_shape,
        scratch_shapes=[pltpu.VMEM((tm, d), BF16)],
        compiler_params=_params("parallel", "arbitrary"),
        name="qkv_project",
    )(*args)


def _out_proj_body(x_ref, o_ref, w_ref, y_ref):
    y_ref[...] = x_ref[...] + _dot(o_ref[...], w_ref[...])


def out_project(x, o, w, layer):
    m, d = x.shape
    tm = min(TM, m)
    return pl.pallas_call(
        _out_proj_body,
        grid=(m // tm,),
        in_specs=[pl.BlockSpec((tm, d), lambda i: (i, 0)),
                  pl.BlockSpec((tm, d), lambda i: (i, 0)),
                  pl.BlockSpec((None, d, d), lambda i: (layer, 0, 0))],
        out_specs=pl.BlockSpec((tm, d), lambda i: (i, 0)),
        out_shape=jax.ShapeDtypeStruct((m, d), F32),
        compiler_params=_params("parallel"),
        name="out_project",
    )(x, o, w)


def _glu_body(a_ref, xt_ref, wv_ref, wg_ref, o_ref):
    a = a_ref[...].astype(BF16)
    val = _dot(a, wv_ref[...])
    gate = _dot(a, wg_ref[...])
    o_ref[...] = xt_ref[...] + val * jax.nn.sigmoid(gate)


def ssm_glu_out(x, a_tm, w_glu, layer, nb, length):
    d = x.shape[1]
    tm = min(TM, length)
    nlt = length // tm
    tn = min(TN, d)
    nq = d // tn
    tile = pl.BlockSpec((tm, tn), lambda n, b, l: (b * nlt + l, n))
    return pl.pallas_call(
        _glu_body,
        grid=(nq, nb, nlt),
        in_specs=[
            pl.BlockSpec((tm, d), lambda n, b, l: (l, b)),
            tile,
            pl.BlockSpec((None, d, tn), lambda n, b, l: (layer, 0, n)),
            pl.BlockSpec((None, d, tn), lambda n, b, l: (layer, 0, n + nq)),
        ],
        out_specs=tile,
        out_shape=jax.ShapeDtypeStruct((nb * length, d), F32),
        compiler_params=_params("parallel", "parallel", "parallel"),
        name="ssm_glu_out",
    )(a_tm, x, w_glu, w_glu)


def _ssm_lag_body(ca_ref, bb_ref, k_ref):
    for i in range(ca_ref.shape[0]):
        k_ref[i] = jnp.dot(ca_ref[i], bb_ref[i], preferred_element_type=F32,
                           precision=lax.Precision.HIGHEST)


def ssm_lag_kernels(ca, bb, gpb):
    g, r, p2 = ca.shape
    c = bb.shape[2]
    return pl.pallas_call(
        _ssm_lag_body,
        grid=(g // gpb,),
        in_specs=[pl.BlockSpec((gpb, r, p2), lambda i: (i, 0, 0)),
                  pl.BlockSpec((gpb, p2, c), lambda i: (i, 0, 0))],
        out_specs=pl.BlockSpec((gpb, r, c), lambda i: (i, 0, 0)),
        out_shape=jax.ShapeDtypeStruct((g, r, c), F32),
        compiler_params=_params("parallel"),
        name="ssm_lag_kernels",
    )(ca, bb)


def _ssm_core_body(nb, x_ref, bdv_ref, ws_ref, wc_ref, at_ref, s0_ref, dsk_ref, y_ref, sf_ref,
                   ds_ref, sst_ref, st_ref):
    nchp, t = x_ref.shape[0], x_ref.shape[1]
    m = nchp * nb
    half = at_ref.shape[2]

    @pl.when(pl.program_id(1) == 0)
    def _():
        st_ref[...] = s0_ref[0]

    xcat = jnp.concatenate(
        [x_ref[:, tau].reshape(m, LANES).astype(BF16) for tau in range(t)], axis=1)
    ds_ref[...] = _dot(xcat, ws_ref[0])
    ar = jnp.broadcast_to(at_ref[0, 0:1, :], (nb, half))
    ai = jnp.broadcast_to(at_ref[0, 1:2, :], (nb, half))

    def step(k, carry):
        sr, si = carry
        rows = pl.ds(pl.multiple_of(k * nb, nb), nb)
        sst_ref[rows, 0:half] = sr
        sst_ref[rows, half:] = si
        nr = ar * sr - ai * si + ds_ref[rows, 0:half]
        ni = ar * si + ai * sr + ds_ref[rows, half:]
        return nr, ni

    sr, si = lax.fori_loop(0, nchp, step, (st_ref[:, 0:half], st_ref[:, half:]))
    st_ref[:, 0:half] = sr
    st_ref[:, half:] = si
    sf_ref[0] = st_ref[...]
    y_in = _dot(sst_ref[...].astype(BF16), wc_ref[0])
    dsk = dsk_ref[...]
    for tt in range(t):
        yt = y_in[:, tt * LANES:(tt + 1) * LANES] + _dot(
            xcat[:, :(tt + 1) * LANES], bdv_ref[0, (t - 1 - tt) * LANES:, :])
        yt = jax.nn.gelu(yt + dsk * x_ref[:, tt].reshape(m, LANES))
        y_ref[:, tt] = yt.reshape(nchp, nb, LANES)


def ssm_core(h_tm, mats, s0, d_skip, nb, length):
    bdv, ws, wc, at = mats
    d = h_tm.shape[1] // nb
    t = SSM_T
    nch = length // t
    nblk = d // LANES
    nchp = max(min(SSM_ROWS // nb, nch), 1)
    m = nchp * nb
    ncol = ws.shape[2]
    x4 = h_tm.reshape(nch, t, nb, d)
    wblk = lambda *s: pl.BlockSpec((1,) + s, lambda c, k: (c, 0, 0))
    seq = pl.BlockSpec((nchp, t, nb, LANES), lambda c, k: (k, 0, 0, c))
    y, sf = pl.pallas_call(
        functools.partial(_ssm_core_body, nb),
        grid=(nblk, nch // nchp),
        in_specs=[seq, wblk(t * LANES, LANES), wblk(t * LANES, ncol), wblk(ncol, t * LANES),
                  wblk(2, ncol // 2), wblk(nb, ncol), pl.BlockSpec((1, LANES), lambda c, k: (0, c))],
        out_specs=[seq, wblk(nb, ncol)],
        out_shape=[jax.ShapeDtypeStruct((nch, t, nb, d), F32),
                   jax.ShapeDtypeStruct((nblk, nb, ncol), F32)],
        scratch_shapes=[pltpu.VMEM((m, ncol), F32), pltpu.VMEM((m, ncol), F32),
                        pltpu.VMEM((nb, ncol), F32)],
        compiler_params=_params("parallel", "arbitrary"),
        name="ssm_core",
    )(x4, bdv, ws, wc, at, s0, d_skip.reshape(1, d))
    return y.reshape(length, nb * d), sf


def ssm_matrices(a_re, a_im, log_step, b_re, b_im, c_re, c_im):
    t = SSM_T
    g, p = a_re.shape
    c = b_re.shape[2]
    gpb = LANES // c
    nblk = g // gpb
    dt = jnp.exp(log_step)[:, None]
    lam_re, lam_im = a_re * dt, a_im * dt
    j = jnp.arange(t + 1, dtype=F32)[:, None, None]
    mag = jnp.exp(lam_re * j)
    pw_re, pw_im = mag * jnp.cos(lam_im * j), mag * jnp.sin(lam_im * j)
    ab_re, ab_im = pw_re[1], pw_im[1]
    den = a_re * a_re + a_im * a_im
    z_re = ((ab_re - 1.0) * a_re + ab_im * a_im) / den
    z_im = (ab_im * a_re - (ab_re - 1.0) * a_im) / den
    bb_re = z_re[..., None] * b_re - z_im[..., None] * b_im
    bb_im = z_re[..., None] * b_im + z_im[..., None] * b_re
    eye = jnp.eye(gpb, dtype=F32)

    jr = (t - 1) - jnp.arange(t, dtype=F32)[:, None, None]
    mag_r = jnp.exp(lam_re * jr)
    rp_re, rp_im = mag_r * jnp.cos(lam_im * jr), mag_r * jnp.sin(lam_im * jr)
    ws_re = rp_re[:, :, :, None] * bb_re[None] - rp_im[:, :, :, None] * bb_im[None]
    ws_im = rp_re[:, :, :, None] * bb_im[None] + rp_im[:, :, :, None] * bb_re[None]

    def ws_expand(w):
        w = w.reshape(t, nblk, gpb, p, c).transpose(1, 0, 2, 4, 3)
        w = w[:, :, :, :, None, :] * eye[None, None, :, None, :, None]
        return w.reshape(nblk, t * LANES, gpb * p)

    ws = jnp.concatenate([ws_expand(ws_re), ws_expand(ws_im)], axis=2)

    def out_weights(q_re, q_im):
        cr, ci = c_re[None], c_im[None]
        qr, qi = q_re[:, :, None, :], q_im[:, :, None, :]
        return cr * qr - ci * qi, -cr * qi - ci * qr

    wc_re, wc_im = out_weights(pw_re[1:], pw_im[1:])

    def wc_expand(w):
        w = w.reshape(t, nblk, gpb, c, p).transpose(1, 2, 4, 0, 3)
        w = w[:, :, :, :, None, :] * eye[None, :, None, None, :, None]
        return w.reshape(nblk, gpb * p, t * LANES)

    wc = jnp.concatenate([wc_expand(wc_re), wc_expand(wc_im)], axis=1)

    ca_re, ca_im = out_weights(rp_re, rp_im)
    ca = jnp.concatenate([ca_re, ca_im], axis=-1).transpose(1, 0, 2, 3).reshape(g, t * c, 2 * p)
    bb = jnp.concatenate([bb_re, bb_im], axis=1)
    kmat = ssm_lag_kernels(ca, bb, gpb).reshape(nblk, gpb, t, c, c)
    bd = kmat.transpose(0, 2, 1, 4, 3)
    bd = bd[:, :, :, :, None, :] * eye[None, None, :, None, :, None]
    bdv = bd.reshape(nblk, t * LANES, LANES)

    at = jnp.stack([pw_re[t], pw_im[t]], axis=1)
    at = at.reshape(nblk, gpb, 2, p).transpose(0, 2, 1, 3).reshape(nblk, 2, gpb * p)
    return bdv.astype(BF16), ws.astype(BF16), wc.astype(BF16), at


def ssm_pack_state(s_re, s_im, nblk):
    nb = s_re.shape[0]
    f = lambda s: s.reshape(nb, nblk, -1).transpose(1, 0, 2)
    return jnp.concatenate([f(s_re), f(s_im)], axis=2)


def ssm_unpack_state(sf, g, p):
    nblk, nb, ncol = sf.shape
    f = lambda s: s.transpose(1, 0, 2).reshape(nb, g, p)
    return f(sf[:, :, :ncol // 2]), f(sf[:, :, ncol // 2:])


def _split3(x):
    hi = x.astype(BF16)
    r = x - hi.astype(F32)
    mid = r.astype(BF16)
    lo = (r - mid.astype(F32)).astype(BF16)
    return hi, mid, lo


def _cumsum_body(x_ref, o_ref):
    nblk = x_ref.shape[2] // LANES
    rows = x_ref.shape[1]
    row = lax.broadcasted_iota(jnp.int32, (LANES, LANES), 0)
    col = lax.broadcasted_iota(jnp.int32, (LANES, LANES), 1)
    tri = jnp.where(row <= col, 1.0, 0.0).astype(BF16)
    carry = jnp.zeros((rows, 1), F32)
    for j in range(nblk):
        sl = slice(j * LANES, (j + 1) * LANES)
        hi, mid, lo = _split3(x_ref[0, :, sl])
        w = _dot(hi, tri) + _dot(mid, tri) + _dot(lo, tri)
        o_ref[0, :, sl] = w + carry
        carry = carry + jnp.sum(x_ref[0, :, sl], axis=1, keepdims=True)


def cumsum_time(x):
    n, h, length = x.shape
    return pl.pallas_call(
        _cumsum_body,
        grid=(n,),
        in_specs=[pl.BlockSpec((1, h, length), lambda i: (i, 0, 0))],
        out_specs=pl.BlockSpec((1, h, length), lambda i: (i, 0, 0)),
        out_shape=jax.ShapeDtypeStruct((n, h, length), F32),
        compiler_params=_params("parallel"),
        name="cumsum_time",
    )(x)


def _softmax_pv(parts):
    m = functools.reduce(jnp.maximum, [jnp.max(s, axis=-1, keepdims=True) for s, _ in parts])
    acc, den = None, None
    for s, v in parts:
        e = jnp.exp(s - m)
        l = jnp.sum(e, axis=-1, keepdims=True)
        o = _dot(e.astype(BF16), v)
        acc = o if acc is None else acc + o
        den = l if den is None else den + l
    return acc / den


def _causal(n):
    rpos = lax.broadcasted_iota(jnp.int32, (n, n), 0)
    cpos = lax.broadcasted_iota(jnp.int32, (n, n), 1)
    return cpos <= rpos


def _fox_prompt_body(tq, q_ref, k_ref, v_ref, c_ref, o_ref):
    length = q_ref.shape[0]
    keep = _causal(tq)
    for i in range(length // tq):
        lo, hi = i * tq, (i + 1) * tq
        q = q_ref[lo:hi, :]
        parts = []
        if i > 0:
            parts.append((_dot_nt(q, k_ref[0:lo, :]) - c_ref[0, :, 0:lo], v_ref[0:lo, :]))
        s = _dot_nt(q, k_ref[lo:hi, :]) - c_ref[0, :, lo:hi]
        parts.append((jnp.where(keep, s, NEG_INF), v_ref[lo:hi, :]))
        o_ref[lo:hi, :] = _softmax_pv(parts).astype(o_ref.dtype)


def fox_prompt_attention(q, kb, vb, c, nb, nh):
    m, d = q.shape
    length = m // nb
    hd = d // nh
    tq = min(FOX_TQ, length)
    seq = pl.BlockSpec((length, hd), lambda b, h: (b, h))
    return pl.pallas_call(
        functools.partial(_fox_prompt_body, tq),
        grid=(nb, nh),
        in_specs=[seq, seq, seq, pl.BlockSpec((1, 1, length), lambda b, h: (b * nh + h, 0, 0))],
        out_specs=seq,
        out_shape=jax.ShapeDtypeStruct((m, d), BF16),
        compiler_params=_params("parallel", "parallel"),
        name="fox_prompt_attention",
    )(q, kb, vb, c)


def _fox_sample_body(nh, q_ref, kc_ref, vc_ref, kn_ref, vn_ref, cc_ref, cn_ref, o_ref,
                     m_ref, l_ref, acc_ref):
    j = pl.program_id(1)
    lq, d = q_ref.shape
    hd = d // nh

    @pl.when(j == 0)
    def _():
        m_ref[...] = jnp.full_like(m_ref, NEG_INF)
        l_ref[...] = jnp.zeros_like(l_ref)
        acc_ref[...] = jnp.zeros_like(acc_ref)

    def update(h, s, v):
        cols = slice(h * hd, (h + 1) * hd)
        m_old = m_ref[h]
        m_new = jnp.maximum(m_old, jnp.max(s, axis=-1, keepdims=True))
        alpha = jnp.exp(m_old - m_new)
        e = jnp.exp(s - m_new)
        l_ref[h] = alpha * l_ref[h] + jnp.sum(e, axis=-1, keepdims=True)
        acc_ref[:, cols] = alpha * acc_ref[:, cols] + _dot(e.astype(BF16), v)
        m_ref[h] = m_new

    scores = [_dot_nt(q_ref[:, h * hd:(h + 1) * hd], kc_ref[:, h, :].astype(BF16)) - cc_ref[0, h:h + 1, :]
              for h in range(nh)]
    for h in range(nh):
        update(h, scores[h], vc_ref[:, h, :].astype(BF16))

    @pl.when(j == pl.num_programs(1) - 1)
    def _():
        keep = _causal(lq)
        for h in range(nh):
            cols = slice(h * hd, (h + 1) * hd)
            s = _dot_nt(q_ref[:, cols], kn_ref[:, cols]) - cn_ref[0, h:h + 1, 0:lq]
            update(h, jnp.where(keep, s, NEG_INF), vn_ref[:, cols])
            o_ref[:, cols] = (acc_ref[:, cols] / l_ref[h]).astype(o_ref.dtype)


def fox_sample_attention(q, k_cache, v_cache, layer, kb, vb, c, nb, nh):
    m, d = q.shape
    lq = m // nb
    past, hd = k_cache.shape[2], k_cache.shape[4]
    tk = min(FOX_TK, past)
    nkv = past // tk
    new = pl.BlockSpec((lq, d), lambda b, j: (b, 0))
    cache = pl.BlockSpec((None, None, tk, nh, hd), lambda b, j: (layer, b, j, 0, 0))
    return pl.pallas_call(
        functools.partial(_fox_sample_body, nh),
        grid=(nb, nkv),
        in_specs=[new, cache, cache, new, new,
                  pl.BlockSpec((1, nh, tk), lambda b, j: (b, 0, j)),
                  pl.BlockSpec((1, nh, LANES), lambda b, j: (b, 0, past // LANES))],
        out_specs=new,
        out_shape=jax.ShapeDtypeStruct((m, d), BF16),
        scratch_shapes=[pltpu.VMEM((nh, lq, 1), F32), pltpu.VMEM((nh, lq, 1), F32),
                        pltpu.VMEM((lq, d), F32)],
        compiler_params=_params("parallel", "arbitrary"),
        name="fox_sample_attention",
    )(q, k_cache, v_cache, kb, vb, c, c)


def _band_prompt_body(tq, q_ref, k_ref, v_ref, bias_ref, o_ref):
    length = q_ref.shape[0]
    span = bias_ref.shape[2]
    for i in range(length // tq):
        k0 = max((i + 1) * tq - span, 0)
        k1 = (i + 1) * tq
        q = q_ref[i * tq:(i + 1) * tq, :]
        s = _dot_nt(q, k_ref[k0:k1, :]) + bias_ref[0, :, span - (k1 - k0):span]
        o_ref[i * tq:(i + 1) * tq, :] = _softmax_pv([(s, v_ref[k0:k1, :])]).astype(o_ref.dtype)


def band_prompt_attention(q, kb, vb, bias, nb, nh):
    m, d = q.shape
    length = m // nb
    hd = d // nh
    tq = bias.shape[1]
    seq = pl.BlockSpec((length, hd), lambda b, h: (b, h))
    return pl.pallas_call(
        functools.partial(_band_prompt_body, tq),
        grid=(nb, nh),
        in_specs=[seq, seq, seq, pl.BlockSpec((1,) + bias.shape[1:], lambda b, h: (h, 0, 0))],
        out_specs=seq,
        out_shape=jax.ShapeDtypeStruct((m, d), BF16),
        compiler_params=_params("parallel", "parallel"),
        name="band_prompt_attention",
    )(q, kb, vb, bias)


def _band_sample_body(nh, q_ref, kc_ref, vc_ref, kn_ref, vn_ref, bias_ref, o_ref):
    lq, d = q_ref.shape
    hd = d // nh
    lc = kc_ref.shape[0]
    for h in range(nh):
        cols = slice(h * hd, (h + 1) * hd)
        q = q_ref[:, cols]
        s_c = _dot_nt(q, kc_ref[:, h, :].astype(BF16)) + bias_ref[h, :, 0:lc]
        s_n = _dot_nt(q, kn_ref[:, cols]) + bias_ref[h, :, lc:lc + lq]
        o = _softmax_pv([(s_c, vc_ref[:, h, :].astype(BF16)), (s_n, vn_ref[:, cols])])
        o_ref[:, cols] = o.astype(o_ref.dtype)


def band_sample_attention(q, k_cache, v_cache, layer, kb, vb, bias, nb, nh):
    m, d = q.shape
    lq = m // nb
    lc, hd = k_cache.shape[2], k_cache.shape[4]
    new = pl.BlockSpec((lq, d), lambda b: (b, 0))
    cache = pl.BlockSpec((None, None, lc, nh, hd), lambda b: (layer, b, 0, 0, 0))
    return pl.pallas_call(
        functools.partial(_band_sample_body, nh),
        grid=(nb,),
        in_specs=[new, cache, cache, new, new, pl.BlockSpec(bias.shape, lambda b: (0, 0, 0))],
        out_specs=new,
        out_shape=jax.ShapeDtypeStruct((m, d), BF16),
        compiler_params=_params("parallel"),
        name="band_sample_attention",
    )(q, k_cache, v_cache, kb, vb, bias)


def band_bias_tile(rel_bias, tq, window):
    nh = rel_bias.shape[0]
    span = window + tq
    n = span + tq
    rel = (span - 1) - np.arange(n)
    diag = rel_bias.astype(F32)[:, np.clip(rel, -REL_CLIP, REL_CLIP) + REL_CLIP]
    skew = jnp.broadcast_to(diag[:, None, :], (nh, tq, n)).reshape(nh, tq * n)
    skew = skew[:, :tq * (n - 1)].reshape(nh, tq, n - 1)[:, :, tq - 1:tq - 1 + span]
    qpos = np.arange(tq)[:, None]
    kpos = np.arange(span)[None, :] - window
    qc, kc = qpos // CHUNK, kpos // CHUNK
    visible = (kc <= qc) & (kc >= qc - BAND_PREV)
    return jnp.where(visible[None], skew, NEG_INF)


def kernel(x_prompt, x_sample, state_ssm_re, state_ssm_im, cache_fox_k, cache_fox_v, cache_fox_logf,
           cache_band_k, cache_band_v, ffn1_norm, ffn1_w_in, ffn1_w_out, mix_norm, ffn2_norm, ffn2_w_in,
           ffn2_w_out, ssm_a_re, ssm_a_im, ssm_log_step, ssm_b_re, ssm_b_im, ssm_c_re, ssm_c_im, ssm_d,
           ssm_w_glu, fox_w_in, fox_b_f, fox_w_out, band_w_in, band_rel_bias, band_w_out, final_norm):
    bp, lp, d = x_prompt.shape
    bs, ls, _ = x_sample.shape
    depth = ffn1_norm.shape[0]
    nh = fox_b_f.shape[1]
    hd = d // nh
    scale = hd ** -0.5
    past = cache_fox_k.shape[2]
    window = cache_band_k.shape[2]
    heads = lambda t, nb, length: t.reshape(nb, length, nh, hd)

    xp = x_prompt.reshape(bp * lp, d)
    xs = x_sample.reshape(bs * ls, d)
    outs = {name: [] for name in (
        "p_ssm_re", "p_ssm_im", "p_fox_k", "p_fox_v", "p_fox_logf", "p_band_k", "p_band_v",
        "s_ssm_re", "s_ssm_im", "s_fox_k", "s_fox_v", "s_fox_logf", "s_band_k", "s_band_v")}

    w1_in, w1_out = ffn1_w_in.astype(BF16), ffn1_w_out.astype(BF16)
    w2_in, w2_out = ffn2_w_in.astype(BF16), ffn2_w_out.astype(BF16)
    w_glu = ssm_w_glu.astype(BF16)
    fox_in, fox_out = fox_w_in.astype(BF16), fox_w_out.astype(BF16)
    band_in, band_out = band_w_in.astype(BF16), band_w_out.astype(BF16)

    for i in range(depth):
        kind, j = i % 3, i // 3
        xp = ffn_half_step(xp, ffn1_norm[i], w1_in, w1_out, i)
        xs = ffn_half_step(xs, ffn1_norm[i], w1_in, w1_out, i)
        if kind == 0:
            mats = ssm_matrices(ssm_a_re[j], ssm_a_im[j], ssm_log_step[j], ssm_b_re[j], ssm_b_im[j],
                                ssm_c_re[j], ssm_c_im[j])
            g, p = ssm_a_re.shape[1:]
            nblk = d // LANES
            zeros = jnp.zeros((bp, g, p), F32)
            streams = ((xp, bp, lp, zeros, zeros, "p"), (xs, bs, ls, state_ssm_re[j], state_ssm_im[j], "s"))
            new_x = []
            for x, nb, length, s_re, s_im, tag in streams:
                h_tm = rmsnorm_time_major(x, mix_norm[i], nb, length)
                a_tm, sf = ssm_core(h_tm, mats, ssm_pack_state(s_re, s_im, nblk), ssm_d[j], nb, length)
                f_re, f_im = ssm_unpack_state(sf, g, p)
                outs[tag + "_ssm_re"].append(f_re); outs[tag + "_ssm_im"].append(f_im)
                new_x.append(ssm_glu_out(x, a_tm, w_glu, j, nb, length))
            xp, xs = new_x
        elif kind == 1:
            wf_t = fox_w_in[j][:, 3 * d:].T.astype(BF16)
            qp, kp, vp, kbp, vbp, lfp = qkv_project(xp, mix_norm[i], fox_in, j, scale, wf_t, fox_b_f[j])
            qs, ks, vs, kbs, vbs, lfs = qkv_project(xs, mix_norm[i], fox_in, j, scale, wf_t, fox_b_f[j])
            lfp = lfp.reshape(nh, bp, lp).transpose(1, 0, 2)
            lfs = lfs.reshape(nh, bs, ls).transpose(1, 0, 2)
            cp = cumsum_time(lfp).reshape(bp * nh, 1, lp)
            lf_all = jnp.concatenate([cache_fox_logf[j].astype(F32).transpose(0, 2, 1), lfs,
                                      jnp.zeros((bs, nh, LANES - ls), F32)], axis=2)
            cs = cumsum_time(lf_all)
            op = fox_prompt_attention(qp, kbp, vbp, cp, bp, nh)
            os_ = fox_sample_attention(qs, cache_fox_k, cache_fox_v, j, kbs, vbs, cs, bs, nh)
            xp = out_project(xp, op, fox_out, j)
            xs = out_project(xs, os_, fox_out, j)
            outs["p_fox_k"].append(heads(kp, bp, lp)); outs["p_fox_v"].append(heads(vp, bp, lp))
            outs["p_fox_logf"].append(lfp.transpose(0, 2, 1))
            outs["s_fox_k"].append(heads(ks, bs, ls)); outs["s_fox_v"].append(heads(vs, bs, ls))
            outs["s_fox_logf"].append(lfs.transpose(0, 2, 1))
        else:
            qp, kp, vp, kbp, vbp = qkv_project(xp, mix_norm[i], band_in, j, scale)
            qs, ks, vs, kbs, vbs = qkv_project(xs, mix_norm[i], band_in, j, scale)
            bw = BAND_PREV * CHUNK
            bias = band_bias_tile(band_rel_bias[j], BAND_TQ, bw)
            op = band_prompt_attention(qp, kbp, vbp, bias, bp, nh)
            os_ = band_sample_attention(qs, cache_band_k, cache_band_v, j, kbs, vbs,
                                        bias[:, :ls, bw - window:bw + ls], bs, nh)
            xp = out_project(xp, op, band_out, j)
            xs = out_project(xs, os_, band_out, j)
            keep = min(bw, lp)
            outs["p_band_k"].append(heads(kp, bp, lp)[:, lp - keep:])
            outs["p_band_v"].append(heads(vp, bp, lp)[:, lp - keep:])
            outs["s_band_k"].append(heads(ks, bs, ls)); outs["s_band_v"].append(heads(vs, bs, ls))
        xp = ffn_half_step(xp, ffn2_norm[i], w2_in, w2_out, i)
        xs = ffn_half_step(xs, ffn2_norm[i], w2_in, w2_out, i)

    y_prompt = rmsnorm(xp, final_norm).reshape(bp, lp, d)
    y_sample = rmsnorm(xs, final_norm).reshape(bs, ls, d)
    st = {k: jnp.stack(v) for k, v in outs.items()}
    return (y_prompt, y_sample, st["p_ssm_re"], st["p_ssm_im"], st["p_fox_k"], st["p_fox_v"],
            st["p_fox_logf"], st["p_band_k"], st["p_band_v"], st["s_ssm_re"], st["s_ssm_im"],
            st["s_fox_k"], st["s_fox_v"], st["s_fox_logf"], st["s_band_k"], st["s_band_v"])
```

```python
import functools

import jax
import jax.numpy as jnp
import numpy as np
from jax import lax
from jax.experimental import pallas as pl
from jax.experimental.pallas import tpu as pltpu

F32 = jnp.float32
BF16 = jnp.bfloat16

EPS = 1e-6
NEG_INF = -1e30
CHUNK = 64
BAND_PREV = 8
REL_CLIP = 256
SSM_GROUP = 16
SSM_T = 16
LANES = 128
VMEM_LIMIT = 56 * 1024 * 1024

TM = 512
TM_FFN = 1024
TF = 512
TN = 512
BAND_TQ = 256
FOX_TQ = 256
FOX_TK = 512
SSM_ROWS = 256


def _params(*sem):
    return pltpu.CompilerParams(dimension_semantics=sem, vmem_limit_bytes=VMEM_LIMIT)


def _rms(x, g):
    return x * lax.rsqrt(jnp.mean(x * x, axis=-1, keepdims=True) + EPS) * g


def _dot(a, b):
    return jnp.dot(a, b, preferred_element_type=F32)


def _dot_nt(a, b):
    return lax.dot_general(a, b, (((1,), (1,)), ((), ())), preferred_element_type=F32)


def _ffn_body(x_ref, g_ref, wg_ref, wu_ref, wo_ref, o_ref, h_ref):
    f = pl.program_id(1)

    @pl.when(f == 0)
    def _():
        h_ref[...] = _rms(x_ref[...], g_ref[...]).astype(BF16)
        o_ref[...] = jnp.zeros_like(o_ref)

    h = h_ref[...]
    gate = _dot(h, wg_ref[...])
    up = _dot(h, wu_ref[...])
    act = (gate * jax.nn.sigmoid(gate) * up).astype(BF16)
    o_ref[...] += _dot(act, wo_ref[...])

    @pl.when(f == pl.num_programs(1) - 1)
    def _():
        o_ref[...] = x_ref[...] + 0.5 * o_ref[...]


def ffn_half_step(x, g, w_in, w_out, layer):
    m, d = x.shape
    f = w_out.shape[1]
    tm = min(TM_FFN, m)
    tf = min(TF, f)
    nf = f // tf
    return pl.pallas_call(
        _ffn_body,
        grid=(m // tm, nf),
        in_specs=[
            pl.BlockSpec((tm, d), lambda i, j: (i, 0)),
            pl.BlockSpec((1, d), lambda i, j: (0, 0)),
            pl.BlockSpec((None, d, tf), lambda i, j: (layer, 0, j)),
            pl.BlockSpec((None, d, tf), lambda i, j: (layer, 0, j + nf)),
            pl.BlockSpec((None, tf, d), lambda i, j: (layer, j, 0)),
        ],
        out_specs=pl.BlockSpec((tm, d), lambda i, j: (i, 0)),
        out_shape=jax.ShapeDtypeStruct((m, d), F32),
        scratch_shapes=[pltpu.VMEM((tm, d), BF16)],
        compiler_params=_params("parallel", "arbitrary"),
        name="ffn_half_step",
    )(x, g.reshape(1, d), w_in, w_in, w_out)


def _rmsnorm_body(x_ref, g_ref, o_ref):
    o_ref[...] = _rms(x_ref[...], g_ref[...]).astype(o_ref.dtype)


def rmsnorm(x, g):
    m, d = x.shape
    tm = min(TM, m)
    return pl.pallas_call(
        _rmsnorm_body,
        grid=(m // tm,),
        in_specs=[pl.BlockSpec((tm, d), lambda i: (i, 0)),
                  pl.BlockSpec((1, d), lambda i: (0, 0))],
        out_specs=pl.BlockSpec((tm, d), lambda i: (i, 0)),
        out_shape=jax.ShapeDtypeStruct((m, d), F32),
        compiler_params=_params("parallel"),
        name="rmsnorm",
    )(x, g.reshape(1, d))


def rmsnorm_time_major(x, g, nb, length):
    d = x.shape[1]
    tm = min(TM, length)
    nlt = length // tm
    return pl.pallas_call(
        _rmsnorm_body,
        grid=(nb, nlt),
        in_specs=[pl.BlockSpec((tm, d), lambda b, l: (b * nlt + l, 0)),
                  pl.BlockSpec((1, d), lambda b, l: (0, 0))],
        out_specs=pl.BlockSpec((tm, d), lambda b, l: (l, b)),
        out_shape=jax.ShapeDtypeStruct((length, nb * d), F32),
        compiler_params=_params("parallel", "parallel"),
        name="rmsnorm_time_major",
    )(x, g.reshape(1, d))


def _qkv_body(scale, has_forget, *refs):
    if has_forget:
        (x_ref, g_ref, wq_ref, wk_ref, wv_ref, wf_ref, bf_ref,
         q_ref, k32_ref, v32_ref, kb_ref, vb_ref, lf_ref, h_ref) = refs
    else:
        (x_ref, g_ref, wq_ref, wk_ref, wv_ref,
         q_ref, k32_ref, v32_ref, kb_ref, vb_ref, h_ref) = refs

    @pl.when(pl.program_id(1) == 0)
    def _():
        h = _rms(x_ref[...], g_ref[...]).astype(BF16)
        h_ref[...] = h
        if has_forget:
            z = _dot_nt(wf_ref[...], h) + bf_ref[...]
            lf_ref[...] = jnp.minimum(z, 0.0) - jnp.log(1.0 + jnp.exp(-jnp.abs(z)))

    h = h_ref[...]
    q_ref[...] = (_dot(h, wq_ref[...]) * scale).astype(BF16)
    k = _dot(h, wk_ref[...])
    k32_ref[...] = k
    kb_ref[...] = k.astype(BF16)
    v = _dot(h, wv_ref[...])
    v32_ref[...] = v
    vb_ref[...] = v.astype(BF16)


def qkv_project(x, g, w, layer, scale, wf_t=None, b_f=None):
    m, d = x.shape
    tm = min(TM, m)
    tn = min(TN, d)
    nq = d // tn
    has_forget = wf_t is not None
    in_specs = [
        pl.BlockSpec((tm, d), lambda i, n: (i, 0)),
        pl.BlockSpec((1, d), lambda i, n: (0, 0)),
        pl.BlockSpec((None, d, tn), lambda i, n: (layer, 0, n)),
        pl.BlockSpec((None, d, tn), lambda i, n: (layer, 0, n + nq)),
        pl.BlockSpec((None, d, tn), lambda i, n: (layer, 0, n + 2 * nq)),
    ]
    args = [x, g.reshape(1, d), w, w, w]
    tile = pl.BlockSpec((tm, tn), lambda i, n: (i, n))
    out_specs = [tile] * 5
    out_shape = [jax.ShapeDtypeStruct((m, d), BF16),
                 jax.ShapeDtypeStruct((m, d), F32), jax.ShapeDtypeStruct((m, d), F32),
                 jax.ShapeDtypeStruct((m, d), BF16), jax.ShapeDtypeStruct((m, d), BF16)]
    if has_forget:
        nh = wf_t.shape[0]
        in_specs += [pl.BlockSpec((nh, d), lambda i, n: (0, 0)),
                     pl.BlockSpec((nh, 1), lambda i, n: (0, 0))]
        args += [wf_t, b_f.reshape(nh, 1)]
        out_specs = out_specs + [pl.BlockSpec((nh, tm), lambda i, n: (0, i))]
        out_shape = out_shape + [jax.ShapeDtypeStruct((nh, m), F32)]
    return pl.pallas_call(
        functools.partial(_qkv_body, scale, has_forget),
        grid=(m // tm, nq),
        in_specs=in_specs,
        out_specs=out_specs,
        out_shape=out_shape,
        scratch_shapes=[pltpu.VMEM((tm, d), BF16)],
        compiler_params=_params("parallel", "arbitrary"),
        name="qkv_project",
    )(*args)


def _out_proj_body(x_ref, o_ref, w_ref, y_ref):
    y_ref[...] = x_ref[...] + _dot(o_ref[...], w_ref[...])


def out_project(x, o, w, layer):
    m, d = x.shape
    tm = min(TM, m)
    return pl.pallas_call(
        _out_proj_body,
        grid=(m // tm,),
        in_specs=[pl.BlockSpec((tm, d), lambda i: (i, 0)),
                  pl.BlockSpec((tm, d), lambda i: (i, 0)),
                  pl.BlockSpec((None, d, d), lambda i: (layer, 0, 0))],
        out_specs=pl.BlockSpec((tm, d), lambda i: (i, 0)),
        out_shape=jax.ShapeDtypeStruct((m, d), F32),
        compiler_params=_params("parallel"),
        name="out_project",
    )(x, o, w)


def _glu_body(a_ref, xt_ref, wv_ref, wg_ref, o_ref):
    a = a_ref[...].astype(BF16)
    val = _dot(a, wv_ref[...])
    gate = _dot(a, wg_ref[...])
    o_ref[...] = xt_ref[...] + val * jax.nn.sigmoid(gate)


def ssm_glu_out(x, a_tm, w_glu, layer, nb, length):
    d = x.shape[1]
    tm = min(TM, length)
    nlt = length // tm
    tn = min(TN, d)
    nq = d // tn
    tile = pl.BlockSpec((tm, tn), lambda n, b, l: (b * nlt + l, n))
    return pl.pallas_call(
        _glu_body,
        grid=(nq, nb, nlt),
        in_specs=[
            pl.BlockSpec((tm, d), lambda n, b, l: (l, b)),
            tile,
            pl.BlockSpec((None, d, tn), lambda n, b, l: (layer, 0, n)),
            pl.BlockSpec((None, d, tn), lambda n, b, l: (layer, 0, n + nq)),
        ],
        out_specs=tile,
        out_shape=jax.ShapeDtypeStruct((nb * length, d), F32),
        compiler_params=_params("parallel", "parallel", "parallel"),
        name="ssm_glu_out",
    )(a_tm, x, w_glu, w_glu)


def _ssm_lag_body(ca_ref, bb_ref, k_ref):
    for i in range(ca_ref.shape[0]):
        k_ref[i] = jnp.dot(ca_ref[i], bb_ref[i], preferred_element_type=F32,
                           precision=lax.Precision.HIGHEST)


def ssm_lag_kernels(ca, bb, gpb):
    g, r, p2 = ca.shape
    c = bb.shape[2]
    return pl.pallas_call(
        _ssm_lag_body,
        grid=(g // gpb,),
        in_specs=[pl.BlockSpec((gpb, r, p2), lambda i: (i, 0, 0)),
                  pl.BlockSpec((gpb, p2, c), lambda i: (i, 0, 0))],
        out_specs=pl.BlockSpec((gpb, r, c), lambda i: (i, 0, 0)),
        out_shape=jax.ShapeDtypeStruct((g, r, c), F32),
        compiler_params=_params("parallel"),
        name="ssm_lag_kernels",
    )(ca, bb)


def _ssm_core_body(nb, x_ref, bdv_ref, ws_ref, wc_ref, at_ref, s0_ref, dsk_ref, y_ref, sf_ref,
                   ds_ref, dsw_ref, sst_ref, st_ref):
    nchp, t = x_ref.shape[0], x_ref.shape[1]
    m = nchp * nb
    ncol = at_ref.shape[2]
    half = LANES // 2

    def swap_halves(v):
        lane = lax.broadcasted_iota(jnp.int32, v.shape, 1)
        return jnp.where(lane % LANES < half, pltpu.roll(v, ncol - half, 1), pltpu.roll(v, half, 1))

    @pl.when(pl.program_id(1) == 0)
    def _():
        st_ref[...] = s0_ref[0]

    xcat = jnp.concatenate(
        [x_ref[:, tau].reshape(m, LANES).astype(BF16) for tau in range(t)], axis=1)
    ds = _dot(xcat, ws_ref[0])
    ds_ref[...] = ds
    dsw_ref[...] = swap_halves(ds)
    a1 = jnp.broadcast_to(at_ref[0, 0:1, :], (nb, ncol))
    a2 = jnp.broadcast_to(at_ref[0, 1:2, :], (nb, ncol))

    def step(k, carry):
        s, sw = carry
        rows = pl.ds(pl.multiple_of(k * nb, nb), nb)
        sst_ref[rows, :] = s
        ns = a1 * s + a2 * sw + ds_ref[rows, :]
        nsw = a1 * sw - a2 * s + dsw_ref[rows, :]
        return ns, nsw

    s0 = st_ref[...]
    s, _ = lax.fori_loop(0, nchp, step, (s0, swap_halves(s0)))
    st_ref[...] = s
    sf_ref[0] = s
    y_in = _dot(sst_ref[...].astype(BF16), wc_ref[0])
    dsk = dsk_ref[...]
    for tt in range(t):
        yt = y_in[:, tt * LANES:(tt + 1) * LANES] + _dot(
            xcat[:, :(tt + 1) * LANES], bdv_ref[0, (t - 1 - tt) * LANES:, :])
        yt = jax.nn.gelu(yt + dsk * x_ref[:, tt].reshape(m, LANES))
        y_ref[:, tt] = yt.reshape(nchp, nb, LANES)


def ssm_core(h_tm, mats, s0, d_skip, nb, length):
    bdv, ws, wc, at = mats
    d = h_tm.shape[1] // nb
    t = SSM_T
    nch = length // t
    nblk = d // LANES
    nchp = max(min(SSM_ROWS // nb, nch), 1)
    m = nchp * nb
    ncol = ws.shape[2]
    x4 = h_tm.reshape(nch, t, nb, d)
    wblk = lambda *s: pl.BlockSpec((1,) + s, lambda c, k: (c, 0, 0))
    seq = pl.BlockSpec((nchp, t, nb, LANES), lambda c, k: (k, 0, 0, c))
    y, sf = pl.pallas_call(
        functools.partial(_ssm_core_body, nb),
        grid=(nblk, nch // nchp),
        in_specs=[seq, wblk(t * LANES, LANES), wblk(t * LANES, ncol), wblk(ncol, t * LANES),
                  wblk(2, ncol), wblk(nb, ncol), pl.BlockSpec((1, LANES), lambda c, k: (0, c))],
        out_specs=[seq, wblk(nb, ncol)],
        out_shape=[jax.ShapeDtypeStruct((nch, t, nb, d), F32),
                   jax.ShapeDtypeStruct((nblk, nb, ncol), F32)],
        scratch_shapes=[pltpu.VMEM((m, ncol), F32), pltpu.VMEM((m, ncol), F32), pltpu.VMEM((m, ncol), F32),
                        pltpu.VMEM((nb, ncol), F32)],
        compiler_params=_params("parallel", "arbitrary"),
        name="ssm_core",
    )(x4, bdv, ws, wc, at, s0, d_skip.reshape(1, d))
    return y.reshape(length, nb * d), sf


def ssm_matrices(a_re, a_im, log_step, b_re, b_im, c_re, c_im):
    t = SSM_T
    g, p = a_re.shape
    c = b_re.shape[2]
    gpb = LANES // c
    nblk = g // gpb
    dt = jnp.exp(log_step)[:, None]
    lam_re, lam_im = a_re * dt, a_im * dt
    j = jnp.arange(t + 1, dtype=F32)[:, None, None]
    mag = jnp.exp(lam_re * j)
    pw_re, pw_im = mag * jnp.cos(lam_im * j), mag * jnp.sin(lam_im * j)
    ab_re, ab_im = pw_re[1], pw_im[1]
    den = a_re * a_re + a_im * a_im
    z_re = ((ab_re - 1.0) * a_re + ab_im * a_im) / den
    z_im = (ab_im * a_re - (ab_re - 1.0) * a_im) / den
    bb_re = z_re[..., None] * b_re - z_im[..., None] * b_im
    bb_im = z_re[..., None] * b_im + z_im[..., None] * b_re
    jr = (t - 1) - jnp.arange(t, dtype=F32)[:, None, None]
    mag_r = jnp.exp(lam_re * jr)
    rp_re, rp_im = mag_r * jnp.cos(lam_im * jr), mag_r * jnp.sin(lam_im * jr)
    ws_re = rp_re[:, :, :, None] * bb_re[None] - rp_im[:, :, :, None] * bb_im[None]
    ws_im = rp_re[:, :, :, None] * bb_im[None] + rp_im[:, :, :, None] * bb_re[None]
    wsc = jnp.concatenate([ws_re, ws_im], axis=2).transpose(1, 0, 3, 2).reshape(nblk, gpb, t * c, 2 * p)

    def out_weights(q_re, q_im):
        cr, ci = c_re[None], c_im[None]
        qr, qi = q_re[:, :, None, :], q_im[:, :, None, :]
        return cr * qr - ci * qi, -cr * qi - ci * qr

    wc_re, wc_im = out_weights(pw_re[1:], pw_im[1:])
    wcc = jnp.concatenate([wc_re, wc_im], axis=3).transpose(1, 3, 0, 2).reshape(nblk, gpb, 2 * p, t * c)

    ca_re, ca_im = out_weights(rp_re, rp_im)
    ca = jnp.concatenate([ca_re, ca_im], axis=-1).transpose(1, 0, 2, 3).reshape(g, t * c, 2 * p)
    bb = jnp.concatenate([bb_re, bb_im], axis=1)
    kmat = ssm_lag_kernels(ca, bb, gpb).reshape(g, t, c, c)
    kc = kmat.transpose(0, 1, 3, 2).reshape(nblk, gpb, t * c, c)

    place = np.zeros((gpb, t, gpb, c, t, c), np.float32)
    for h in range(gpb):
        place[h, :, h] = np.eye(t * c, dtype=np.float32).reshape(t, c, t, c)
    place = jnp.asarray(place.reshape(gpb, t * LANES, t * c), BF16)
    expand = functools.partial(jnp.einsum, preferred_element_type=F32)
    ws = expand("hrk,bhkn->brhn", place, wsc.astype(BF16)).reshape(nblk, t * LANES, gpb * 2 * p)
    wc = expand("bhnk,hrk->bhnr", wcc.astype(BF16), place).reshape(nblk, gpb * 2 * p, t * LANES)
    bdv = expand("hrk,bhkc->brhc", place, kc.astype(BF16)).reshape(nblk, t * LANES, LANES)

    at_re = pw_re[t].reshape(nblk, gpb, p)
    at_im = pw_im[t].reshape(nblk, gpb, p)
    a1 = jnp.concatenate([at_re, at_re], axis=2).reshape(nblk, gpb * 2 * p)
    a2 = jnp.concatenate([-at_im, at_im], axis=2).reshape(nblk, gpb * 2 * p)
    return bdv.astype(BF16), ws.astype(BF16), wc.astype(BF16), jnp.stack([a1, a2], axis=1)


def ssm_pack_state(s_re, s_im, nblk):
    nb, g, p = s_re.shape
    s = jnp.stack([s_re.reshape(nb, nblk, g // nblk, p), s_im.reshape(nb, nblk, g // nblk, p)], axis=3)
    return s.transpose(1, 0, 2, 3, 4).reshape(nblk, nb, -1)


def ssm_unpack_state(sf, g, p):
    nblk, nb, _ = sf.shape
    s = sf.reshape(nblk, nb, g // nblk, 2, p).transpose(1, 0, 2, 3, 4).reshape(nb, g, 2, p)
    return s[:, :, 0], s[:, :, 1]


def _split3(x):
    hi = x.astype(BF16)
    r = x - hi.astype(F32)
    mid = r.astype(BF16)
    lo = (r - mid.astype(F32)).astype(BF16)
    return hi, mid, lo


def _cumsum_body(x_ref, o_ref):
    nblk = x_ref.shape[2] // LANES
    rows = x_ref.shape[1]
    row = lax.broadcasted_iota(jnp.int32, (LANES, LANES), 0)
    col = lax.broadcasted_iota(jnp.int32, (LANES, LANES), 1)
    tri = jnp.where(row <= col, 1.0, 0.0).astype(BF16)
    carry = jnp.zeros((rows, 1), F32)
    for j in range(nblk):
        sl = slice(j * LANES, (j + 1) * LANES)
        hi, mid, lo = _split3(x_ref[0, :, sl])
        w = _dot(hi, tri) + _dot(mid, tri) + _dot(lo, tri)
        o_ref[0, :, sl] = w + carry
        carry = carry + jnp.sum(x_ref[0, :, sl], axis=1, keepdims=True)


def cumsum_time(x):
    n, h, length = x.shape
    return pl.pallas_call(
        _cumsum_body,
        grid=(n,),
        in_specs=[pl.BlockSpec((1, h, length), lambda i: (i, 0, 0))],
        out_specs=pl.BlockSpec((1, h, length), lambda i: (i, 0, 0)),
        out_shape=jax.ShapeDtypeStruct((n, h, length), F32),
        compiler_params=_params("parallel"),
        name="cumsum_time",
    )(x)


def _softmax_pv(parts):
    m = functools.reduce(jnp.maximum, [jnp.max(s, axis=-1, keepdims=True) for s, _ in parts])
    acc, den = None, None
    for s, v in parts:
        e = jnp.exp(s - m)
        l = jnp.sum(e, axis=-1, keepdims=True)
        o = _dot(e.astype(BF16), v)
        acc = o if acc is None else acc + o
        den = l if den is None else den + l
    return acc / den


def _causal(n):
    rpos = lax.broadcasted_iota(jnp.int32, (n, n), 0)
    cpos = lax.broadcasted_iota(jnp.int32, (n, n), 1)
    return cpos <= rpos


def _fox_prompt_body(tq, q_ref, k_ref, v_ref, c_ref, o_ref):
    length = q_ref.shape[0]
    keep = _causal(tq)
    for i in range(length // tq):
        lo, hi = i * tq, (i + 1) * tq
        q = q_ref[lo:hi, :]
        parts = []
        if i > 0:
            parts.append((_dot_nt(q, k_ref[0:lo, :]) - c_ref[0, :, 0:lo], v_ref[0:lo, :]))
        s = _dot_nt(q, k_ref[lo:hi, :]) - c_ref[0, :, lo:hi]
        parts.append((jnp.where(keep, s, NEG_INF), v_ref[lo:hi, :]))
        o_ref[lo:hi, :] = _softmax_pv(parts).astype(o_ref.dtype)


def fox_prompt_attention(q, kb, vb, c, nb, nh):
    m, d = q.shape
    length = m // nb
    hd = d // nh
    tq = min(FOX_TQ, length)
    seq = pl.BlockSpec((length, hd), lambda b, h: (b, h))
    return pl.pallas_call(
        functools.partial(_fox_prompt_body, tq),
        grid=(nb, nh),
        in_specs=[seq, seq, seq, pl.BlockSpec((1, 1, length), lambda b, h: (b * nh + h, 0, 0))],
        out_specs=seq,
        out_shape=jax.ShapeDtypeStruct((m, d), BF16),
        compiler_params=_params("parallel", "parallel"),
        name="fox_prompt_attention",
    )(q, kb, vb, c)


def _fox_sample_body(nh, q_ref, kc_ref, vc_ref, kn_ref, vn_ref, cc_ref, cn_ref, o_ref,
                     m_ref, l_ref, acc_ref):
    j = pl.program_id(1)
    lq, d = q_ref.shape
    hd = d // nh

    @pl.when(j == 0)
    def _():
        m_ref[...] = jnp.full_like(m_ref, NEG_INF)
        l_ref[...] = jnp.zeros_like(l_ref)
        acc_ref[...] = jnp.zeros_like(acc_ref)

    heads = lambda ref: [ref[:, h * hd:(h + 1) * hd] for h in range(nh)]
    qs = heads(q_ref)

    def update(s, vs):
        m_old = m_ref[...]
        m_new = jnp.maximum(m_old, jnp.max(s, axis=-1, keepdims=True))
        alpha = jnp.exp(m_old - m_new)
        e = jnp.exp(s - m_new)
        l_ref[...] = alpha * l_ref[...] + jnp.sum(e, axis=-1, keepdims=True)
        e = e.astype(BF16)
        pv = jnp.concatenate([_dot(e[h * lq:(h + 1) * lq], vs[h]) for h in range(nh)], axis=0)
        acc_ref[...] = alpha * acc_ref[...] + pv
        m_ref[...] = m_new

    k_all = pltpu.einshape("lhd->hld", kc_ref[...])
    v_all = pltpu.einshape("lhd->hld", vc_ref[...])
    s = jnp.concatenate([_dot_nt(qs[h], k_all[h].astype(BF16)) - cc_ref[0, h:h + 1, :]
                         for h in range(nh)], axis=0)
    update(s, [v_all[h].astype(BF16) for h in range(nh)])

    @pl.when(j == pl.num_programs(1) - 1)
    def _():
        keep = _causal(lq)
        kn = heads(kn_ref)
        s = jnp.concatenate(
            [jnp.where(keep, _dot_nt(qs[h], kn[h]) - cn_ref[0, h:h + 1, 0:lq], NEG_INF)
             for h in range(nh)], axis=0)
        update(s, heads(vn_ref))
        o = acc_ref[...] / l_ref[...]
        for h in range(nh):
            o_ref[:, h * hd:(h + 1) * hd] = o[h * lq:(h + 1) * lq].astype(o_ref.dtype)


def fox_sample_attention(q, k_cache, v_cache, layer, kb, vb, c, nb, nh):
    m, d = q.shape
    lq = m // nb
    past, hd = k_cache.shape[2], k_cache.shape[4]
    tk = min(FOX_TK, past)
    nkv = past // tk
    new = pl.BlockSpec((lq, d), lambda b, j: (b, 0))
    cache = pl.BlockSpec((None, None, tk, nh, hd), lambda b, j: (layer, b, j, 0, 0))
    return pl.pallas_call(
        functools.partial(_fox_sample_body, nh),
        grid=(nb, nkv),
        in_specs=[new, cache, cache, new, new,
                  pl.BlockSpec((1, nh, tk), lambda b, j: (b, 0, j)),
                  pl.BlockSpec((1, nh, LANES), lambda b, j: (b, 0, past // LANES))],
        out_specs=new,
        out_shape=jax.ShapeDtypeStruct((m, d), BF16),
        scratch_shapes=[pltpu.VMEM((nh * lq, 1), F32), pltpu.VMEM((nh * lq, 1), F32),
                        pltpu.VMEM((nh * lq, hd), F32)],
        compiler_params=_params("parallel", "arbitrary"),
        name="fox_sample_attention",
    )(q, k_cache, v_cache, kb, vb, c, c)


def _band_prompt_body(tq, q_ref, k_ref, v_ref, bias_ref, o_ref):
    length = q_ref.shape[0]
    span = bias_ref.shape[2]
    for i in range(length // tq):
        k0 = max((i + 1) * tq - span, 0)
        k1 = (i + 1) * tq
        q = q_ref[i * tq:(i + 1) * tq, :]
        s = _dot_nt(q, k_ref[k0:k1, :]) + bias_ref[0, :, span - (k1 - k0):span]
        o_ref[i * tq:(i + 1) * tq, :] = _softmax_pv([(s, v_ref[k0:k1, :])]).astype(o_ref.dtype)


def band_prompt_attention(q, kb, vb, bias, nb, nh):
    m, d = q.shape
    length = m // nb
    hd = d // nh
    tq = bias.shape[1]
    seq = pl.BlockSpec((length, hd), lambda b, h: (b, h))
    return pl.pallas_call(
        functools.partial(_band_prompt_body, tq),
        grid=(nb, nh),
        in_specs=[seq, seq, seq, pl.BlockSpec((1,) + bias.shape[1:], lambda b, h: (h, 0, 0))],
        out_specs=seq,
        out_shape=jax.ShapeDtypeStruct((m, d), BF16),
        compiler_params=_params("parallel", "parallel"),
        name="band_prompt_attention",
    )(q, kb, vb, bias)


def _band_sample_body(nh, q_ref, kc_ref, vc_ref, kn_ref, vn_ref, bias_ref, o_ref):
    lq, d = q_ref.shape
    hd = d // nh
    lc = kc_ref.shape[0]
    k_all = pltpu.einshape("lhd->hld", kc_ref[...])
    v_all = pltpu.einshape("lhd->hld", vc_ref[...])
    heads = lambda ref: [ref[:, h * hd:(h + 1) * hd] for h in range(nh)]
    qs, kn, vn = heads(q_ref), heads(kn_ref), heads(vn_ref)
    s_c = jnp.concatenate([_dot_nt(qs[h], k_all[h].astype(BF16)) + bias_ref[h, :, 0:lc]
                           for h in range(nh)], axis=0)
    s_n = jnp.concatenate([_dot_nt(qs[h], kn[h]) + bias_ref[h, :, lc:lc + lq] for h in range(nh)], axis=0)
    m = jnp.maximum(jnp.max(s_c, axis=-1, keepdims=True), jnp.max(s_n, axis=-1, keepdims=True))
    e_c, e_n = jnp.exp(s_c - m), jnp.exp(s_n - m)
    inv = 1.0 / (jnp.sum(e_c, axis=-1, keepdims=True) + jnp.sum(e_n, axis=-1, keepdims=True))
    e_c, e_n = e_c.astype(BF16), e_n.astype(BF16)
    for h in range(nh):
        rows = slice(h * lq, (h + 1) * lq)
        o = _dot(e_c[rows], v_all[h].astype(BF16)) + _dot(e_n[rows], vn[h])
        o_ref[:, h * hd:(h + 1) * hd] = (o * inv[rows]).astype(o_ref.dtype)


def band_sample_attention(q, k_cache, v_cache, layer, kb, vb, bias, nb, nh):
    m, d = q.shape
    lq = m // nb
    lc, hd = k_cache.shape[2], k_cache.shape[4]
    new = pl.BlockSpec((lq, d), lambda b: (b, 0))
    cache = pl.BlockSpec((None, None, lc, nh, hd), lambda b: (layer, b, 0, 0, 0))
    return pl.pallas_call(
        functools.partial(_band_sample_body, nh),
        grid=(nb,),
        in_specs=[new, cache, cache, new, new, pl.BlockSpec(bias.shape, lambda b: (0, 0, 0))],
        out_specs=new,
        out_shape=jax.ShapeDtypeStruct((m, d), BF16),
        compiler_params=_params("parallel"),
        name="band_sample_attention",
    )(q, k_cache, v_cache, kb, vb, bias)


def band_bias_tile(rel_bias, tq, window):
    nh = rel_bias.shape[0]
    span = window + tq
    n = span + tq
    rel = (span - 1) - np.arange(n)
    diag = rel_bias.astype(F32)[:, np.clip(rel, -REL_CLIP, REL_CLIP) + REL_CLIP]
    skew = jnp.broadcast_to(diag[:, None, :], (nh, tq, n)).reshape(nh, tq * n)
    skew = skew[:, :tq * (n - 1)].reshape(nh, tq, n - 1)[:, :, tq - 1:tq - 1 + span]
    qpos = np.arange(tq)[:, None]
    kpos = np.arange(span)[None, :] - window
    qc, kc = qpos // CHUNK, kpos // CHUNK
    visible = (kc <= qc) & (kc >= qc - BAND_PREV)
    return jnp.where(visible[None], skew, NEG_INF)


def kernel(x_prompt, x_sample, state_ssm_re, state_ssm_im, cache_fox_k, cache_fox_v, cache_fox_logf,
           cache_band_k, cache_band_v, ffn1_norm, ffn1_w_in, ffn1_w_out, mix_norm, ffn2_norm, ffn2_w_in,
           ffn2_w_out, ssm_a_re, ssm_a_im, ssm_log_step, ssm_b_re, ssm_b_im, ssm_c_re, ssm_c_im, ssm_d,
           ssm_w_glu, fox_w_in, fox_b_f, fox_w_out, band_w_in, band_rel_bias, band_w_out, final_norm):
    bp, lp, d = x_prompt.shape
    bs, ls, _ = x_sample.shape
    depth = ffn1_norm.shape[0]
    nh = fox_b_f.shape[1]
    hd = d // nh
    scale = hd ** -0.5
    past = cache_fox_k.shape[2]
    window = cache_band_k.shape[2]
    heads = lambda t, nb, length: t.reshape(nb, length, nh, hd)

    xp = x_prompt.reshape(bp * lp, d)
    xs = x_sample.reshape(bs * ls, d)
    outs = {name: [] for name in (
        "p_ssm_re", "p_ssm_im", "p_fox_k", "p_fox_v", "p_fox_logf", "p_band_k", "p_band_v",
        "s_ssm_re", "s_ssm_im", "s_fox_k", "s_fox_v", "s_fox_logf", "s_band_k", "s_band_v")}

    w1_in, w1_out = ffn1_w_in.astype(BF16), ffn1_w_out.astype(BF16)
    w2_in, w2_out = ffn2_w_in.astype(BF16), ffn2_w_out.astype(BF16)
    w_glu = ssm_w_glu.astype(BF16)
    fox_in, fox_out = fox_w_in.astype(BF16), fox_w_out.astype(BF16)
    band_in, band_out = band_w_in.astype(BF16), band_w_out.astype(BF16)

    for i in range(depth):
        kind, j = i % 3, i // 3
        xp = ffn_half_step(xp, ffn1_norm[i], w1_in, w1_out, i)
        xs = ffn_half_step(xs, ffn1_norm[i], w1_in, w1_out, i)
        if kind == 0:
            mats = ssm_matrices(ssm_a_re[j], ssm_a_im[j], ssm_log_step[j], ssm_b_re[j], ssm_b_im[j],
                                ssm_c_re[j], ssm_c_im[j])
            g, p = ssm_a_re.shape[1:]
            nblk = d // LANES
            zeros = jnp.zeros((bp, g, p), F32)
            streams = ((xp, bp, lp, zeros, zeros, "p"), (xs, bs, ls, state_ssm_re[j], state_ssm_im[j], "s"))
            new_x = []
            for x, nb, length, s_re, s_im, tag in streams:
                h_tm = rmsnorm_time_major(x, mix_norm[i], nb, length)
                a_tm, sf = ssm_core(h_tm, mats, ssm_pack_state(s_re, s_im, nblk), ssm_d[j], nb, length)
                f_re, f_im = ssm_unpack_state(sf, g, p)
                outs[tag + "_ssm_re"].append(f_re); outs[tag + "_ssm_im"].append(f_im)
                new_x.append(ssm_glu_out(x, a_tm, w_glu, j, nb, length))
            xp, xs = new_x
        elif kind == 1:
            wf_t = fox_w_in[j][:, 3 * d:].T.astype(BF16)
            qp, kp, vp, kbp, vbp, lfp = qkv_project(xp, mix_norm[i], fox_in, j, scale, wf_t, fox_b_f[j])
            qs, ks, vs, kbs, vbs, lfs = qkv_project(xs, mix_norm[i], fox_in, j, scale, wf_t, fox_b_f[j])
            lfp = lfp.reshape(nh, bp, lp).transpose(1, 0, 2)
            lfs = lfs.reshape(nh, bs, ls).transpose(1, 0, 2)
            cp = cumsum_time(lfp).reshape(bp * nh, 1, lp)
            lf_all = jnp.concatenate([cache_fox_logf[j].astype(F32).transpose(0, 2, 1), lfs,
                                      jnp.zeros((bs, nh, LANES - ls), F32)], axis=2)
            cs = cumsum_time(lf_all)
            op = fox_prompt_attention(qp, kbp, vbp, cp, bp, nh)
            os_ = fox_sample_attention(qs, cache_fox_k, cache_fox_v, j, kbs, vbs, cs, bs, nh)
            xp = out_project(xp, op, fox_out, j)
            xs = out_project(xs, os_, fox_out, j)
            outs["p_fox_k"].append(heads(kp, bp, lp)); outs["p_fox_v"].append(heads(vp, bp, lp))
            outs["p_fox_logf"].append(lfp.transpose(0, 2, 1))
            outs["s_fox_k"].append(heads(ks, bs, ls)); outs["s_fox_v"].append(heads(vs, bs, ls))
            outs["s_fox_logf"].append(lfs.transpose(0, 2, 1))
        else:
            qp, kp, vp, kbp, vbp = qkv_project(xp, mix_norm[i], band_in, j, scale)
            qs, ks, vs, kbs, vbs = qkv_project(xs, mix_norm[i], band_in, j, scale)
            bw = BAND_PREV * CHUNK
            bias = band_bias_tile(band_rel_bias[j], BAND_TQ, bw)
            op = band_prompt_attention(qp, kbp, vbp, bias, bp, nh)
            os_ = band_sample_attention(qs, cache_band_k, cache_band_v, j, kbs, vbs,
                                        bias[:, :ls, bw - window:bw + ls], bs, nh)
            xp = out_project(xp, op, band_out, j)
            xs = out_project(xs, os_, band_out, j)
            keep = min(bw, lp)
            outs["p_band_k"].append(heads(kp, bp, lp)[:, lp - keep:])
            outs["p_band_v"].append(heads(vp, bp, lp)[:, lp - keep:])
            outs["s_band_k"].append(heads(ks, bs, ls)); outs["s_band_v"].append(heads(vs, bs, ls))
        xp = ffn_half_step(xp, ffn2_norm[i], w2_in, w2_out, i)
        xs = ffn_half_step(xs, ffn2_norm[i], w2_in, w2_out, i)

    y_prompt = rmsnorm(xp, final_norm).reshape(bp, lp, d)
    y_sample = rmsnorm(xs, final_norm).reshape(bs, ls, d)
    st = {k: jnp.stack(v) for k, v in outs.items()}
    return (y_prompt, y_sample, st["p_ssm_re"], st["p_ssm_im"], st["p_fox_k"], st["p_fox_v"],
            st["p_fox_logf"], st["p_band_k"], st["p_band_v"], st["s_ssm_re"], st["s_ssm_im"],
            st["s_fox_k"], st["s_fox_v"], st["s_fox_logf"], st["s_band_k"], st["s_band_v"])
```

```python
import functools

import jax
import jax.numpy as jnp
import numpy as np
from jax import lax
from jax.experimental import pallas as pl
from jax.experimental.pallas import tpu as pltpu

F32 = jnp.float32
BF16 = jnp.bfloat16

EPS = 1e-6
NEG_INF = -1e30
CHUNK = 64
BAND_PREV = 8
REL_CLIP = 256
SSM_GROUP = 16
SSM_T = 8
LANES = 128
VMEM_LIMIT = 56 * 1024 * 1024

TM = 512
TM_FFN = 1024
TF = 512
TF_CAST = 256
TM_QKV = 1024
TN = 512
BAND_TQ = 256
FOX_TQ = 256
FOX_TK = 512
SSM_ROWS = 512


def _params(*sem):
    return pltpu.CompilerParams(dimension_semantics=sem, vmem_limit_bytes=VMEM_LIMIT)


def _rms(x, g):
    return x * lax.rsqrt(jnp.mean(x * x, axis=-1, keepdims=True) + EPS) * g


def _dot(a, b):
    return jnp.dot(a, b, preferred_element_type=F32)


def _dot_nt(a, b):
    return lax.dot_general(a, b, (((1,), (1,)), ((), ())), preferred_element_type=F32)


def _ffn_body(x_ref, g_ref, wg_ref, wu_ref, wo_ref, o_ref, h_ref):
    f = pl.program_id(1)

    @pl.when(f == 0)
    def _():
        h_ref[...] = _rms(x_ref[...], g_ref[...]).astype(BF16)
        o_ref[...] = jnp.zeros_like(o_ref)

    h = h_ref[...]
    gate = _dot(h, wg_ref[...])
    up = _dot(h, wu_ref[...])
    act = (gate * jax.nn.sigmoid(gate) * up).astype(BF16)
    o_ref[...] += _dot(act, wo_ref[...])

    @pl.when(f == pl.num_programs(1) - 1)
    def _():
        o_ref[...] = x_ref[...] + 0.5 * o_ref[...]


def ffn_half_step(x, g, w_gate, w_up, w_out):
    m, d = x.shape
    f = w_out.shape[0]
    tm = min(TM_FFN, m)
    tf = min(TF, f)
    return pl.pallas_call(
        _ffn_body,
        grid=(m // tm, f // tf),
        in_specs=[
            pl.BlockSpec((tm, d), lambda i, j: (i, 0)),
            pl.BlockSpec((1, d), lambda i, j: (0, 0)),
            pl.BlockSpec((d, tf), lambda i, j: (0, j)),
            pl.BlockSpec((d, tf), lambda i, j: (0, j)),
            pl.BlockSpec((tf, d), lambda i, j: (j, 0)),
        ],
        out_specs=pl.BlockSpec((tm, d), lambda i, j: (i, 0)),
        out_shape=jax.ShapeDtypeStruct((m, d), F32),
        scratch_shapes=[pltpu.VMEM((tm, d), BF16)],
        compiler_params=_params("parallel", "arbitrary"),
        name="ffn_half_step",
    )(x, g.reshape(1, d), w_gate, w_up, w_out)


def _ffn_cast_body(x_ref, g_ref, wg32_ref, wu32_ref, wo32_ref, o_ref, wg_ref, wu_ref, wo_ref, h_ref):
    wg_ref[...] = wg32_ref[...].astype(BF16)
    wu_ref[...] = wu32_ref[...].astype(BF16)
    wo_ref[...] = wo32_ref[...].astype(BF16)
    _ffn_body(x_ref, g_ref, wg_ref, wu_ref, wo_ref, o_ref, h_ref)


def ffn_half_step_casting(x, g, w_in, w_out, layer):
    m, d = x.shape
    f = w_out.shape[1]
    tf = min(TF_CAST, f)
    nf = f // tf
    once = dict(pipeline_mode=pl.Buffered(1))
    col = pl.BlockSpec((d, tf), lambda i, j: (0, j))
    return pl.pallas_call(
        _ffn_cast_body,
        grid=(1, nf),
        in_specs=[
            pl.BlockSpec((m, d), lambda i, j: (0, 0), **once),
            pl.BlockSpec((1, d), lambda i, j: (0, 0)),
            pl.BlockSpec((None, d, tf), lambda i, j: (layer, 0, j)),
            pl.BlockSpec((None, d, tf), lambda i, j: (layer, 0, j + nf)),
            pl.BlockSpec((None, tf, d), lambda i, j: (layer, j, 0)),
        ],
        out_specs=[pl.BlockSpec((m, d), lambda i, j: (0, 0)), col, col,
                   pl.BlockSpec((tf, d), lambda i, j: (j, 0))],
        out_shape=[jax.ShapeDtypeStruct((m, d), F32), jax.ShapeDtypeStruct((d, f), BF16),
                   jax.ShapeDtypeStruct((d, f), BF16), jax.ShapeDtypeStruct((f, d), BF16)],
        scratch_shapes=[pltpu.VMEM((m, d), BF16)],
        compiler_params=_params("arbitrary", "arbitrary"),
        name="ffn_half_step_casting",
    )(x, g.reshape(1, d), w_in, w_in, w_out)


def _rmsnorm_body(x_ref, g_ref, o_ref):
    o_ref[...] = _rms(x_ref[...], g_ref[...]).astype(o_ref.dtype)


def rmsnorm(x, g):
    m, d = x.shape
    tm = min(TM, m)
    return pl.pallas_call(
        _rmsnorm_body,
        grid=(m // tm,),
        in_specs=[pl.BlockSpec((tm, d), lambda i: (i, 0)),
                  pl.BlockSpec((1, d), lambda i: (0, 0))],
        out_specs=pl.BlockSpec((tm, d), lambda i: (i, 0)),
        out_shape=jax.ShapeDtypeStruct((m, d), F32),
        compiler_params=_params("parallel"),
        name="rmsnorm",
    )(x, g.reshape(1, d))


def rmsnorm_time_major(x, g, nb, length):
    d = x.shape[1]
    tm = min(TM, length)
    nlt = length // tm
    return pl.pallas_call(
        _rmsnorm_body,
        grid=(nb, nlt),
        in_specs=[pl.BlockSpec((tm, d), lambda b, l: (b * nlt + l, 0)),
                  pl.BlockSpec((1, d), lambda b, l: (0, 0))],
        out_specs=pl.BlockSpec((tm, d), lambda b, l: (l, b)),
        out_shape=jax.ShapeDtypeStruct((length, nb * d), F32),
        compiler_params=_params("parallel", "parallel"),
        name="rmsnorm_time_major",
    )(x, g.reshape(1, d))


def _qkv_body(scale, has_forget, *refs):
    if has_forget:
        (x_ref, g_ref, wq_ref, wk_ref, wv_ref, wf_ref, bf_ref,
         q_ref, k32_ref, v32_ref, kb_ref, vb_ref, lf_ref, h_ref) = refs
    else:
        (x_ref, g_ref, wq_ref, wk_ref, wv_ref,
         q_ref, k32_ref, v32_ref, kb_ref, vb_ref, h_ref) = refs

    @pl.when(pl.program_id(1) == 0)
    def _():
        h = _rms(x_ref[...], g_ref[...]).astype(BF16)
        h_ref[...] = h
        if has_forget:
            z = _dot_nt(wf_ref[...], h) + bf_ref[...]
            lf_ref[...] = jnp.minimum(z, 0.0) - jnp.log(1.0 + jnp.exp(-jnp.abs(z)))

    h = h_ref[...]
    q_ref[...] = (_dot(h, wq_ref[...]) * scale).astype(BF16)
    k = _dot(h, wk_ref[...])
    k32_ref[...] = k
    kb_ref[...] = k.astype(BF16)
    v = _dot(h, wv_ref[...])
    v32_ref[...] = v
    vb_ref[...] = v.astype(BF16)


def qkv_project(x, g, w, layer, scale, wf_t=None, b_f=None):
    m, d = x.shape
    tm = min(TM_QKV, m)
    tn = min(TN, d)
    nq = d // tn
    has_forget = wf_t is not None
    in_specs = [
        pl.BlockSpec((tm, d), lambda i, n: (i, 0)),
        pl.BlockSpec((1, d), lambda i, n: (0, 0)),
        pl.BlockSpec((None, d, tn), lambda i, n: (layer, 0, n)),
        pl.BlockSpec((None, d, tn), lambda i, n: (layer, 0, n + nq)),
        pl.BlockSpec((None, d, tn), lambda i, n: (layer, 0, n + 2 * nq)),
    ]
    args = [x, g.reshape(1, d), w, w, w]
    tile = pl.BlockSpec((tm, tn), lambda i, n: (i, n))
    out_specs = [tile] * 5
    out_shape = [jax.ShapeDtypeStruct((m, d), BF16),
                 jax.ShapeDtypeStruct((m, d), F32), jax.ShapeDtypeStruct((m, d), F32),
                 jax.ShapeDtypeStruct((m, d), BF16), jax.ShapeDtypeStruct((m, d), BF16)]
    if has_forget:
        nh = wf_t.shape[0]
        in_specs += [pl.BlockSpec((nh, d), lambda i, n: (0, 0)),
                     pl.BlockSpec((nh, 1), lambda i, n: (0, 0))]
        args += [wf_t, b_f.reshape(nh, 1)]
        out_specs = out_specs + [pl.BlockSpec((nh, tm), lambda i, n: (0, i))]
        out_shape = out_shape + [jax.ShapeDtypeStruct((nh, m), F32)]
    return pl.pallas_call(
        functools.partial(_qkv_body, scale, has_forget),
        grid=(m // tm, nq),
        in_specs=in_specs,
        out_specs=out_specs,
        out_shape=out_shape,
        scratch_shapes=[pltpu.VMEM((tm, d), BF16)],
        compiler_params=_params("parallel", "arbitrary"),
        name="qkv_project",
    )(*args)


def _out_proj_body(x_ref, o_ref, w_ref, y_ref):
    y_ref[...] = x_ref[...] + _dot(o_ref[...], w_ref[...])


def out_project(x, o, w, layer):
    m, d = x.shape
    tm = min(TM, m)
    return pl.pallas_call(
        _out_proj_body,
        grid=(m // tm,),
        in_specs=[pl.BlockSpec((tm, d), lambda i: (i, 0)),
                  pl.BlockSpec((tm, d), lambda i: (i, 0)),
                  pl.BlockSpec((None, d, d), lambda i: (layer, 0, 0))],
        out_specs=pl.BlockSpec((tm, d), lambda i: (i, 0)),
        out_shape=jax.ShapeDtypeStruct((m, d), F32),
        compiler_params=_params("parallel"),
        name="out_project",
    )(x, o, w)


def _glu_body(a_ref, xt_ref, wv_ref, wg_ref, o_ref):
    a = a_ref[...].astype(BF16)
    val = _dot(a, wv_ref[...])
    gate = _dot(a, wg_ref[...])
    o_ref[...] = xt_ref[...] + val * jax.nn.sigmoid(gate)


def ssm_glu_out(x, a_tm, w_glu, layer, nb, length):
    d = x.shape[1]
    tm = min(TM, length)
    nlt = length // tm
    tn = min(TN, d)
    nq = d // tn
    tile = pl.BlockSpec((tm, tn), lambda n, b, l: (b * nlt + l, n))
    return pl.pallas_call(
        _glu_body,
        grid=(nq, nb, nlt),
        in_specs=[
            pl.BlockSpec((tm, d), lambda n, b, l: (l, b)),
            tile,
            pl.BlockSpec((None, d, tn), lambda n, b, l: (layer, 0, n)),
            pl.BlockSpec((None, d, tn), lambda n, b, l: (layer, 0, n + nq)),
        ],
        out_specs=tile,
        out_shape=jax.ShapeDtypeStruct((nb * length, d), F32),
        compiler_params=_params("parallel", "parallel", "parallel"),
        name="ssm_glu_out",
    )(a_tm, x, w_glu, w_glu)


def _ssm_lag_body(ca_ref, bb_ref, k_ref):
    for i in range(ca_ref.shape[0]):
        k_ref[i] = jnp.dot(ca_ref[i], bb_ref[i], preferred_element_type=F32,
                           precision=lax.Precision.HIGHEST)


def ssm_lag_kernels(ca, bb, gpb):
    g, r, p2 = ca.shape
    c = bb.shape[2]
    return pl.pallas_call(
        _ssm_lag_body,
        grid=(g // gpb,),
        in_specs=[pl.BlockSpec((gpb, r, p2), lambda i: (i, 0, 0)),
                  pl.BlockSpec((gpb, p2, c), lambda i: (i, 0, 0))],
        out_specs=pl.BlockSpec((gpb, r, c), lambda i: (i, 0, 0)),
        out_shape=jax.ShapeDtypeStruct((g, r, c), F32),
        compiler_params=_params("parallel"),
        name="ssm_lag_kernels",
    )(ca, bb)


def _ssm_core_body(nb, x_ref, bdv_ref, ws_ref, wc_ref, at_ref, s0_ref, dsk_ref, y_ref, sf_ref,
                   ds_ref, dsw_ref, sst_ref, st_ref):
    nchp, t = x_ref.shape[0], x_ref.shape[1]
    m = nchp * nb
    ncol = at_ref.shape[2]
    half = LANES // 2

    def swap_halves(v):
        lane = lax.broadcasted_iota(jnp.int32, v.shape, 1)
        return jnp.where(lane % LANES < half, pltpu.roll(v, ncol - half, 1), pltpu.roll(v, half, 1))

    @pl.when(pl.program_id(1) == 0)
    def _():
        st_ref[...] = s0_ref[0]

    xcat = jnp.concatenate(
        [x_ref[:, tau].reshape(m, LANES).astype(BF16) for tau in range(t)], axis=1)
    ds = _dot(xcat, ws_ref[0])
    ds_ref[...] = ds
    dsw_ref[...] = swap_halves(ds)
    a1 = jnp.broadcast_to(at_ref[0, 0:1, :], (nb, ncol))
    a2 = jnp.broadcast_to(at_ref[0, 1:2, :], (nb, ncol))

    def step(k, carry):
        s, sw = carry
        rows = pl.ds(pl.multiple_of(k * nb, nb), nb)
        sst_ref[rows, :] = s
        ns = a1 * s + a2 * sw + ds_ref[rows, :]
        nsw = a1 * sw - a2 * s + dsw_ref[rows, :]
        return ns, nsw

    s0 = st_ref[...]
    s, _ = lax.fori_loop(0, nchp, step, (s0, swap_halves(s0)))
    st_ref[...] = s
    sf_ref[0] = s
    y_in = _dot(sst_ref[...].astype(BF16), wc_ref[0])
    dsk = dsk_ref[...]
    for tt in range(t):
        yt = y_in[:, tt * LANES:(tt + 1) * LANES] + _dot(
            xcat[:, :(tt + 1) * LANES], bdv_ref[0, (t - 1 - tt) * LANES:, :])
        yt = jax.nn.gelu(yt + dsk * x_ref[:, tt].reshape(m, LANES))
        y_ref[:, tt] = yt.reshape(nchp, nb, LANES)


def ssm_core(h_tm, mats, s0, d_skip, nb, length):
    bdv, ws, wc, at = mats
    d = h_tm.shape[1] // nb
    t = SSM_T
    nch = length // t
    nblk = d // LANES
    nchp = max(min(SSM_ROWS // nb, nch), 1)
    m = nchp * nb
    ncol = ws.shape[2]
    x4 = h_tm.reshape(nch, t, nb, d)
    wblk = lambda *s: pl.BlockSpec((1,) + s, lambda c, k: (c, 0, 0))
    seq = pl.BlockSpec((nchp, t, nb, LANES), lambda c, k: (k, 0, 0, c))
    y, sf = pl.pallas_call(
        functools.partial(_ssm_core_body, nb),
        grid=(nblk, nch // nchp),
        in_specs=[seq, wblk(t * LANES, LANES), wblk(t * LANES, ncol), wblk(ncol, t * LANES),
                  wblk(2, ncol), wblk(nb, ncol), pl.BlockSpec((1, LANES), lambda c, k: (0, c))],
        out_specs=[seq, wblk(nb, ncol)],
        out_shape=[jax.ShapeDtypeStruct((nch, t, nb, d), F32),
                   jax.ShapeDtypeStruct((nblk, nb, ncol), F32)],
        scratch_shapes=[pltpu.VMEM((m, ncol), F32), pltpu.VMEM((m, ncol), F32), pltpu.VMEM((m, ncol), F32),
                        pltpu.VMEM((nb, ncol), F32)],
        compiler_params=_params("parallel", "arbitrary"),
        name="ssm_core",
    )(x4, bdv, ws, wc, at, s0, d_skip.reshape(1, d))
    return y.reshape(length, nb * d), sf


def ssm_matrices(a_re, a_im, log_step, b_re, b_im, c_re, c_im):
    t = SSM_T
    g, p = a_re.shape
    c = b_re.shape[2]
    gpb = LANES // c
    nblk = g // gpb
    dt = jnp.exp(log_step)[:, None]
    lam_re, lam_im = a_re * dt, a_im * dt
    j = jnp.arange(t + 1, dtype=F32)[:, None, None]
    mag = jnp.exp(lam_re * j)
    pw_re, pw_im = mag * jnp.cos(lam_im * j), mag * jnp.sin(lam_im * j)
    ab_re, ab_im = pw_re[1], pw_im[1]
    den = a_re * a_re + a_im * a_im
    z_re = ((ab_re - 1.0) * a_re + ab_im * a_im) / den
    z_im = (ab_im * a_re - (ab_re - 1.0) * a_im) / den
    bb_re = z_re[..., None] * b_re - z_im[..., None] * b_im
    bb_im = z_re[..., None] * b_im + z_im[..., None] * b_re
    jr = (t - 1) - jnp.arange(t, dtype=F32)[:, None, None]
    mag_r = jnp.exp(lam_re * jr)
    rp_re, rp_im = mag_r * jnp.cos(lam_im * jr), mag_r * jnp.sin(lam_im * jr)
    ws_re = rp_re[:, :, :, None] * bb_re[None] - rp_im[:, :, :, None] * bb_im[None]
    ws_im = rp_re[:, :, :, None] * bb_im[None] + rp_im[:, :, :, None] * bb_re[None]
    wsc = jnp.concatenate([ws_re, ws_im], axis=2).transpose(1, 0, 3, 2).reshape(nblk, gpb, t * c, 2 * p)

    def out_weights(q_re, q_im):
        cr, ci = c_re[None], c_im[None]
        qr, qi = q_re[:, :, None, :], q_im[:, :, None, :]
        return cr * qr - ci * qi, -cr * qi - ci * qr

    wc_re, wc_im = out_weights(pw_re[1:], pw_im[1:])
    wcc = jnp.concatenate([wc_re, wc_im], axis=3).transpose(1, 3, 0, 2).reshape(nblk, gpb, 2 * p, t * c)

    ca_re, ca_im = out_weights(rp_re, rp_im)
    ca = jnp.concatenate([ca_re, ca_im], axis=-1).transpose(1, 0, 2, 3).reshape(g, t * c, 2 * p)
    bb = jnp.concatenate([bb_re, bb_im], axis=1)
    kmat = ssm_lag_kernels(ca, bb, gpb).reshape(g, t, c, c)
    kc = kmat.transpose(0, 1, 3, 2).reshape(nblk, gpb, t * c, c)

    place = np.zeros((gpb, t, gpb, c, t, c), np.float32)
    for h in range(gpb):
        place[h, :, h] = np.eye(t * c, dtype=np.float32).reshape(t, c, t, c)
    place = jnp.asarray(place.reshape(gpb, t * LANES, t * c), BF16)
    expand = functools.partial(jnp.einsum, preferred_element_type=F32)
    ws = expand("hrk,bhkn->brhn", place, wsc.astype(BF16)).reshape(nblk, t * LANES, gpb * 2 * p)
    wc = expand("bhnk,hrk->bhnr", wcc.astype(BF16), place).reshape(nblk, gpb * 2 * p, t * LANES)
    bdv = expand("hrk,bhkc->brhc", place, kc.astype(BF16)).reshape(nblk, t * LANES, LANES)

    at_re = pw_re[t].reshape(nblk, gpb, p)
    at_im = pw_im[t].reshape(nblk, gpb, p)
    a1 = jnp.concatenate([at_re, at_re], axis=2).reshape(nblk, gpb * 2 * p)
    a2 = jnp.concatenate([-at_im, at_im], axis=2).reshape(nblk, gpb * 2 * p)
    return bdv.astype(BF16), ws.astype(BF16), wc.astype(BF16), jnp.stack([a1, a2], axis=1)


def ssm_pack_state(s_re, s_im, nblk):
    nb, g, p = s_re.shape
    s = jnp.stack([s_re.reshape(nb, nblk, g // nblk, p), s_im.reshape(nb, nblk, g // nblk, p)], axis=3)
    return s.transpose(1, 0, 2, 3, 4).reshape(nblk, nb, -1)


def ssm_unpack_state(sf, g, p):
    nblk, nb, _ = sf.shape
    s = sf.reshape(nblk, nb, g // nblk, 2, p).transpose(1, 0, 2, 3, 4).reshape(nb, g, 2, p)
    return s[:, :, 0], s[:, :, 1]


def _split3(x):
    hi = x.astype(BF16)
    r = x - hi.astype(F32)
    mid = r.astype(BF16)
    lo = (r - mid.astype(F32)).astype(BF16)
    return hi, mid, lo


def _cumsum_body(x_ref, o_ref):
    nblk = x_ref.shape[2] // LANES
    rows = x_ref.shape[1]
    row = lax.broadcasted_iota(jnp.int32, (LANES, LANES), 0)
    col = lax.broadcasted_iota(jnp.int32, (LANES, LANES), 1)
    tri = jnp.where(row <= col, 1.0, 0.0).astype(BF16)
    carry = jnp.zeros((rows, 1), F32)
    for j in range(nblk):
        sl = slice(j * LANES, (j + 1) * LANES)
        hi, mid, lo = _split3(x_ref[0, :, sl])
        w = _dot(hi, tri) + _dot(mid, tri) + _dot(lo, tri)
        o_ref[0, :, sl] = w + carry
        carry = carry + jnp.sum(x_ref[0, :, sl], axis=1, keepdims=True)


def cumsum_time(x):
    n, h, length = x.shape
    return pl.pallas_call(
        _cumsum_body,
        grid=(n,),
        in_specs=[pl.BlockSpec((1, h, length), lambda i: (i, 0, 0))],
        out_specs=pl.BlockSpec((1, h, length), lambda i: (i, 0, 0)),
        out_shape=jax.ShapeDtypeStruct((n, h, length), F32),
        compiler_params=_params("parallel"),
        name="cumsum_time",
    )(x)


def _softmax_pv(parts):
    m = functools.reduce(jnp.maximum, [jnp.max(s, axis=-1, keepdims=True) for s, _ in parts])
    acc, den = None, None
    for s, v in parts:
        e = jnp.exp(s - m)
        l = jnp.sum(e, axis=-1, keepdims=True)
        o = _dot(e.astype(BF16), v)
        acc = o if acc is None else acc + o
        den = l if den is None else den + l
    return acc / den


def _causal(n):
    rpos = lax.broadcasted_iota(jnp.int32, (n, n), 0)
    cpos = lax.broadcasted_iota(jnp.int32, (n, n), 1)
    return cpos <= rpos


def _fox_prompt_body(tq, q_ref, k_ref, v_ref, c_ref, o_ref):
    length = q_ref.shape[0]
    keep = _causal(tq)
    for i in range(length // tq):
        lo, hi = i * tq, (i + 1) * tq
        q = q_ref[lo:hi, :]
        parts = []
        if i > 0:
            parts.append((_dot_nt(q, k_ref[0:lo, :]) - c_ref[0, :, 0:lo], v_ref[0:lo, :]))
        s = _dot_nt(q, k_ref[lo:hi, :]) - c_ref[0, :, lo:hi]
        parts.append((jnp.where(keep, s, NEG_INF), v_ref[lo:hi, :]))
        o_ref[lo:hi, :] = _softmax_pv(parts).astype(o_ref.dtype)


def fox_prompt_attention(q, kb, vb, c, nb, nh):
    m, d = q.shape
    length = m // nb
    hd = d // nh
    tq = min(FOX_TQ, length)
    seq = pl.BlockSpec((length, hd), lambda b, h: (b, h))
    return pl.pallas_call(
        functools.partial(_fox_prompt_body, tq),
        grid=(nb, nh),
        in_specs=[seq, seq, seq, pl.BlockSpec((1, 1, length), lambda b, h: (b * nh + h, 0, 0))],
        out_specs=seq,
        out_shape=jax.ShapeDtypeStruct((m, d), BF16),
        compiler_params=_params("parallel", "parallel"),
        name="fox_prompt_attention",
    )(q, kb, vb, c)


def _fox_sample_body(nh, q_ref, kc_ref, vc_ref, kn_ref, vn_ref, cc_ref, cn_ref, o_ref,
                     m_ref, l_ref, acc_ref):
    j = pl.program_id(1)
    lq, d = q_ref.shape
    hd = d // nh

    @pl.when(j == 0)
    def _():
        m_ref[...] = jnp.full_like(m_ref, NEG_INF)
        l_ref[...] = jnp.zeros_like(l_ref)
        acc_ref[...] = jnp.zeros_like(acc_ref)

    heads = lambda ref: [ref[:, h * hd:(h + 1) * hd] for h in range(nh)]
    qs = heads(q_ref)

    def update(s, vs):
        m_old = m_ref[...]
        m_new = jnp.maximum(m_old, jnp.max(s, axis=-1, keepdims=True))
        alpha = jnp.exp(m_old - m_new)
        e = jnp.exp(s - m_new)
        l_ref[...] = alpha * l_ref[...] + jnp.sum(e, axis=-1, keepdims=True)
        e = e.astype(BF16)
        pv = jnp.concatenate([_dot(e[h * lq:(h + 1) * lq], vs[h]) for h in range(nh)], axis=0)
        acc_ref[...] = alpha * acc_ref[...] + pv
        m_ref[...] = m_new

    k_all = pltpu.einshape("lhd->hld", kc_ref[...])
    v_all = pltpu.einshape("lhd->hld", vc_ref[...])
    s = jnp.concatenate([_dot_nt(qs[h], k_all[h].astype(BF16)) - cc_ref[0, h:h + 1, :]
                         for h in range(nh)], axis=0)
    update(s, [v_all[h].astype(BF16) for h in range(nh)])

    @pl.when(j == pl.num_programs(1) - 1)
    def _():
        keep = _causal(lq)
        kn = heads(kn_ref)
        s = jnp.concatenate(
            [jnp.where(keep, _dot_nt(qs[h], kn[h]) - cn_ref[0, h:h + 1, 0:lq], NEG_INF)
             for h in range(nh)], axis=0)
        update(s, heads(vn_ref))
        o = acc_ref[...] / l_ref[...]
        for h in range(nh):
            o_ref[:, h * hd:(h + 1) * hd] = o[h * lq:(h + 1) * lq].astype(o_ref.dtype)


def fox_sample_attention(q, k_cache, v_cache, layer, kb, vb, c, nb, nh):
    m, d = q.shape
    lq = m // nb
    past, hd = k_cache.shape[2], k_cache.shape[4]
    tk = min(FOX_TK, past)
    nkv = past // tk
    new = pl.BlockSpec((lq, d), lambda b, j: (b, 0))
    cache = pl.BlockSpec((None, None, tk, nh, hd), lambda b, j: (layer, b, j, 0, 0))
    return pl.pallas_call(
        functools.partial(_fox_sample_body, nh),
        grid=(nb, nkv),
        in_specs=[new, cache, cache, new, new,
                  pl.BlockSpec((1, nh, tk), lambda b, j: (b, 0, j)),
                  pl.BlockSpec((1, nh, LANES), lambda b, j: (b, 0, past // LANES))],
        out_specs=new,
        out_shape=jax.ShapeDtypeStruct((m, d), BF16),
        scratch_shapes=[pltpu.VMEM((nh * lq, 1), F32), pltpu.VMEM((nh * lq, 1), F32),
                        pltpu.VMEM((nh * lq, hd), F32)],
        compiler_params=_params("parallel", "arbitrary"),
        name="fox_sample_attention",
    )(q, k_cache, v_cache, kb, vb, c, c)


def _band_prompt_body(tq, q_ref, k_ref, v_ref, bias_ref, o_ref):
    length = q_ref.shape[0]
    span = bias_ref.shape[2]
    for i in range(length // tq):
        k0 = max((i + 1) * tq - span, 0)
        k1 = (i + 1) * tq
        q = q_ref[i * tq:(i + 1) * tq, :]
        s = _dot_nt(q, k_ref[k0:k1, :]) + bias_ref[0, :, span - (k1 - k0):span]
        o_ref[i * tq:(i + 1) * tq, :] = _softmax_pv([(s, v_ref[k0:k1, :])]).astype(o_ref.dtype)


def band_prompt_attention(q, kb, vb, bias, nb, nh):
    m, d = q.shape
    length = m // nb
    hd = d // nh
    tq = bias.shape[1]
    seq = pl.BlockSpec((length, hd), lambda b, h: (b, h))
    return pl.pallas_call(
        functools.partial(_band_prompt_body, tq),
        grid=(nb, nh),
        in_specs=[seq, seq, seq, pl.BlockSpec((1,) + bias.shape[1:], lambda b, h: (h, 0, 0))],
        out_specs=seq,
        out_shape=jax.ShapeDtypeStruct((m, d), BF16),
        compiler_params=_params("parallel", "parallel"),
        name="band_prompt_attention",
    )(q, kb, vb, bias)


def _band_sample_body(nh, q_ref, kc_ref, vc_ref, kn_ref, vn_ref, bias_ref, o_ref):
    lq, d = q_ref.shape
    hd = d // nh
    lc = kc_ref.shape[0]
    k_all = pltpu.einshape("lhd->hld", kc_ref[...])
    v_all = pltpu.einshape("lhd->hld", vc_ref[...])
    heads = lambda ref: [ref[:, h * hd:(h + 1) * hd] for h in range(nh)]
    qs, kn, vn = heads(q_ref), heads(kn_ref), heads(vn_ref)
    s_c = jnp.concatenate([_dot_nt(qs[h], k_all[h].astype(BF16)) + bias_ref[h, :, 0:lc]
                           for h in range(nh)], axis=0)
    s_n = jnp.concatenate([_dot_nt(qs[h], kn[h]) + bias_ref[h, :, lc:lc + lq] for h in range(nh)], axis=0)
    m = jnp.maximum(jnp.max(s_c, axis=-1, keepdims=True), jnp.max(s_n, axis=-1, keepdims=True))
    e_c, e_n = jnp.exp(s_c - m), jnp.exp(s_n - m)
    inv = 1.0 / (jnp.sum(e_c, axis=-1, keepdims=True) + jnp.sum(e_n, axis=-1, keepdims=True))
    e_c, e_n = e_c.astype(BF16), e_n.astype(BF16)
    for h in range(nh):
        rows = slice(h * lq, (h + 1) * lq)
        o = _dot(e_c[rows], v_all[h].astype(BF16)) + _dot(e_n[rows], vn[h])
        o_ref[:, h * hd:(h + 1) * hd] = (o * inv[rows]).astype(o_ref.dtype)


def band_sample_attention(q, k_cache, v_cache, layer, kb, vb, bias, nb, nh):
    m, d = q.shape
    lq = m // nb
    lc, hd = k_cache.shape[2], k_cache.shape[4]
    new = pl.BlockSpec((lq, d), lambda b: (b, 0))
    cache = pl.BlockSpec((None, None, lc, nh, hd), lambda b: (layer, b, 0, 0, 0))
    return pl.pallas_call(
        functools.partial(_band_sample_body, nh),
        grid=(nb,),
        in_specs=[new, cache, cache, new, new, pl.BlockSpec(bias.shape, lambda b: (0, 0, 0))],
        out_specs=new,
        out_shape=jax.ShapeDtypeStruct((m, d), BF16),
        compiler_params=_params("parallel"),
        name="band_sample_attention",
    )(q, k_cache, v_cache, kb, vb, bias)


def band_bias_tile(rel_bias, tq, window):
    nh = rel_bias.shape[0]
    span = window + tq
    n = span + tq
    rel = (span - 1) - np.arange(n)
    diag = rel_bias.astype(F32)[:, np.clip(rel, -REL_CLIP, REL_CLIP) + REL_CLIP]
    skew = jnp.broadcast_to(diag[:, None, :], (nh, tq, n)).reshape(nh, tq * n)
    skew = skew[:, :tq * (n - 1)].reshape(nh, tq, n - 1)[:, :, tq - 1:tq - 1 + span]
    qpos = np.arange(tq)[:, None]
    kpos = np.arange(span)[None, :] - window
    qc, kc = qpos // CHUNK, kpos // CHUNK
    visible = (kc <= qc) & (kc >= qc - BAND_PREV)
    return jnp.where(visible[None], skew, NEG_INF)


def kernel(x_prompt, x_sample, state_ssm_re, state_ssm_im, cache_fox_k, cache_fox_v, cache_fox_logf,
           cache_band_k, cache_band_v, ffn1_norm, ffn1_w_in, ffn1_w_out, mix_norm, ffn2_norm, ffn2_w_in,
           ffn2_w_out, ssm_a_re, ssm_a_im, ssm_log_step, ssm_b_re, ssm_b_im, ssm_c_re, ssm_c_im, ssm_d,
           ssm_w_glu, fox_w_in, fox_b_f, fox_w_out, band_w_in, band_rel_bias, band_w_out, final_norm):
    bp, lp, d = x_prompt.shape
    bs, ls, _ = x_sample.shape
    depth = ffn1_norm.shape[0]
    nh = fox_b_f.shape[1]
    hd = d // nh
    scale = hd ** -0.5
    past = cache_fox_k.shape[2]
    window = cache_band_k.shape[2]
    heads = lambda t, nb, length: t.reshape(nb, length, nh, hd)

    xp = x_prompt.reshape(bp * lp, d)
    xs = x_sample.reshape(bs * ls, d)
    outs = {name: [] for name in (
        "p_ssm_re", "p_ssm_im", "p_fox_k", "p_fox_v", "p_fox_logf", "p_band_k", "p_band_v",
        "s_ssm_re", "s_ssm_im", "s_fox_k", "s_fox_v", "s_fox_logf", "s_band_k", "s_band_v")}

    w_glu = ssm_w_glu.astype(BF16)
    fox_in, fox_out = fox_w_in.astype(BF16), fox_w_out.astype(BF16)
    band_in, band_out = band_w_in.astype(BF16), band_w_out.astype(BF16)

    def ffn(xp, xs, g, w_in, w_out, layer):
        xs, w_gate, w_up, w_down = ffn_half_step_casting(xs, g, w_in, w_out, layer)
        return ffn_half_step(xp, g, w_gate, w_up, w_down), xs

    for i in range(depth):
        kind, j = i % 3, i // 3
        xp, xs = ffn(xp, xs, ffn1_norm[i], ffn1_w_in, ffn1_w_out, i)
        if kind == 0:
            mats = ssm_matrices(ssm_a_re[j], ssm_a_im[j], ssm_log_step[j], ssm_b_re[j], ssm_b_im[j],
                                ssm_c_re[j], ssm_c_im[j])
            g, p = ssm_a_re.shape[1:]
            nblk = d // LANES
            zeros = jnp.zeros((bp, g, p), F32)
            streams = ((xp, bp, lp, zeros, zeros, "p"), (xs, bs, ls, state_ssm_re[j], state_ssm_im[j], "s"))
            new_x = []
            for x, nb, length, s_re, s_im, tag in streams:
                h_tm = rmsnorm_time_major(x, mix_norm[i], nb, length)
                a_tm, sf = ssm_core(h_tm, mats, ssm_pack_state(s_re, s_im, nblk), ssm_d[j], nb, length)
                f_re, f_im = ssm_unpack_state(sf, g, p)
                outs[tag + "_ssm_re"].append(f_re); outs[tag + "_ssm_im"].append(f_im)
                new_x.append(ssm_glu_out(x, a_tm, w_glu, j, nb, length))
            xp, xs = new_x
        elif kind == 1:
            wf_t = fox_w_in[j][:, 3 * d:].T.astype(BF16)
            qp, kp, vp, kbp, vbp, lfp = qkv_project(xp, mix_norm[i], fox_in, j, scale, wf_t, fox_b_f[j])
            qs, ks, vs, kbs, vbs, lfs = qkv_project(xs, mix_norm[i], fox_in, j, scale, wf_t, fox_b_f[j])
            lfp = lfp.reshape(nh, bp, lp).transpose(1, 0, 2)
            lfs = lfs.reshape(nh, bs, ls).transpose(1, 0, 2)
            cp = cumsum_time(lfp).reshape(bp * nh, 1, lp)
            lf_all = jnp.concatenate([cache_fox_logf[j].astype(F32).transpose(0, 2, 1), lfs,
                                      jnp.zeros((bs, nh, LANES - ls), F32)], axis=2)
            cs = cumsum_time(lf_all)
            op = fox_prompt_attention(qp, kbp, vbp, cp, bp, nh)
            os_ = fox_sample_attention(qs, cache_fox_k, cache_fox_v, j, kbs, vbs, cs, bs, nh)
            xp = out_project(xp, op, fox_out, j)
            xs = out_project(xs, os_, fox_out, j)
            outs["p_fox_k"].append(heads(kp, bp, lp)); outs["p_fox_v"].append(heads(vp, bp, lp))
            outs["p_fox_logf"].append(lfp.transpose(0, 2, 1))
            outs["s_fox_k"].append(heads(ks, bs, ls)); outs["s_fox_v"].append(heads(vs, bs, ls))
            outs["s_fox_logf"].append(lfs.transpose(0, 2, 1))
        else:
            qp, kp, vp, kbp, vbp = qkv_project(xp, mix_norm[i], band_in, j, scale)
            qs, ks, vs, kbs, vbs = qkv_project(xs, mix_norm[i], band_in, j, scale)
            bw = BAND_PREV * CHUNK
            bias = band_bias_tile(band_rel_bias[j], BAND_TQ, bw)
            op = band_prompt_attention(qp, kbp, vbp, bias, bp, nh)
            os_ = band_sample_attention(qs, cache_band_k, cache_band_v, j, kbs, vbs,
                                        bias[:, :ls, bw - window:bw + ls], bs, nh)
            xp = out_project(xp, op, band_out, j)
            xs = out_project(xs, os_, band_out, j)
            keep = min(bw, lp)
            outs["p_band_k"].append(heads(kp, bp, lp)[:, lp - keep:])
            outs["p_band_v"].append(heads(vp, bp, lp)[:, lp - keep:])
            outs["s_band_k"].append(heads(ks, bs, ls)); outs["s_band_v"].append(heads(vs, bs, ls))
        xp, xs = ffn(xp, xs, ffn2_norm[i], ffn2_w_in, ffn2_w_out, i)

    y_prompt = rmsnorm(xp, final_norm).reshape(bp, lp, d)
    y_sample = rmsnorm(xs, final_norm).reshape(bs, ls, d)
    st = {k: jnp.stack(v) for k, v in outs.items()}
    return (y_prompt, y_sample, st["p_ssm_re"], st["p_ssm_im"], st["p_fox_k"], st["p_fox_v"],
            st["p_fox_logf"], st["p_band_k"], st["p_band_v"], st["s_ssm_re"], st["s_ssm_im"],
            st["s_fox_k"], st["s_fox_v"], st["s_fox_logf"], st["s_band_k"], st["s_band_v"])
```

```python
import functools

import jax
import jax.numpy as jnp
import numpy as np
from jax import lax
from jax.experimental import pallas as pl
from jax.experimental.pallas import tpu as pltpu

F32 = jnp.float32
BF16 = jnp.bfloat16

EPS = 1e-6
NEG_INF = -1e30
LOG2E = 1.4426950408889634
CHUNK = 64
BAND_PREV = 8
REL_CLIP = 256
SSM_GROUP = 16
SSM_T = 8
LANES = 128
VMEM_LIMIT = 56 * 1024 * 1024

TM = 512
TM_FFN = 1024
TF = 512
TF_CAST = 256
TM_QKV = 1024
TN = 512
BAND_TQ = 256
FOX_TQ = 256
FOX_TK = 512
SSM_ROWS = 512


def _params(*sem):
    return pltpu.CompilerParams(dimension_semantics=sem, vmem_limit_bytes=VMEM_LIMIT)


def _rms(x, g):
    return x * lax.rsqrt(jnp.mean(x * x, axis=-1, keepdims=True) + EPS) * g


def _dot(a, b):
    return jnp.dot(a, b, preferred_element_type=F32)


def _dot_nt(a, b):
    return lax.dot_general(a, b, (((1,), (1,)), ((), ())), preferred_element_type=F32)


def _ffn_body(x_ref, g_ref, wg_ref, wu_ref, wo_ref, o_ref, h_ref):
    f = pl.program_id(1)

    @pl.when(f == 0)
    def _():
        h_ref[...] = _rms(x_ref[...], g_ref[...]).astype(BF16)
        o_ref[...] = jnp.zeros_like(o_ref)

    h = h_ref[...]
    gate = _dot(h, wg_ref[...])
    up = _dot(h, wu_ref[...])
    act = (gate * jax.nn.sigmoid(gate) * up).astype(BF16)
    o_ref[...] += _dot(act, wo_ref[...])

    @pl.when(f == pl.num_programs(1) - 1)
    def _():
        o_ref[...] = x_ref[...] + 0.5 * o_ref[...]


def ffn_half_step(x, g, w_gate, w_up, w_out):
    m, d = x.shape
    f = w_out.shape[0]
    tm = min(TM_FFN, m)
    tf = min(TF, f)
    return pl.pallas_call(
        _ffn_body,
        grid=(m // tm, f // tf),
        in_specs=[
            pl.BlockSpec((tm, d), lambda i, j: (i, 0)),
            pl.BlockSpec((1, d), lambda i, j: (0, 0)),
            pl.BlockSpec((d, tf), lambda i, j: (0, j)),
            pl.BlockSpec((d, tf), lambda i, j: (0, j)),
            pl.BlockSpec((tf, d), lambda i, j: (j, 0)),
        ],
        out_specs=pl.BlockSpec((tm, d), lambda i, j: (i, 0)),
        out_shape=jax.ShapeDtypeStruct((m, d), F32),
        scratch_shapes=[pltpu.VMEM((tm, d), BF16)],
        compiler_params=_params("parallel", "arbitrary"),
        name="ffn_half_step",
    )(x, g.reshape(1, d), w_gate, w_up, w_out)


def _ffn_cast_body(x_ref, g_ref, wg32_ref, wu32_ref, wo32_ref, o_ref, wg_ref, wu_ref, wo_ref, h_ref):
    wg_ref[...] = wg32_ref[...].astype(BF16)
    wu_ref[...] = wu32_ref[...].astype(BF16)
    wo_ref[...] = wo32_ref[...].astype(BF16)
    _ffn_body(x_ref, g_ref, wg_ref, wu_ref, wo_ref, o_ref, h_ref)


def ffn_half_step_casting(x, g, w_in, w_out, layer):
    m, d = x.shape
    f = w_out.shape[1]
    tf = min(TF_CAST, f)
    nf = f // tf
    once = dict(pipeline_mode=pl.Buffered(1))
    col = pl.BlockSpec((d, tf), lambda i, j: (0, j))
    return pl.pallas_call(
        _ffn_cast_body,
        grid=(1, nf),
        in_specs=[
            pl.BlockSpec((m, d), lambda i, j: (0, 0), **once),
            pl.BlockSpec((1, d), lambda i, j: (0, 0)),
            pl.BlockSpec((None, d, tf), lambda i, j: (layer, 0, j)),
            pl.BlockSpec((None, d, tf), lambda i, j: (layer, 0, j + nf)),
            pl.BlockSpec((None, tf, d), lambda i, j: (layer, j, 0)),
        ],
        out_specs=[pl.BlockSpec((m, d), lambda i, j: (0, 0)), col, col,
                   pl.BlockSpec((tf, d), lambda i, j: (j, 0))],
        out_shape=[jax.ShapeDtypeStruct((m, d), F32), jax.ShapeDtypeStruct((d, f), BF16),
                   jax.ShapeDtypeStruct((d, f), BF16), jax.ShapeDtypeStruct((f, d), BF16)],
        scratch_shapes=[pltpu.VMEM((m, d), BF16)],
        compiler_params=_params("arbitrary", "arbitrary"),
        name="ffn_half_step_casting",
    )(x, g.reshape(1, d), w_in, w_in, w_out)


def _rmsnorm_body(x_ref, g_ref, o_ref):
    o_ref[...] = _rms(x_ref[...], g_ref[...]).astype(o_ref.dtype)


def rmsnorm(x, g):
    m, d = x.shape
    tm = min(TM, m)
    return pl.pallas_call(
        _rmsnorm_body,
        grid=(m // tm,),
        in_specs=[pl.BlockSpec((tm, d), lambda i: (i, 0)),
                  pl.BlockSpec((1, d), lambda i: (0, 0))],
        out_specs=pl.BlockSpec((tm, d), lambda i: (i, 0)),
        out_shape=jax.ShapeDtypeStruct((m, d), F32),
        compiler_params=_params("parallel"),
        name="rmsnorm",
    )(x, g.reshape(1, d))


def _rmsnorm_tm_body(x_ref, g_ref, o_ref):
    h = _rms(x_ref[...], g_ref[...])
    o_ref[...] = pltpu.einshape("bld->lbd", h).reshape(o_ref.shape)


def rmsnorm_time_major(x, g, nb, length):
    d = x.shape[1]
    t = SSM_T
    tl = min(TM // nb, length)
    return pl.pallas_call(
        _rmsnorm_tm_body,
        grid=(length // tl,),
        in_specs=[pl.BlockSpec((nb, tl, d), lambda l: (0, l, 0)),
                  pl.BlockSpec((1, d), lambda l: (0, 0))],
        out_specs=pl.BlockSpec((tl // t, t, nb, d), lambda l: (l, 0, 0, 0)),
        out_shape=jax.ShapeDtypeStruct((length // t, t, nb, d), F32),
        compiler_params=_params("parallel"),
        name="rmsnorm_time_major",
    )(x.reshape(nb, length, d), g.reshape(1, d))


def _qkv_body(scale, has_forget, *refs):
    if has_forget:
        (x_ref, g_ref, wq_ref, wk_ref, wv_ref, wf_ref, bf_ref,
         q_ref, k32_ref, v32_ref, kb_ref, vb_ref, lf_ref, h_ref) = refs
    else:
        (x_ref, g_ref, wq_ref, wk_ref, wv_ref,
         q_ref, k32_ref, v32_ref, kb_ref, vb_ref, h_ref) = refs

    @pl.when(pl.program_id(1) == 0)
    def _():
        h = _rms(x_ref[...], g_ref[...]).astype(BF16)
        h_ref[...] = h
        if has_forget:
            z = _dot_nt(wf_ref[...], h) + bf_ref[...]
            lf_ref[...] = jnp.minimum(z, 0.0) - jnp.log(1.0 + jnp.exp(-jnp.abs(z)))

    h = h_ref[...]
    q_ref[...] = (_dot(h, wq_ref[...]) * scale).astype(BF16)
    k = _dot(h, wk_ref[...])
    k32_ref[...] = k
    kb_ref[...] = k.astype(BF16)
    v = _dot(h, wv_ref[...])
    v32_ref[...] = v
    vb_ref[...] = v.astype(BF16)


def qkv_project(x, g, w, layer, scale, wf_t=None, b_f=None):
    m, d = x.shape
    tm = min(TM_QKV, m)
    tn = min(TN, d)
    nq = d // tn
    has_forget = wf_t is not None
    in_specs = [
        pl.BlockSpec((tm, d), lambda i, n: (i, 0)),
        pl.BlockSpec((1, d), lambda i, n: (0, 0)),
        pl.BlockSpec((None, d, tn), lambda i, n: (layer, 0, n)),
        pl.BlockSpec((None, d, tn), lambda i, n: (layer, 0, n + nq)),
        pl.BlockSpec((None, d, tn), lambda i, n: (layer, 0, n + 2 * nq)),
    ]
    args = [x, g.reshape(1, d), w, w, w]
    tile = pl.BlockSpec((tm, tn), lambda i, n: (i, n))
    out_specs = [tile] * 5
    out_shape = [jax.ShapeDtypeStruct((m, d), BF16),
                 jax.ShapeDtypeStruct((m, d), F32), jax.ShapeDtypeStruct((m, d), F32),
                 jax.ShapeDtypeStruct((m, d), BF16), jax.ShapeDtypeStruct((m, d), BF16)]
    if has_forget:
        nh = wf_t.shape[0]
        in_specs += [pl.BlockSpec((nh, d), lambda i, n: (0, 0)),
                     pl.BlockSpec((nh, 1), lambda i, n: (0, 0))]
        args += [wf_t, b_f.reshape(nh, 1)]
        out_specs = out_specs + [pl.BlockSpec((nh, tm), lambda i, n: (0, i))]
        out_shape = out_shape + [jax.ShapeDtypeStruct((nh, m), F32)]
    return pl.pallas_call(
        functools.partial(_qkv_body, scale, has_forget),
        grid=(m // tm, nq),
        in_specs=in_specs,
        out_specs=out_specs,
        out_shape=out_shape,
        scratch_shapes=[pltpu.VMEM((tm, d), BF16)],
        compiler_params=_params("parallel", "arbitrary"),
        name="qkv_project",
    )(*args)


def _out_proj_body(x_ref, o_ref, w_ref, y_ref):
    y_ref[...] = x_ref[...] + _dot(o_ref[...], w_ref[...])


def out_project(x, o, w, layer):
    m, d = x.shape
    tm = min(TM, m)
    return pl.pallas_call(
        _out_proj_body,
        grid=(m // tm,),
        in_specs=[pl.BlockSpec((tm, d), lambda i: (i, 0)),
                  pl.BlockSpec((tm, d), lambda i: (i, 0)),
                  pl.BlockSpec((None, d, d), lambda i: (layer, 0, 0))],
        out_specs=pl.BlockSpec((tm, d), lambda i: (i, 0)),
        out_shape=jax.ShapeDtypeStruct((m, d), F32),
        compiler_params=_params("parallel"),
        name="out_project",
    )(x, o, w)


def _glu_body(a_ref, xt_ref, wv_ref, wg_ref, o_ref):
    nb, tl, tn = xt_ref.shape
    a = a_ref[...].reshape(tl * nb, a_ref.shape[3]).astype(BF16)
    val = _dot(a, wv_ref[...])
    gate = _dot(a, wg_ref[...])
    r = (val * jax.nn.sigmoid(gate)).reshape(tl, nb, tn)
    o_ref[...] = xt_ref[...] + pltpu.einshape("lbn->bln", r)


def ssm_glu_out(x, a4, w_glu, layer, nb, length):
    d = x.shape[1]
    t = SSM_T
    tl = min(TM // nb, length)
    tn = min(TN, d)
    nq = d // tn
    tile = pl.BlockSpec((nb, tl, tn), lambda n, l: (0, l, n))
    return pl.pallas_call(
        _glu_body,
        grid=(nq, length // tl),
        in_specs=[
            pl.BlockSpec((tl // t, t, nb, d), lambda n, l: (l, 0, 0, 0)),
            tile,
            pl.BlockSpec((None, d, tn), lambda n, l: (layer, 0, n)),
            pl.BlockSpec((None, d, tn), lambda n, l: (layer, 0, n + nq)),
        ],
        out_specs=tile,
        out_shape=jax.ShapeDtypeStruct((nb, length, d), F32),
        compiler_params=_params("parallel", "parallel"),
        name="ssm_glu_out",
    )(a4, x.reshape(nb, length, d), w_glu, w_glu).reshape(nb * length, d)


def _ssm_lag_body(ca_ref, bb_ref, k_ref):
    for i in range(ca_ref.shape[0]):
        k_ref[i] = jnp.dot(ca_ref[i], bb_ref[i], preferred_element_type=F32,
                           precision=lax.Precision.HIGHEST)


def ssm_lag_kernels(ca, bb, gpb):
    g, r, p2 = ca.shape
    c = bb.shape[2]
    return pl.pallas_call(
        _ssm_lag_body,
        grid=(g // gpb,),
        in_specs=[pl.BlockSpec((gpb, r, p2), lambda i: (i, 0, 0)),
                  pl.BlockSpec((gpb, p2, c), lambda i: (i, 0, 0))],
        out_specs=pl.BlockSpec((gpb, r, c), lambda i: (i, 0, 0)),
        out_shape=jax.ShapeDtypeStruct((g, r, c), F32),
        compiler_params=_params("parallel"),
        name="ssm_lag_kernels",
    )(ca, bb)


def _ssm_core_body(nb, x_ref, bdv_ref, ws_ref, wc_ref, at_ref, s0_ref, dsk_ref, y_ref, sf_ref,
                   ds_ref, dsw_ref, sst_ref, st_ref):
    nchp, t = x_ref.shape[0], x_ref.shape[1]
    m = nchp * nb
    ncol = at_ref.shape[2]
    half = LANES // 2

    def swap_halves(v):
        lane = lax.broadcasted_iota(jnp.int32, v.shape, 1)
        return jnp.where(lane % LANES < half, pltpu.roll(v, ncol - half, 1), pltpu.roll(v, half, 1))

    @pl.when(pl.program_id(1) == 0)
    def _():
        st_ref[...] = s0_ref[0]

    xcat = jnp.concatenate(
        [x_ref[:, tau].reshape(m, LANES).astype(BF16) for tau in range(t)], axis=1)
    ds = _dot(xcat, ws_ref[0])
    ds_ref[...] = ds
    dsw_ref[...] = swap_halves(ds)
    a1 = jnp.broadcast_to(at_ref[0, 0:1, :], (nb, ncol))
    a2 = jnp.broadcast_to(at_ref[0, 1:2, :], (nb, ncol))

    def step(k, carry):
        s, sw = carry
        rows = pl.ds(pl.multiple_of(k * nb, nb), nb)
        sst_ref[rows, :] = s
        ns = a1 * s + a2 * sw + ds_ref[rows, :]
        nsw = a1 * sw - a2 * s + dsw_ref[rows, :]
        return ns, nsw

    s0 = st_ref[...]
    s, _ = lax.fori_loop(0, nchp, step, (s0, swap_halves(s0)))
    st_ref[...] = s
    sf_ref[0] = s
    y_in = _dot(sst_ref[...].astype(BF16), wc_ref[0])
    dsk = dsk_ref[...]
    for tt in range(t):
        yt = y_in[:, tt * LANES:(tt + 1) * LANES] + _dot(
            xcat[:, :(tt + 1) * LANES], bdv_ref[0, (t - 1 - tt) * LANES:, :])
        yt = jax.nn.gelu(yt + dsk * x_ref[:, tt].reshape(m, LANES))
        y_ref[:, tt] = yt.reshape(nchp, nb, LANES)


def ssm_core(h4, mats, s0, d_skip):
    bdv, ws, wc, at = mats
    nch, t, nb, d = h4.shape
    nblk = d // LANES
    nchp = max(min(SSM_ROWS // nb, nch), 1)
    m = nchp * nb
    ncol = ws.shape[2]
    wblk = lambda *s: pl.BlockSpec((1,) + s, lambda c, k: (c, 0, 0))
    seq = pl.BlockSpec((nchp, t, nb, LANES), lambda c, k: (k, 0, 0, c))
    y, sf = pl.pallas_call(
        functools.partial(_ssm_core_body, nb),
        grid=(nblk, nch // nchp),
        in_specs=[seq, wblk(t * LANES, LANES), wblk(t * LANES, ncol), wblk(ncol, t * LANES),
                  wblk(2, ncol), wblk(nb, ncol), pl.BlockSpec((1, LANES), lambda c, k: (0, c))],
        out_specs=[seq, wblk(nb, ncol)],
        out_shape=[jax.ShapeDtypeStruct((nch, t, nb, d), F32),
                   jax.ShapeDtypeStruct((nblk, nb, ncol), F32)],
        scratch_shapes=[pltpu.VMEM((m, ncol), F32), pltpu.VMEM((m, ncol), F32), pltpu.VMEM((m, ncol), F32),
                        pltpu.VMEM((nb, ncol), F32)],
        compiler_params=_params("parallel", "arbitrary"),
        name="ssm_core",
    )(h4, bdv, ws, wc, at, s0, d_skip.reshape(1, d))
    return y, sf


def ssm_matrices(a_re, a_im, log_step, b_re, b_im, c_re, c_im):
    t = SSM_T
    g, p = a_re.shape
    c = b_re.shape[2]
    gpb = LANES // c
    nblk = g // gpb
    dt = jnp.exp(log_step)[:, None]
    lam_re, lam_im = a_re * dt, a_im * dt
    j = jnp.arange(t + 1, dtype=F32)[:, None, None]
    mag = jnp.exp(lam_re * j)
    pw_re, pw_im = mag * jnp.cos(lam_im * j), mag * jnp.sin(lam_im * j)
    ab_re, ab_im = pw_re[1], pw_im[1]
    den = a_re * a_re + a_im * a_im
    z_re = ((ab_re - 1.0) * a_re + ab_im * a_im) / den
    z_im = (ab_im * a_re - (ab_re - 1.0) * a_im) / den
    bb_re = z_re[..., None] * b_re - z_im[..., None] * b_im
    bb_im = z_re[..., None] * b_im + z_im[..., None] * b_re
    jr = (t - 1) - jnp.arange(t, dtype=F32)[:, None, None]
    mag_r = jnp.exp(lam_re * jr)
    rp_re, rp_im = mag_r * jnp.cos(lam_im * jr), mag_r * jnp.sin(lam_im * jr)
    ws_re = rp_re[:, :, :, None] * bb_re[None] - rp_im[:, :, :, None] * bb_im[None]
    ws_im = rp_re[:, :, :, None] * bb_im[None] + rp_im[:, :, :, None] * bb_re[None]
    wsc = jnp.concatenate([ws_re, ws_im], axis=2).transpose(1, 0, 3, 2).reshape(nblk, gpb, t * c, 2 * p)

    def out_weights(q_re, q_im):
        cr, ci = c_re[None], c_im[None]
        qr, qi = q_re[:, :, None, :], q_im[:, :, None, :]
        return cr * qr - ci * qi, -cr * qi - ci * qr

    wc_re, wc_im = out_weights(pw_re[1:], pw_im[1:])
    wcc = jnp.concatenate([wc_re, wc_im], axis=3).transpose(1, 3, 0, 2).reshape(nblk, gpb, 2 * p, t * c)

    ca_re, ca_im = out_weights(rp_re, rp_im)
    ca = jnp.concatenate([ca_re, ca_im], axis=-1).transpose(1, 0, 2, 3).reshape(g, t * c, 2 * p)
    bb = jnp.concatenate([bb_re, bb_im], axis=1)
    kmat = ssm_lag_kernels(ca, bb, gpb).reshape(g, t, c, c)
    kc = kmat.transpose(0, 1, 3, 2).reshape(nblk, gpb, t * c, c)

    place = np.zeros((gpb, t, gpb, c, t, c), np.float32)
    for h in range(gpb):
        place[h, :, h] = np.eye(t * c, dtype=np.float32).reshape(t, c, t, c)
    place = jnp.asarray(place.reshape(gpb, t * LANES, t * c), BF16)
    expand = functools.partial(jnp.einsum, preferred_element_type=F32)
    ws = expand("hrk,bhkn->brhn", place, wsc.astype(BF16)).reshape(nblk, t * LANES, gpb * 2 * p)
    wc = expand("bhnk,hrk->bhnr", wcc.astype(BF16), place).reshape(nblk, gpb * 2 * p, t * LANES)
    bdv = expand("hrk,bhkc->brhc", place, kc.astype(BF16)).reshape(nblk, t * LANES, LANES)

    at_re = pw_re[t].reshape(nblk, gpb, p)
    at_im = pw_im[t].reshape(nblk, gpb, p)
    a1 = jnp.concatenate([at_re, at_re], axis=2).reshape(nblk, gpb * 2 * p)
    a2 = jnp.concatenate([-at_im, at_im], axis=2).reshape(nblk, gpb * 2 * p)
    return bdv.astype(BF16), ws.astype(BF16), wc.astype(BF16), jnp.stack([a1, a2], axis=1)


def ssm_pack_state(s_re, s_im, nblk):
    nb, g, p = s_re.shape
    s = jnp.stack([s_re.reshape(nb, nblk, g // nblk, p), s_im.reshape(nb, nblk, g // nblk, p)], axis=3)
    return s.transpose(1, 0, 2, 3, 4).reshape(nblk, nb, -1)


def ssm_unpack_state(sf, g, p):
    nblk, nb, _ = sf.shape
    s = sf.reshape(nblk, nb, g // nblk, 2, p).transpose(1, 0, 2, 3, 4).reshape(nb, g, 2, p)
    return s[:, :, 0], s[:, :, 1]


def _split3(x):
    hi = x.astype(BF16)
    r = x - hi.astype(F32)
    mid = r.astype(BF16)
    lo = (r - mid.astype(F32)).astype(BF16)
    return hi, mid, lo


def _cumsum_body(x_ref, o_ref):
    nblk = x_ref.shape[2] // LANES
    rows = x_ref.shape[1]
    row = lax.broadcasted_iota(jnp.int32, (LANES, LANES), 0)
    col = lax.broadcasted_iota(jnp.int32, (LANES, LANES), 1)
    tri = jnp.where(row <= col, 1.0, 0.0).astype(BF16)
    carry = jnp.zeros((rows, 1), F32)
    for j in range(nblk):
        sl = slice(j * LANES, (j + 1) * LANES)
        hi, mid, lo = _split3(x_ref[0, :, sl])
        w = _dot(hi, tri) + _dot(mid, tri) + _dot(lo, tri)
        o_ref[0, :, sl] = w + carry
        carry = carry + jnp.sum(x_ref[0, :, sl], axis=1, keepdims=True)


def cumsum_time(x):
    n, h, length = x.shape
    return pl.pallas_call(
        _cumsum_body,
        grid=(n,),
        in_specs=[pl.BlockSpec((1, h, length), lambda i: (i, 0, 0))],
        out_specs=pl.BlockSpec((1, h, length), lambda i: (i, 0, 0)),
        out_shape=jax.ShapeDtypeStruct((n, h, length), F32),
        compiler_params=_params("parallel"),
        name="cumsum_time",
    )(x)


def _softmax_pv(parts):
    m = functools.reduce(jnp.maximum, [jnp.max(s, axis=-1, keepdims=True) for s, _ in parts])
    acc, den = None, None
    for s, v in parts:
        e = jnp.exp2(s - m)
        l = jnp.sum(e, axis=-1, keepdims=True)
        o = _dot(e.astype(BF16), v)
        acc = o if acc is None else acc + o
        den = l if den is None else den + l
    return acc / den


def _causal(n):
    rpos = lax.broadcasted_iota(jnp.int32, (n, n), 0)
    cpos = lax.broadcasted_iota(jnp.int32, (n, n), 1)
    return cpos <= rpos


def _fox_prompt_body(tq, q_ref, k_ref, v_ref, c_ref, o_ref):
    length = q_ref.shape[0]
    keep = _causal(tq)
    for i in range(length // tq):
        lo, hi = i * tq, (i + 1) * tq
        q = q_ref[lo:hi, :]
        parts = []
        if i > 0:
            parts.append((_dot_nt(q, k_ref[0:lo, :]) - c_ref[0, :, 0:lo], v_ref[0:lo, :]))
        s = _dot_nt(q, k_ref[lo:hi, :]) - c_ref[0, :, lo:hi]
        parts.append((jnp.where(keep, s, NEG_INF), v_ref[lo:hi, :]))
        o_ref[lo:hi, :] = _softmax_pv(parts).astype(o_ref.dtype)


def fox_prompt_attention(q, kb, vb, c, nb, nh):
    m, d = q.shape
    length = m // nb
    hd = d // nh
    tq = min(FOX_TQ, length)
    seq = pl.BlockSpec((length, hd), lambda b, h: (b, h))
    return pl.pallas_call(
        functools.partial(_fox_prompt_body, tq),
        grid=(nb, nh),
        in_specs=[seq, seq, seq, pl.BlockSpec((1, 1, length), lambda b, h: (b * nh + h, 0, 0))],
        out_specs=seq,
        out_shape=jax.ShapeDtypeStruct((m, d), BF16),
        compiler_params=_params("parallel", "parallel"),
        name="fox_prompt_attention",
    )(q, kb, vb, c)


def _fox_sample_body(nh, q_ref, kc_ref, vc_ref, kn_ref, vn_ref, cc_ref, cn_ref, o_ref,
                     m_ref, l_ref, acc_ref):
    j = pl.program_id(1)
    lq, d = q_ref.shape
    hd = d // nh

    @pl.when(j == 0)
    def _():
        m_ref[...] = jnp.full_like(m_ref, NEG_INF)
        l_ref[...] = jnp.zeros_like(l_ref)
        acc_ref[...] = jnp.zeros_like(acc_ref)

    heads = lambda ref: [ref[:, h * hd:(h + 1) * hd] for h in range(nh)]
    qs = heads(q_ref)

    def update(s, vs):
        m_old = m_ref[...]
        m_new = jnp.maximum(m_old, jnp.max(s, axis=-1, keepdims=True))
        alpha = jnp.exp2(m_old - m_new)
        e = jnp.exp2(s - m_new)
        l_ref[...] = alpha * l_ref[...] + jnp.sum(e, axis=-1, keepdims=True)
        e = e.astype(BF16)
        pv = jnp.concatenate([_dot(e[h * lq:(h + 1) * lq], vs[h]) for h in range(nh)], axis=0)
        acc_ref[...] = alpha * acc_ref[...] + pv
        m_ref[...] = m_new

    k_all = pltpu.einshape("lhd->hld", kc_ref[...])
    v_all = pltpu.einshape("lhd->hld", vc_ref[...])
    s = jnp.concatenate([_dot_nt(qs[h], k_all[h].astype(BF16)) - cc_ref[0, h:h + 1, :]
                         for h in range(nh)], axis=0)
    update(s, [v_all[h].astype(BF16) for h in range(nh)])

    @pl.when(j == pl.num_programs(1) - 1)
    def _():
        keep = _causal(lq)
        kn = heads(kn_ref)
        s = jnp.concatenate(
            [jnp.where(keep, _dot_nt(qs[h], kn[h]) - cn_ref[0, h:h + 1, 0:lq], NEG_INF)
             for h in range(nh)], axis=0)
        update(s, heads(vn_ref))
        o = acc_ref[...] / l_ref[...]
        for h in range(nh):
            o_ref[:, h * hd:(h + 1) * hd] = o[h * lq:(h + 1) * lq].astype(o_ref.dtype)


def fox_sample_attention(q, k_cache, v_cache, layer, kb, vb, c, nb, nh):
    m, d = q.shape
    lq = m // nb
    past, hd = k_cache.shape[2], k_cache.shape[4]
    tk = min(FOX_TK, past)
    nkv = past // tk
    new = pl.BlockSpec((lq, d), lambda b, j: (b, 0))
    cache = pl.BlockSpec((None, None, tk, nh, hd), lambda b, j: (layer, b, j, 0, 0))
    return pl.pallas_call(
        functools.partial(_fox_sample_body, nh),
        grid=(nb, nkv),
        in_specs=[new, cache, cache, new, new,
                  pl.BlockSpec((1, nh, tk), lambda b, j: (b, 0, j)),
                  pl.BlockSpec((1, nh, LANES), lambda b, j: (b, 0, past // LANES))],
        out_specs=new,
        out_shape=jax.ShapeDtypeStruct((m, d), BF16),
        scratch_shapes=[pltpu.VMEM((nh * lq, 1), F32), pltpu.VMEM((nh * lq, 1), F32),
                        pltpu.VMEM((nh * lq, hd), F32)],
        compiler_params=_params("parallel", "arbitrary"),
        name="fox_sample_attention",
    )(q, k_cache, v_cache, kb, vb, c, c)


def _band_prompt_body(tq, q_ref, k_ref, v_ref, bias_ref, o_ref):
    length = q_ref.shape[0]
    span = bias_ref.shape[2]
    for i in range(length // tq):
        k0 = max((i + 1) * tq - span, 0)
        k1 = (i + 1) * tq
        q = q_ref[i * tq:(i + 1) * tq, :]
        s = _dot_nt(q, k_ref[k0:k1, :]) + bias_ref[0, :, span - (k1 - k0):span]
        o_ref[i * tq:(i + 1) * tq, :] = _softmax_pv([(s, v_ref[k0:k1, :])]).astype(o_ref.dtype)


def band_prompt_attention(q, kb, vb, bias, nb, nh):
    m, d = q.shape
    length = m // nb
    hd = d // nh
    tq = bias.shape[1]
    seq = pl.BlockSpec((length, hd), lambda b, h: (b, h))
    return pl.pallas_call(
        functools.partial(_band_prompt_body, tq),
        grid=(nb, nh),
        in_specs=[seq, seq, seq, pl.BlockSpec((1,) + bias.shape[1:], lambda b, h: (h, 0, 0))],
        out_specs=seq,
        out_shape=jax.ShapeDtypeStruct((m, d), BF16),
        compiler_params=_params("parallel", "parallel"),
        name="band_prompt_attention",
    )(q, kb, vb, bias)


def _band_sample_body(nh, q_ref, kc_ref, vc_ref, kn_ref, vn_ref, bias_ref, o_ref):
    lq, d = q_ref.shape
    hd = d // nh
    lc = kc_ref.shape[0]
    k_all = pltpu.einshape("lhd->hld", kc_ref[...])
    v_all = pltpu.einshape("lhd->hld", vc_ref[...])
    heads = lambda ref: [ref[:, h * hd:(h + 1) * hd] for h in range(nh)]
    qs, kn, vn = heads(q_ref), heads(kn_ref), heads(vn_ref)
    s_c = jnp.concatenate([_dot_nt(qs[h], k_all[h].astype(BF16)) + bias_ref[h, :, 0:lc]
                           for h in range(nh)], axis=0)
    s_n = jnp.concatenate([_dot_nt(qs[h], kn[h]) + bias_ref[h, :, lc:lc + lq] for h in range(nh)], axis=0)
    m = jnp.maximum(jnp.max(s_c, axis=-1, keepdims=True), jnp.max(s_n, axis=-1, keepdims=True))
    e_c, e_n = jnp.exp2(s_c - m), jnp.exp2(s_n - m)
    inv = 1.0 / (jnp.sum(e_c, axis=-1, keepdims=True) + jnp.sum(e_n, axis=-1, keepdims=True))
    e_c, e_n = e_c.astype(BF16), e_n.astype(BF16)
    for h in range(nh):
        rows = slice(h * lq, (h + 1) * lq)
        o = _dot(e_c[rows], v_all[h].astype(BF16)) + _dot(e_n[rows], vn[h])
        o_ref[:, h * hd:(h + 1) * hd] = (o * inv[rows]).astype(o_ref.dtype)


def band_sample_attention(q, k_cache, v_cache, layer, kb, vb, bias, nb, nh):
    m, d = q.shape
    lq = m // nb
    lc, hd = k_cache.shape[2], k_cache.shape[4]
    new = pl.BlockSpec((lq, d), lambda b: (b, 0))
    cache = pl.BlockSpec((None, None, lc, nh, hd), lambda b: (layer, b, 0, 0, 0))
    return pl.pallas_call(
        functools.partial(_band_sample_body, nh),
        grid=(nb,),
        in_specs=[new, cache, cache, new, new, pl.BlockSpec(bias.shape, lambda b: (0, 0, 0))],
        out_specs=new,
        out_shape=jax.ShapeDtypeStruct((m, d), BF16),
        compiler_params=_params("parallel"),
        name="band_sample_attention",
    )(q, k_cache, v_cache, kb, vb, bias)


def band_bias_tile(rel_bias, tq, window):
    nh = rel_bias.shape[0]
    span = window + tq
    n = span + tq
    rel = (span - 1) - np.arange(n)
    diag = rel_bias.astype(F32)[:, np.clip(rel, -REL_CLIP, REL_CLIP) + REL_CLIP]
    skew = jnp.broadcast_to(diag[:, None, :], (nh, tq, n)).reshape(nh, tq * n)
    skew = skew[:, :tq * (n - 1)].reshape(nh, tq, n - 1)[:, :, tq - 1:tq - 1 + span]
    qpos = np.arange(tq)[:, None]
    kpos = np.arange(span)[None, :] - window
    qc, kc = qpos // CHUNK, kpos // CHUNK
    visible = (kc <= qc) & (kc >= qc - BAND_PREV)
    return jnp.where(visible[None], skew, NEG_INF)


def kernel(x_prompt, x_sample, state_ssm_re, state_ssm_im, cache_fox_k, cache_fox_v, cache_fox_logf,
           cache_band_k, cache_band_v, ffn1_norm, ffn1_w_in, ffn1_w_out, mix_norm, ffn2_norm, ffn2_w_in,
           ffn2_w_out, ssm_a_re, ssm_a_im, ssm_log_step, ssm_b_re, ssm_b_im, ssm_c_re, ssm_c_im, ssm_d,
           ssm_w_glu, fox_w_in, fox_b_f, fox_w_out, band_w_in, band_rel_bias, band_w_out, final_norm):
    bp, lp, d = x_prompt.shape
    bs, ls, _ = x_sample.shape
    depth = ffn1_norm.shape[0]
    nh = fox_b_f.shape[1]
    hd = d // nh
    scale = hd ** -0.5 * LOG2E
    past = cache_fox_k.shape[2]
    window = cache_band_k.shape[2]
    heads = lambda t, nb, length: t.reshape(nb, length, nh, hd)

    xp = x_prompt.reshape(bp * lp, d)
    xs = x_sample.reshape(bs * ls, d)
    outs = {name: [] for name in (
        "p_ssm_re", "p_ssm_im", "p_fox_k", "p_fox_v", "p_fox_logf", "p_band_k", "p_band_v",
        "s_ssm_re", "s_ssm_im", "s_fox_k", "s_fox_v", "s_fox_logf", "s_band_k", "s_band_v")}

    w_glu = ssm_w_glu.astype(BF16)
    fox_in, fox_out = fox_w_in.astype(BF16), fox_w_out.astype(BF16)
    band_in, band_out = band_w_in.astype(BF16), band_w_out.astype(BF16)

    def ffn(xp, xs, g, w_in, w_out, layer):
        xs, w_gate, w_up, w_down = ffn_half_step_casting(xs, g, w_in, w_out, layer)
        return ffn_half_step(xp, g, w_gate, w_up, w_down), xs

    for i in range(depth):
        kind, j = i % 3, i // 3
        xp, xs = ffn(xp, xs, ffn1_norm[i], ffn1_w_in, ffn1_w_out, i)
        if kind == 0:
            mats = ssm_matrices(ssm_a_re[j], ssm_a_im[j], ssm_log_step[j], ssm_b_re[j], ssm_b_im[j],
                                ssm_c_re[j], ssm_c_im[j])
            g, p = ssm_a_re.shape[1:]
            nblk = d // LANES
            zeros = jnp.zeros((bp, g, p), F32)
            streams = ((xp, bp, lp, zeros, zeros, "p"), (xs, bs, ls, state_ssm_re[j], state_ssm_im[j], "s"))
            new_x = []
            for x, nb, length, s_re, s_im, tag in streams:
                h_tm = rmsnorm_time_major(x, mix_norm[i], nb, length)
                a_tm, sf = ssm_core(h_tm, mats, ssm_pack_state(s_re, s_im, nblk), ssm_d[j])
                f_re, f_im = ssm_unpack_state(sf, g, p)
                outs[tag + "_ssm_re"].append(f_re); outs[tag + "_ssm_im"].append(f_im)
                new_x.append(ssm_glu_out(x, a_tm, w_glu, j, nb, length))
            xp, xs = new_x
        elif kind == 1:
            wf_t = fox_w_in[j][:, 3 * d:].T.astype(BF16)
            qp, kp, vp, kbp, vbp, lfp = qkv_project(xp, mix_norm[i], fox_in, j, scale, wf_t, fox_b_f[j])
            qs, ks, vs, kbs, vbs, lfs = qkv_project(xs, mix_norm[i], fox_in, j, scale, wf_t, fox_b_f[j])
            lfp = lfp.reshape(nh, bp, lp).transpose(1, 0, 2)
            lfs = lfs.reshape(nh, bs, ls).transpose(1, 0, 2)
            cp = (cumsum_time(lfp) * LOG2E).reshape(bp * nh, 1, lp)
            lf_all = jnp.concatenate([cache_fox_logf[j].astype(F32).transpose(0, 2, 1), lfs,
                                      jnp.zeros((bs, nh, LANES - ls), F32)], axis=2)
            cs = cumsum_time(lf_all) * LOG2E
            op = fox_prompt_attention(qp, kbp, vbp, cp, bp, nh)
            os_ = fox_sample_attention(qs, cache_fox_k, cache_fox_v, j, kbs, vbs, cs, bs, nh)
            xp = out_project(xp, op, fox_out, j)
            xs = out_project(xs, os_, fox_out, j)
            outs["p_fox_k"].append(heads(kp, bp, lp)); outs["p_fox_v"].append(heads(vp, bp, lp))
            outs["p_fox_logf"].append(lfp.transpose(0, 2, 1))
            outs["s_fox_k"].append(heads(ks, bs, ls)); outs["s_fox_v"].append(heads(vs, bs, ls))
            outs["s_fox_logf"].append(lfs.transpose(0, 2, 1))
        else:
            qp, kp, vp, kbp, vbp = qkv_project(xp, mix_norm[i], band_in, j, scale)
            qs, ks, vs, kbs, vbs = qkv_project(xs, mix_norm[i], band_in, j, scale)
            bw = BAND_PREV * CHUNK
            bias = band_bias_tile(band_rel_bias[j], BAND_TQ, bw) * LOG2E
            op = band_prompt_attention(qp, kbp, vbp, bias, bp, nh)
            os_ = band_sample_attention(qs, cache_band_k, cache_band_v, j, kbs, vbs,
                                        bias[:, :ls, bw - window:bw + ls], bs, nh)
            xp = out_project(xp, op, band_out, j)
            xs = out_project(xs, os_, band_out, j)
            keep = min(bw, lp)
            outs["p_band_k"].append(heads(kp, bp, lp)[:, lp - keep:])
            outs["p_band_v"].append(heads(vp, bp, lp)[:, lp - keep:])
            outs["s_band_k"].append(heads(ks, bs, ls)); outs["s_band_v"].append(heads(vs, bs, ls))
        xp, xs = ffn(xp, xs, ffn2_norm[i], ffn2_w_in, ffn2_w_out, i)

    y_prompt = rmsnorm(xp, final_norm).reshape(bp, lp, d)
    y_sample = rmsnorm(xs, final_norm).reshape(bs, ls, d)
    st = {k: jnp.stack(v) for k, v in outs.items()}
    return (y_prompt, y_sample, st["p_ssm_re"], st["p_ssm_im"], st["p_fox_k"], st["p_fox_v"],
            st["p_fox_logf"], st["p_band_k"], st["p_band_v"], st["s_ssm_re"], st["s_ssm_im"],
            st["s_fox_k"], st["s_fox_v"], st["s_fox_logf"], st["s_band_k"], st["s_band_v"])
```

```python
import functools

import jax
import jax.numpy as jnp
import numpy as np
from jax import lax
from jax.experimental import pallas as pl
from jax.experimental.pallas import tpu as pltpu

F32 = jnp.float32
BF16 = jnp.bfloat16

EPS = 1e-6
NEG_INF = -1e30
LOG2E = 1.4426950408889634
CHUNK = 64
BAND_PREV = 8
REL_CLIP = 256
SSM_GROUP = 16
SSM_T = 8
LANES = 128
VMEM_LIMIT = 56 * 1024 * 1024

TM = 512
TM_FFN = 1024
TF = 512
TF_CAST = 256
TM_QKV = 1024
TN = 512
TN_GLU = 1024
BAND_TQ = 256
FOX_TQ = 256
FOX_TK = 512
SSM_ROWS = 512


def _params(*sem):
    return pltpu.CompilerParams(dimension_semantics=sem, vmem_limit_bytes=VMEM_LIMIT)


def _rms(x, g):
    return x * lax.rsqrt(jnp.mean(x * x, axis=-1, keepdims=True) + EPS) * g


def _dot(a, b):
    return jnp.dot(a, b, preferred_element_type=F32)


def _dot_nt(a, b):
    return lax.dot_general(a, b, (((1,), (1,)), ((), ())), preferred_element_type=F32)


def _ffn_body(x_ref, g_ref, wg_ref, wu_ref, wo_ref, o_ref, h_ref):
    f = pl.program_id(1)

    @pl.when(f == 0)
    def _():
        h_ref[...] = _rms(x_ref[...], g_ref[...]).astype(BF16)
        o_ref[...] = jnp.zeros_like(o_ref)

    h = h_ref[...]
    gate = _dot(h, wg_ref[...])
    up = _dot(h, wu_ref[...])
    act = (gate * jax.nn.sigmoid(gate) * up).astype(BF16)
    o_ref[...] += _dot(act, wo_ref[...])

    @pl.when(f == pl.num_programs(1) - 1)
    def _():
        o_ref[...] = x_ref[...] + 0.5 * o_ref[...]


def ffn_half_step(x, g, w_gate, w_up, w_out):
    m, d = x.shape
    f = w_out.shape[0]
    tm = min(TM_FFN, m)
    tf = min(TF, f)
    return pl.pallas_call(
        _ffn_body,
        grid=(m // tm, f // tf),
        in_specs=[
            pl.BlockSpec((tm, d), lambda i, j: (i, 0)),
            pl.BlockSpec((1, d), lambda i, j: (0, 0)),
            pl.BlockSpec((d, tf), lambda i, j: (0, j)),
            pl.BlockSpec((d, tf), lambda i, j: (0, j)),
            pl.BlockSpec((tf, d), lambda i, j: (j, 0)),
        ],
        out_specs=pl.BlockSpec((tm, d), lambda i, j: (i, 0)),
        out_shape=jax.ShapeDtypeStruct((m, d), F32),
        scratch_shapes=[pltpu.VMEM((tm, d), BF16)],
        compiler_params=_params("parallel", "arbitrary"),
        name="ffn_half_step",
    )(x, g.reshape(1, d), w_gate, w_up, w_out)


def _ffn_cast_body(x_ref, g_ref, wg32_ref, wu32_ref, wo32_ref, o_ref, wg_ref, wu_ref, wo_ref, h_ref):
    wg_ref[...] = wg32_ref[...].astype(BF16)
    wu_ref[...] = wu32_ref[...].astype(BF16)
    wo_ref[...] = wo32_ref[...].astype(BF16)
    _ffn_body(x_ref, g_ref, wg_ref, wu_ref, wo_ref, o_ref, h_ref)


def ffn_half_step_casting(x, g, w_in, w_out, layer):
    m, d = x.shape
    f = w_out.shape[1]
    tf = min(TF_CAST, f)
    nf = f // tf
    once = dict(pipeline_mode=pl.Buffered(1))
    col = pl.BlockSpec((d, tf), lambda i, j: (0, j))
    return pl.pallas_call(
        _ffn_cast_body,
        grid=(1, nf),
        in_specs=[
            pl.BlockSpec((m, d), lambda i, j: (0, 0), **once),
            pl.BlockSpec((1, d), lambda i, j: (0, 0)),
            pl.BlockSpec((None, d, tf), lambda i, j: (layer, 0, j)),
            pl.BlockSpec((None, d, tf), lambda i, j: (layer, 0, j + nf)),
            pl.BlockSpec((None, tf, d), lambda i, j: (layer, j, 0)),
        ],
        out_specs=[pl.BlockSpec((m, d), lambda i, j: (0, 0)), col, col,
                   pl.BlockSpec((tf, d), lambda i, j: (j, 0))],
        out_shape=[jax.ShapeDtypeStruct((m, d), F32), jax.ShapeDtypeStruct((d, f), BF16),
                   jax.ShapeDtypeStruct((d, f), BF16), jax.ShapeDtypeStruct((f, d), BF16)],
        scratch_shapes=[pltpu.VMEM((m, d), BF16)],
        compiler_params=_params("arbitrary", "arbitrary"),
        name="ffn_half_step_casting",
    )(x, g.reshape(1, d), w_in, w_in, w_out)


def _rmsnorm_body(x_ref, g_ref, o_ref):
    o_ref[...] = _rms(x_ref[...], g_ref[...]).astype(o_ref.dtype)


def rmsnorm(x, g):
    m, d = x.shape
    tm = min(TM, m)
    return pl.pallas_call(
        _rmsnorm_body,
        grid=(m // tm,),
        in_specs=[pl.BlockSpec((tm, d), lambda i: (i, 0)),
                  pl.BlockSpec((1, d), lambda i: (0, 0))],
        out_specs=pl.BlockSpec((tm, d), lambda i: (i, 0)),
        out_shape=jax.ShapeDtypeStruct((m, d), F32),
        compiler_params=_params("parallel"),
        name="rmsnorm",
    )(x, g.reshape(1, d))


def _rmsnorm_tm_body(x_ref, g_ref, o_ref):
    h = _rms(x_ref[...], g_ref[...])
    o_ref[...] = pltpu.einshape("bld->lbd", h).reshape(o_ref.shape)


def rmsnorm_time_major(x, g, nb, length):
    d = x.shape[1]
    t = SSM_T
    tl = min(TM // nb, length)
    return pl.pallas_call(
        _rmsnorm_tm_body,
        grid=(length // tl,),
        in_specs=[pl.BlockSpec((nb, tl, d), lambda l: (0, l, 0)),
                  pl.BlockSpec((1, d), lambda l: (0, 0))],
        out_specs=pl.BlockSpec((tl // t, t, nb, d), lambda l: (l, 0, 0, 0)),
        out_shape=jax.ShapeDtypeStruct((length // t, t, nb, d), F32),
        compiler_params=_params("parallel"),
        name="rmsnorm_time_major",
    )(x.reshape(nb, length, d), g.reshape(1, d))


def _qkv_body(scale, has_forget, *refs):
    if has_forget:
        (x_ref, g_ref, wq_ref, wk_ref, wv_ref, wf_ref, bf_ref,
         q_ref, k32_ref, v32_ref, kb_ref, vb_ref, lf_ref, h_ref) = refs
    else:
        (x_ref, g_ref, wq_ref, wk_ref, wv_ref,
         q_ref, k32_ref, v32_ref, kb_ref, vb_ref, h_ref) = refs

    @pl.when(pl.program_id(1) == 0)
    def _():
        h = _rms(x_ref[...], g_ref[...]).astype(BF16)
        h_ref[...] = h
        if has_forget:
            z = _dot_nt(wf_ref[...], h) + bf_ref[...]
            lf_ref[...] = jnp.minimum(z, 0.0) - jnp.log(1.0 + jnp.exp(-jnp.abs(z)))

    h = h_ref[...]
    q_ref[...] = (_dot(h, wq_ref[...]) * scale).astype(BF16)
    k = _dot(h, wk_ref[...])
    k32_ref[...] = k
    kb_ref[...] = k.astype(BF16)
    v = _dot(h, wv_ref[...])
    v32_ref[...] = v
    vb_ref[...] = v.astype(BF16)


def qkv_project(x, g, w, layer, scale, wf_t=None, b_f=None):
    m, d = x.shape
    tm = min(TM_QKV, m)
    tn = min(TN, d)
    nq = d // tn
    has_forget = wf_t is not None
    in_specs = [
        pl.BlockSpec((tm, d), lambda i, n: (i, 0)),
        pl.BlockSpec((1, d), lambda i, n: (0, 0)),
        pl.BlockSpec((None, d, tn), lambda i, n: (layer, 0, n)),
        pl.BlockSpec((None, d, tn), lambda i, n: (layer, 0, n + nq)),
        pl.BlockSpec((None, d, tn), lambda i, n: (layer, 0, n + 2 * nq)),
    ]
    args = [x, g.reshape(1, d), w, w, w]
    tile = pl.BlockSpec((tm, tn), lambda i, n: (i, n))
    out_specs = [tile] * 5
    out_shape = [jax.ShapeDtypeStruct((m, d), BF16),
                 jax.ShapeDtypeStruct((m, d), F32), jax.ShapeDtypeStruct((m, d), F32),
                 jax.ShapeDtypeStruct((m, d), BF16), jax.ShapeDtypeStruct((m, d), BF16)]
    if has_forget:
        nh = wf_t.shape[0]
        in_specs += [pl.BlockSpec((nh, d), lambda i, n: (0, 0)),
                     pl.BlockSpec((nh, 1), lambda i, n: (0, 0))]
        args += [wf_t, b_f.reshape(nh, 1)]
        out_specs = out_specs + [pl.BlockSpec((nh, tm), lambda i, n: (0, i))]
        out_shape = out_shape + [jax.ShapeDtypeStruct((nh, m), F32)]
    return pl.pallas_call(
        functools.partial(_qkv_body, scale, has_forget),
        grid=(m // tm, nq),
        in_specs=in_specs,
        out_specs=out_specs,
        out_shape=out_shape,
        scratch_shapes=[pltpu.VMEM((tm, d), BF16)],
        compiler_params=_params("parallel", "arbitrary"),
        name="qkv_project",
    )(*args)


def _out_proj_body(x_ref, o_ref, w_ref, y_ref):
    y_ref[...] = x_ref[...] + _dot(o_ref[...], w_ref[...])


def out_project(x, o, w, layer):
    m, d = x.shape
    tm = min(TM, m)
    return pl.pallas_call(
        _out_proj_body,
        grid=(m // tm,),
        in_specs=[pl.BlockSpec((tm, d), lambda i: (i, 0)),
                  pl.BlockSpec((tm, d), lambda i: (i, 0)),
                  pl.BlockSpec((None, d, d), lambda i: (layer, 0, 0))],
        out_specs=pl.BlockSpec((tm, d), lambda i: (i, 0)),
        out_shape=jax.ShapeDtypeStruct((m, d), F32),
        compiler_params=_params("parallel"),
        name="out_project",
    )(x, o, w)


def _glu_body(a_ref, xt_ref, wv_ref, wg_ref, o_ref):
    nb, tl, tn = xt_ref.shape
    a = a_ref[...].reshape(tl * nb, a_ref.shape[3]).astype(BF16)
    val = _dot(a, wv_ref[...])
    gate = _dot(a, wg_ref[...])
    r = (val * jax.nn.sigmoid(gate)).reshape(tl, nb, tn)
    o_ref[...] = xt_ref[...] + pltpu.einshape("lbn->bln", r)


def ssm_glu_out(x, a4, w_glu, layer, nb, length):
    d = x.shape[1]
    t = SSM_T
    tl = min(TM // nb, length)
    tn = min(TN_GLU, d)
    nq = d // tn
    tile = pl.BlockSpec((nb, tl, tn), lambda n, l: (0, l, n))
    return pl.pallas_call(
        _glu_body,
        grid=(nq, length // tl),
        in_specs=[
            pl.BlockSpec((tl // t, t, nb, d), lambda n, l: (l, 0, 0, 0)),
            tile,
            pl.BlockSpec((None, d, tn), lambda n, l: (layer, 0, n)),
            pl.BlockSpec((None, d, tn), lambda n, l: (layer, 0, n + nq)),
        ],
        out_specs=tile,
        out_shape=jax.ShapeDtypeStruct((nb, length, d), F32),
        compiler_params=_params("parallel", "parallel"),
        name="ssm_glu_out",
    )(a4, x.reshape(nb, length, d), w_glu, w_glu).reshape(nb * length, d)


def _ssm_lag_body(ca_ref, bb_ref, k_ref):
    for i in range(ca_ref.shape[0]):
        k_ref[i] = jnp.dot(ca_ref[i], bb_ref[i], preferred_element_type=F32,
                           precision=lax.Precision.HIGHEST)


def ssm_lag_kernels(ca, bb, gpb):
    g, r, p2 = ca.shape
    c = bb.shape[2]
    return pl.pallas_call(
        _ssm_lag_body,
        grid=(g // gpb,),
        in_specs=[pl.BlockSpec((gpb, r, p2), lambda i: (i, 0, 0)),
                  pl.BlockSpec((gpb, p2, c), lambda i: (i, 0, 0))],
        out_specs=pl.BlockSpec((gpb, r, c), lambda i: (i, 0, 0)),
        out_shape=jax.ShapeDtypeStruct((g, r, c), F32),
        compiler_params=_params("parallel"),
        name="ssm_lag_kernels",
    )(ca, bb)


def _ssm_core_body(nb, x_ref, bdv_ref, ws_ref, wc_ref, at_ref, s0_ref, dsk_ref, y_ref, sf_ref,
                   ds_ref, dsw_ref, sst_ref, st_ref):
    nchp, t = x_ref.shape[0], x_ref.shape[1]
    m = nchp * nb
    ncol = at_ref.shape[2]
    half = LANES // 2

    def swap_halves(v):
        lane = lax.broadcasted_iota(jnp.int32, v.shape, 1)
        return jnp.where(lane % LANES < half, pltpu.roll(v, ncol - half, 1), pltpu.roll(v, half, 1))

    @pl.when(pl.program_id(1) == 0)
    def _():
        st_ref[...] = s0_ref[0]

    xcat = jnp.concatenate(
        [x_ref[:, tau].reshape(m, LANES).astype(BF16) for tau in range(t)], axis=1)
    ds = _dot(xcat, ws_ref[0])
    ds_ref[...] = ds
    dsw_ref[...] = swap_halves(ds)
    a1 = jnp.broadcast_to(at_ref[0, 0:1, :], (nb, ncol))
    a2 = jnp.broadcast_to(at_ref[0, 1:2, :], (nb, ncol))

    def step(k, carry):
        s, sw = carry
        rows = pl.ds(pl.multiple_of(k * nb, nb), nb)
        sst_ref[rows, :] = s
        ns = a1 * s + a2 * sw + ds_ref[rows, :]
        nsw = a1 * sw - a2 * s + dsw_ref[rows, :]
        return ns, nsw

    s0 = st_ref[...]
    s, _ = lax.fori_loop(0, nchp, step, (s0, swap_halves(s0)))
    st_ref[...] = s
    sf_ref[0] = s
    y_in = _dot(sst_ref[...].astype(BF16), wc_ref[0])
    dsk = dsk_ref[...]
    for tt in range(t):
        yt = y_in[:, tt * LANES:(tt + 1) * LANES] + _dot(
            xcat[:, :(tt + 1) * LANES], bdv_ref[0, (t - 1 - tt) * LANES:, :])
        yt = jax.nn.gelu(yt + dsk * x_ref[:, tt].reshape(m, LANES))
        y_ref[:, tt] = yt.reshape(nchp, nb, LANES)


def ssm_core(h4, mats, layer, s0, d_skip):
    bdv, ws, wc, at = mats
    nch, t, nb, d = h4.shape
    nblk = d // LANES
    nchp = max(min(SSM_ROWS // nb, nch), 1)
    m = nchp * nb
    ncol = ws.shape[2]
    blk0 = layer * nblk
    wblk = lambda *s: pl.BlockSpec((1,) + s, lambda c, k: (c + blk0, 0, 0))
    sblk = pl.BlockSpec((1, nb, ncol), lambda c, k: (c, 0, 0))
    seq = pl.BlockSpec((nchp, t, nb, LANES), lambda c, k: (k, 0, 0, c))
    y, sf = pl.pallas_call(
        functools.partial(_ssm_core_body, nb),
        grid=(nblk, nch // nchp),
        in_specs=[seq, wblk(t * LANES, LANES), wblk(t * LANES, ncol), wblk(ncol, t * LANES),
                  wblk(2, ncol), sblk, pl.BlockSpec((1, LANES), lambda c, k: (0, c))],
        out_specs=[seq, sblk],
        out_shape=[jax.ShapeDtypeStruct((nch, t, nb, d), F32),
                   jax.ShapeDtypeStruct((nblk, nb, ncol), F32)],
        scratch_shapes=[pltpu.VMEM((m, ncol), F32), pltpu.VMEM((m, ncol), F32), pltpu.VMEM((m, ncol), F32),
                        pltpu.VMEM((nb, ncol), F32)],
        compiler_params=_params("parallel", "arbitrary"),
        name="ssm_core",
    )(h4, bdv, ws, wc, at, s0, d_skip.reshape(1, d))
    return y, sf


def ssm_matrices(a_re, a_im, log_step, b_re, b_im, c_re, c_im):
    t = SSM_T
    g, p = a_re.shape
    c = b_re.shape[2]
    gpb = LANES // c
    nblk = g // gpb
    dt = jnp.exp(log_step)[:, None]
    lam_re, lam_im = a_re * dt, a_im * dt
    j = jnp.arange(t + 1, dtype=F32)[:, None, None]
    mag = jnp.exp(lam_re * j)
    pw_re, pw_im = mag * jnp.cos(lam_im * j), mag * jnp.sin(lam_im * j)
    ab_re, ab_im = pw_re[1], pw_im[1]
    den = a_re * a_re + a_im * a_im
    z_re = ((ab_re - 1.0) * a_re + ab_im * a_im) / den
    z_im = (ab_im * a_re - (ab_re - 1.0) * a_im) / den
    bb_re = z_re[..., None] * b_re - z_im[..., None] * b_im
    bb_im = z_re[..., None] * b_im + z_im[..., None] * b_re
    jr = (t - 1) - jnp.arange(t, dtype=F32)[:, None, None]
    mag_r = jnp.exp(lam_re * jr)
    rp_re, rp_im = mag_r * jnp.cos(lam_im * jr), mag_r * jnp.sin(lam_im * jr)
    ws_re = rp_re[:, :, :, None] * bb_re[None] - rp_im[:, :, :, None] * bb_im[None]
    ws_im = rp_re[:, :, :, None] * bb_im[None] + rp_im[:, :, :, None] * bb_re[None]
    wsc = jnp.concatenate([ws_re, ws_im], axis=2).transpose(1, 0, 3, 2).reshape(nblk, gpb, t * c, 2 * p)

    def out_weights(q_re, q_im):
        cr, ci = c_re[None], c_im[None]
        qr, qi = q_re[:, :, None, :], q_im[:, :, None, :]
        return cr * qr - ci * qi, -cr * qi - ci * qr

    wc_re, wc_im = out_weights(pw_re[1:], pw_im[1:])
    wcc = jnp.concatenate([wc_re, wc_im], axis=3).transpose(1, 3, 0, 2).reshape(nblk, gpb, 2 * p, t * c)

    ca_re, ca_im = out_weights(rp_re, rp_im)
    ca = jnp.concatenate([ca_re, ca_im], axis=-1).transpose(1, 0, 2, 3).reshape(g, t * c, 2 * p)
    bb = jnp.concatenate([bb_re, bb_im], axis=1)
    kmat = ssm_lag_kernels(ca, bb, gpb).reshape(g, t, c, c)
    kc = kmat.transpose(0, 1, 3, 2).reshape(nblk, gpb, t * c, c)

    place = np.zeros((gpb, t, gpb, c, t, c), np.float32)
    for h in range(gpb):
        place[h, :, h] = np.eye(t * c, dtype=np.float32).reshape(t, c, t, c)
    place = jnp.asarray(place.reshape(gpb, t * LANES, t * c), BF16)
    expand = functools.partial(jnp.einsum, preferred_element_type=F32)
    ws = expand("hrk,bhkn->brhn", place, wsc.astype(BF16)).reshape(nblk, t * LANES, gpb * 2 * p)
    wc = expand("bhnk,hrk->bhnr", wcc.astype(BF16), place).reshape(nblk, gpb * 2 * p, t * LANES)
    bdv = expand("hrk,bhkc->brhc", place, kc.astype(BF16)).reshape(nblk, t * LANES, LANES)

    at_re = pw_re[t].reshape(nblk, gpb, p)
    at_im = pw_im[t].reshape(nblk, gpb, p)
    a1 = jnp.concatenate([at_re, at_re], axis=2).reshape(nblk, gpb * 2 * p)
    a2 = jnp.concatenate([-at_im, at_im], axis=2).reshape(nblk, gpb * 2 * p)
    return bdv.astype(BF16), ws.astype(BF16), wc.astype(BF16), jnp.stack([a1, a2], axis=1)


def ssm_pack_state(s_re, s_im, nblk):
    nb, g, p = s_re.shape
    s = jnp.stack([s_re.reshape(nb, nblk, g // nblk, p), s_im.reshape(nb, nblk, g // nblk, p)], axis=3)
    return s.transpose(1, 0, 2, 3, 4).reshape(nblk, nb, -1)


def ssm_unpack_state(sf, g, p):
    nblk, nb, _ = sf.shape
    s = sf.reshape(nblk, nb, g // nblk, 2, p).transpose(1, 0, 2, 3, 4).reshape(nb, g, 2, p)
    return s[:, :, 0], s[:, :, 1]


def _split3(x):
    hi = x.astype(BF16)
    r = x - hi.astype(F32)
    mid = r.astype(BF16)
    lo = (r - mid.astype(F32)).astype(BF16)
    return hi, mid, lo


def _cumsum_body(x_ref, o_ref):
    nblk = x_ref.shape[2] // LANES
    rows = x_ref.shape[1]
    row = lax.broadcasted_iota(jnp.int32, (LANES, LANES), 0)
    col = lax.broadcasted_iota(jnp.int32, (LANES, LANES), 1)
    tri = jnp.where(row <= col, 1.0, 0.0).astype(BF16)
    carry = jnp.zeros((rows, 1), F32)
    for j in range(nblk):
        sl = slice(j * LANES, (j + 1) * LANES)
        hi, mid, lo = _split3(x_ref[0, :, sl])
        w = _dot(hi, tri) + _dot(mid, tri) + _dot(lo, tri)
        o_ref[0, :, sl] = w + carry
        carry = carry + jnp.sum(x_ref[0, :, sl], axis=1, keepdims=True)


def cumsum_time(x):
    n, h, length = x.shape
    return pl.pallas_call(
        _cumsum_body,
        grid=(n,),
        in_specs=[pl.BlockSpec((1, h, length), lambda i: (i, 0, 0))],
        out_specs=pl.BlockSpec((1, h, length), lambda i: (i, 0, 0)),
        out_shape=jax.ShapeDtypeStruct((n, h, length), F32),
        compiler_params=_params("parallel"),
        name="cumsum_time",
    )(x)


def _softmax_pv(parts):
    m = functools.reduce(jnp.maximum, [jnp.max(s, axis=-1, keepdims=True) for s, _ in parts])
    acc, den = None, None
    for s, v in parts:
        e = jnp.exp2(s - m)
        l = jnp.sum(e, axis=-1, keepdims=True)
        o = _dot(e.astype(BF16), v)
        acc = o if acc is None else acc + o
        den = l if den is None else den + l
    return acc / den


def _causal(n):
    rpos = lax.broadcasted_iota(jnp.int32, (n, n), 0)
    cpos = lax.broadcasted_iota(jnp.int32, (n, n), 1)
    return cpos <= rpos


def _fox_prompt_body(tq, q_ref, k_ref, v_ref, c_ref, o_ref):
    length = q_ref.shape[0]
    keep = _causal(tq)
    for i in range(length // tq):
        lo, hi = i * tq, (i + 1) * tq
        q = q_ref[lo:hi, :]
        s = _dot_nt(q, k_ref[0:hi, :]) - c_ref[0, :, 0:hi]
        s_diag = jnp.where(keep, s[:, lo:hi], NEG_INF)
        m = jnp.max(s_diag, axis=-1, keepdims=True)
        if i > 0:
            m = jnp.maximum(m, jnp.max(s[:, 0:lo], axis=-1, keepdims=True))
            e = jnp.concatenate([jnp.exp2(s[:, 0:lo] - m), jnp.exp2(s_diag - m)], axis=1)
        else:
            e = jnp.exp2(s_diag - m)
        den = jnp.sum(e, axis=-1, keepdims=True)
        o_ref[lo:hi, :] = (_dot(e.astype(BF16), v_ref[0:hi, :]) / den).astype(o_ref.dtype)


def fox_prompt_attention(q, kb, vb, c, nb, nh):
    m, d = q.shape
    length = m // nb
    hd = d // nh
    tq = min(FOX_TQ, length)
    seq = pl.BlockSpec((length, hd), lambda b, h: (b, h))
    return pl.pallas_call(
        functools.partial(_fox_prompt_body, tq),
        grid=(nb, nh),
        in_specs=[seq, seq, seq, pl.BlockSpec((1, 1, length), lambda b, h: (b * nh + h, 0, 0))],
        out_specs=seq,
        out_shape=jax.ShapeDtypeStruct((m, d), BF16),
        compiler_params=_params("parallel", "parallel"),
        name="fox_prompt_attention",
    )(q, kb, vb, c)


def _fox_sample_body(nh, q_ref, kc_ref, vc_ref, kn_ref, vn_ref, cc_ref, cn_ref, o_ref,
                     m_ref, l_ref, acc_ref):
    j = pl.program_id(1)
    lq, d = q_ref.shape
    hd = d // nh

    @pl.when(j == 0)
    def _():
        m_ref[...] = jnp.full_like(m_ref, NEG_INF)
        l_ref[...] = jnp.zeros_like(l_ref)
        acc_ref[...] = jnp.zeros_like(acc_ref)

    heads = lambda ref: [ref[:, h * hd:(h + 1) * hd] for h in range(nh)]
    qs = heads(q_ref)

    def update(s, vs):
        m_old = m_ref[...]
        m_new = jnp.maximum(m_old, jnp.max(s, axis=-1, keepdims=True))
        alpha = jnp.exp2(m_old - m_new)
        e = jnp.exp2(s - m_new)
        l_ref[...] = alpha * l_ref[...] + jnp.sum(e, axis=-1, keepdims=True)
        e = e.astype(BF16)
        pv = jnp.concatenate([_dot(e[h * lq:(h + 1) * lq], vs[h]) for h in range(nh)], axis=0)
        acc_ref[...] = alpha * acc_ref[...] + pv
        m_ref[...] = m_new

    k_all = pltpu.einshape("lhd->hld", kc_ref[...])
    v_all = pltpu.einshape("lhd->hld", vc_ref[...])
    s = jnp.concatenate([_dot_nt(qs[h], k_all[h].astype(BF16)) - cc_ref[0, h:h + 1, :]
                         for h in range(nh)], axis=0)
    update(s, [v_all[h].astype(BF16) for h in range(nh)])

    @pl.when(j == pl.num_programs(1) - 1)
    def _():
        keep = _causal(lq)
        kn = heads(kn_ref)
        s = jnp.concatenate(
            [jnp.where(keep, _dot_nt(qs[h], kn[h]) - cn_ref[0, h:h + 1, 0:lq], NEG_INF)
             for h in range(nh)], axis=0)
        update(s, heads(vn_ref))
        o = acc_ref[...] / l_ref[...]
        for h in range(nh):
            o_ref[:, h * hd:(h + 1) * hd] = o[h * lq:(h + 1) * lq].astype(o_ref.dtype)


def fox_sample_attention(q, k_cache, v_cache, layer, kb, vb, c, nb, nh):
    m, d = q.shape
    lq = m // nb
    past, hd = k_cache.shape[2], k_cache.shape[4]
    tk = min(FOX_TK, past)
    nkv = past // tk
    new = pl.BlockSpec((lq, d), lambda b, j: (b, 0))
    cache = pl.BlockSpec((None, None, tk, nh, hd), lambda b, j: (layer, b, j, 0, 0))
    return pl.pallas_call(
        functools.partial(_fox_sample_body, nh),
        grid=(nb, nkv),
        in_specs=[new, cache, cache, new, new,
                  pl.BlockSpec((1, nh, tk), lambda b, j: (b, 0, j)),
                  pl.BlockSpec((1, nh, LANES), lambda b, j: (b, 0, past // LANES))],
        out_specs=new,
        out_shape=jax.ShapeDtypeStruct((m, d), BF16),
        scratch_shapes=[pltpu.VMEM((nh * lq, 1), F32), pltpu.VMEM((nh * lq, 1), F32),
                        pltpu.VMEM((nh * lq, hd), F32)],
        compiler_params=_params("parallel", "arbitrary"),
        name="fox_sample_attention",
    )(q, k_cache, v_cache, kb, vb, c, c)


def _band_prompt_body(tq, q_ref, k_ref, v_ref, bias_ref, o_ref):
    length = q_ref.shape[0]
    span = bias_ref.shape[2]
    for i in range(length // tq):
        k0 = max((i + 1) * tq - span, 0)
        k1 = (i + 1) * tq
        q = q_ref[i * tq:(i + 1) * tq, :]
        s = _dot_nt(q, k_ref[k0:k1, :]) + bias_ref[0, :, span - (k1 - k0):span]
        o_ref[i * tq:(i + 1) * tq, :] = _softmax_pv([(s, v_ref[k0:k1, :])]).astype(o_ref.dtype)


def band_prompt_attention(q, kb, vb, bias, nb, nh):
    m, d = q.shape
    length = m // nb
    hd = d // nh
    tq = bias.shape[1]
    seq = pl.BlockSpec((length, hd), lambda b, h: (b, h))
    return pl.pallas_call(
        functools.partial(_band_prompt_body, tq),
        grid=(nb, nh),
        in_specs=[seq, seq, seq, pl.BlockSpec((1,) + bias.shape[1:], lambda b, h: (h, 0, 0))],
        out_specs=seq,
        out_shape=jax.ShapeDtypeStruct((m, d), BF16),
        compiler_params=_params("parallel", "parallel"),
        name="band_prompt_attention",
    )(q, kb, vb, bias)


def _band_sample_body(nh, q_ref, kc_ref, vc_ref, kn_ref, vn_ref, bias_ref, o_ref):
    lq, d = q_ref.shape
    hd = d // nh
    lc = kc_ref.shape[0]
    k_all = pltpu.einshape("lhd->hld", kc_ref[...])
    v_all = pltpu.einshape("lhd->hld", vc_ref[...])
    heads = lambda ref: [ref[:, h * hd:(h + 1) * hd] for h in range(nh)]
    qs, kn, vn = heads(q_ref), heads(kn_ref), heads(vn_ref)
    s_c = jnp.concatenate([_dot_nt(qs[h], k_all[h].astype(BF16)) + bias_ref[h, :, 0:lc]
                           for h in range(nh)], axis=0)
    s_n = jnp.concatenate([_dot_nt(qs[h], kn[h]) + bias_ref[h, :, lc:lc + lq] for h in range(nh)], axis=0)
    m = jnp.maximum(jnp.max(s_c, axis=-1, keepdims=True), jnp.max(s_n, axis=-1, keepdims=True))
    e_c, e_n = jnp.exp2(s_c - m), jnp.exp2(s_n - m)
    inv = 1.0 / (jnp.sum(e_c, axis=-1, keepdims=True) + jnp.sum(e_n, axis=-1, keepdims=True))
    e_c, e_n = e_c.astype(BF16), e_n.astype(BF16)
    for h in range(nh):
        rows = slice(h * lq, (h + 1) * lq)
        o = _dot(e_c[rows], v_all[h].astype(BF16)) + _dot(e_n[rows], vn[h])
        o_ref[:, h * hd:(h + 1) * hd] = (o * inv[rows]).astype(o_ref.dtype)


def band_sample_attention(q, k_cache, v_cache, layer, kb, vb, bias, nb, nh):
    m, d = q.shape
    lq = m // nb
    lc, hd = k_cache.shape[2], k_cache.shape[4]
    new = pl.BlockSpec((lq, d), lambda b: (b, 0))
    cache = pl.BlockSpec((None, None, lc, nh, hd), lambda b: (layer, b, 0, 0, 0))
    return pl.pallas_call(
        functools.partial(_band_sample_body, nh),
        grid=(nb,),
        in_specs=[new, cache, cache, new, new, pl.BlockSpec(bias.shape, lambda b: (0, 0, 0))],
        out_specs=new,
        out_shape=jax.ShapeDtypeStruct((m, d), BF16),
        compiler_params=_params("parallel"),
        name="band_sample_attention",
    )(q, k_cache, v_cache, kb, vb, bias)


def band_bias_tile(rel_bias, tq, window):
    nh = rel_bias.shape[0]
    span = window + tq
    n = span + tq
    rel = (span - 1) - np.arange(n)
    diag = rel_bias.astype(F32)[:, np.clip(rel, -REL_CLIP, REL_CLIP) + REL_CLIP]
    skew = jnp.broadcast_to(diag[:, None, :], (nh, tq, n)).reshape(nh, tq * n)
    skew = skew[:, :tq * (n - 1)].reshape(nh, tq, n - 1)[:, :, tq - 1:tq - 1 + span]
    qpos = np.arange(tq)[:, None]
    kpos = np.arange(span)[None, :] - window
    qc, kc = qpos // CHUNK, kpos // CHUNK
    visible = (kc <= qc) & (kc >= qc - BAND_PREV)
    return jnp.where(visible[None], skew, NEG_INF)


def kernel(x_prompt, x_sample, state_ssm_re, state_ssm_im, cache_fox_k, cache_fox_v, cache_fox_logf,
           cache_band_k, cache_band_v, ffn1_norm, ffn1_w_in, ffn1_w_out, mix_norm, ffn2_norm, ffn2_w_in,
           ffn2_w_out, ssm_a_re, ssm_a_im, ssm_log_step, ssm_b_re, ssm_b_im, ssm_c_re, ssm_c_im, ssm_d,
           ssm_w_glu, fox_w_in, fox_b_f, fox_w_out, band_w_in, band_rel_bias, band_w_out, final_norm):
    bp, lp, d = x_prompt.shape
    bs, ls, _ = x_sample.shape
    depth = ffn1_norm.shape[0]
    nh = fox_b_f.shape[1]
    hd = d // nh
    scale = hd ** -0.5 * LOG2E
    past = cache_fox_k.shape[2]
    window = cache_band_k.shape[2]
    heads = lambda t, nb, length: t.reshape(nb, length, nh, hd)

    xp = x_prompt.reshape(bp * lp, d)
    xs = x_sample.reshape(bs * ls, d)
    outs = {name: [] for name in (
        "p_ssm_re", "p_ssm_im", "p_fox_k", "p_fox_v", "p_fox_logf", "p_band_k", "p_band_v",
        "s_ssm_re", "s_ssm_im", "s_fox_k", "s_fox_v", "s_fox_logf", "s_band_k", "s_band_v")}

    w_glu = ssm_w_glu.astype(BF16)
    fox_in, fox_out = fox_w_in.astype(BF16), fox_w_out.astype(BF16)
    band_in, band_out = band_w_in.astype(BF16), band_w_out.astype(BF16)

    def ffn(xp, xs, g, w_in, w_out, layer):
        xs, w_gate, w_up, w_down = ffn_half_step_casting(xs, g, w_in, w_out, layer)
        return ffn_half_step(xp, g, w_gate, w_up, w_down), xs

    g, p = ssm_a_re.shape[1:]
    fold = lambda a: a.reshape((-1,) + a.shape[2:])
    mats = ssm_matrices(fold(ssm_a_re), fold(ssm_a_im), fold(ssm_log_step), fold(ssm_b_re), fold(ssm_b_im),
                        fold(ssm_c_re), fold(ssm_c_im))

    for i in range(depth):
        kind, j = i % 3, i // 3
        xp, xs = ffn(xp, xs, ffn1_norm[i], ffn1_w_in, ffn1_w_out, i)
        if kind == 0:
            nblk = d // LANES
            zeros = jnp.zeros((bp, g, p), F32)
            streams = ((xp, bp, lp, zeros, zeros, "p"), (xs, bs, ls, state_ssm_re[j], state_ssm_im[j], "s"))
            new_x = []
            for x, nb, length, s_re, s_im, tag in streams:
                h_tm = rmsnorm_time_major(x, mix_norm[i], nb, length)
                a_tm, sf = ssm_core(h_tm, mats, j, ssm_pack_state(s_re, s_im, nblk), ssm_d[j])
                f_re, f_im = ssm_unpack_state(sf, g, p)
                outs[tag + "_ssm_re"].append(f_re); outs[tag + "_ssm_im"].append(f_im)
                new_x.append(ssm_glu_out(x, a_tm, w_glu, j, nb, length))
            xp, xs = new_x
        elif kind == 1:
            wf_t = fox_w_in[j][:, 3 * d:].T.astype(BF16)
            qp, kp, vp, kbp, vbp, lfp = qkv_project(xp, mix_norm[i], fox_in, j, scale, wf_t, fox_b_f[j])
            qs, ks, vs, kbs, vbs, lfs = qkv_project(xs, mix_norm[i], fox_in, j, scale, wf_t, fox_b_f[j])
            lfp = lfp.reshape(nh, bp, lp).transpose(1, 0, 2)
            lfs = lfs.reshape(nh, bs, ls).transpose(1, 0, 2)
            cp = (cumsum_time(lfp) * LOG2E).reshape(bp * nh, 1, lp)
            lf_all = jnp.concatenate([cache_fox_logf[j].astype(F32).transpose(0, 2, 1), lfs,
                                      jnp.zeros((bs, nh, LANES - ls), F32)], axis=2)
            cs = cumsum_time(lf_all) * LOG2E
            op = fox_prompt_attention(qp, kbp, vbp, cp, bp, nh)
            os_ = fox_sample_attention(qs, cache_fox_k, cache_fox_v, j, kbs, vbs, cs, bs, nh)
            xp = out_project(xp, op, fox_out, j)
            xs = out_project(xs, os_, fox_out, j)
            outs["p_fox_k"].append(heads(kp, bp, lp)); outs["p_fox_v"].append(heads(vp, bp, lp))
            outs["p_fox_logf"].append(lfp.transpose(0, 2, 1))
            outs["s_fox_k"].append(heads(ks, bs, ls)); outs["s_fox_v"].append(heads(vs, bs, ls))
            outs["s_fox_logf"].append(lfs.transpose(0, 2, 1))
        else:
            qp, kp, vp, kbp, vbp = qkv_project(xp, mix_norm[i], band_in, j, scale)
            qs, ks, vs, kbs, vbs = qkv_project(xs, mix_norm[i], band_in, j, scale)
            bw = BAND_PREV * CHUNK
            bias = band_bias_tile(band_rel_bias[j], BAND_TQ, bw) * LOG2E
            op = band_prompt_attention(qp, kbp, vbp, bias, bp, nh)
            os_ = band_sample_attention(qs, cache_band_k, cache_band_v, j, kbs, vbs,
                                        bias[:, :ls, bw - window:bw + ls], bs, nh)
            xp = out_project(xp, op, band_out, j)
            xs = out_project(xs, os_, band_out, j)
            keep = min(bw, lp)
            outs["p_band_k"].append(heads(kp, bp, lp)[:, lp - keep:])
            outs["p_band_v"].append(heads(vp, bp, lp)[:, lp - keep:])
            outs["s_band_k"].append(heads(ks, bs, ls)); outs["s_band_v"].append(heads(vs, bs, ls))
        xp, xs = ffn(xp, xs, ffn2_norm[i], ffn2_w_in, ffn2_w_out, i)

    y_prompt = rmsnorm(xp, final_norm).reshape(bp, lp, d)
    y_sample = rmsnorm(xs, final_norm).reshape(bs, ls, d)
    st = {k: jnp.stack(v) for k, v in outs.items()}
    return (y_prompt, y_sample, st["p_ssm_re"], st["p_ssm_im"], st["p_fox_k"], st["p_fox_v"],
            st["p_fox_logf"], st["p_band_k"], st["p_band_v"], st["s_ssm_re"], st["s_ssm_im"],
            st["s_fox_k"], st["s_fox_v"], st["s_fox_logf"], st["s_band_k"], st["s_band_v"])
```

```python
import functools

import jax
import jax.numpy as jnp
import numpy as np
from jax import lax
from jax.experimental import pallas as pl
from jax.experimental.pallas import tpu as pltpu

F32 = jnp.float32
BF16 = jnp.bfloat16

EPS = 1e-6
NEG_INF = -1e30
LOG2E = 1.4426950408889634
CHUNK = 64
BAND_PREV = 8
REL_CLIP = 256
SSM_GROUP = 16
SSM_T = 8
LANES = 128
VMEM_LIMIT = 56 * 1024 * 1024

TM = 512
TM_FFN = 1024
TF = 512
TF_CAST = 256
TM_QKV = 1024
TN = 512
TN_GLU = 1024
BAND_TQ = 256
FOX_TQ = 256
ATTN_HEADS_PER_STEP = 4
FOX_TK = 512
SSM_ROWS = 512


def _params(*sem):
    return pltpu.CompilerParams(dimension_semantics=sem, vmem_limit_bytes=VMEM_LIMIT)


def _rms(x, g):
    return x * lax.rsqrt(jnp.mean(x * x, axis=-1, keepdims=True) + EPS) * g


def _dot(a, b):
    return jnp.dot(a, b, preferred_element_type=F32)


def _dot_nt(a, b):
    return lax.dot_general(a, b, (((1,), (1,)), ((), ())), preferred_element_type=F32)


def _ffn_body(x_ref, g_ref, wg_ref, wu_ref, wo_ref, o_ref, h_ref):
    f = pl.program_id(1)

    @pl.when(f == 0)
    def _():
        x = x_ref[...]
        h_ref[...] = _rms(x, g_ref[...]).astype(BF16)
        o_ref[...] = x

    h = h_ref[...]
    gate = _dot(h, wg_ref[...])
    up = _dot(h, wu_ref[...])
    act = (gate * jax.nn.sigmoid(gate) * up).astype(BF16)
    o_ref[...] += _dot(act, wo_ref[...])


def ffn_half_step(x, g, w_gate, w_up, w_out):
    m, d = x.shape
    f = w_out.shape[0]
    tm = min(TM_FFN, m)
    tf = min(TF, f)
    return pl.pallas_call(
        _ffn_body,
        grid=(m // tm, f // tf),
        in_specs=[
            pl.BlockSpec((tm, d), lambda i, j: (i, 0)),
            pl.BlockSpec((1, d), lambda i, j: (0, 0)),
            pl.BlockSpec((d, tf), lambda i, j: (0, j)),
            pl.BlockSpec((d, tf), lambda i, j: (0, j)),
            pl.BlockSpec((tf, d), lambda i, j: (j, 0)),
        ],
        out_specs=pl.BlockSpec((tm, d), lambda i, j: (i, 0)),
        out_shape=jax.ShapeDtypeStruct((m, d), F32),
        scratch_shapes=[pltpu.VMEM((tm, d), BF16)],
        compiler_params=_params("parallel", "arbitrary"),
        name="ffn_half_step",
    )(x, g.reshape(1, d), w_gate, w_up, w_out)


def _ffn_cast_body(x_ref, g_ref, wg32_ref, wu32_ref, wo32_ref, o_ref, wg_ref, wu_ref, wo_ref, h_ref):
    wg_ref[...] = wg32_ref[...].astype(BF16)
    wu_ref[...] = wu32_ref[...].astype(BF16)
    wo_ref[...] = (0.5 * wo32_ref[...]).astype(BF16)
    _ffn_body(x_ref, g_ref, wg_ref, wu_ref, wo_ref, o_ref, h_ref)


def ffn_half_step_casting(x, g, w_in, w_out, layer):
    m, d = x.shape
    f = w_out.shape[1]
    tf = min(TF_CAST, f)
    nf = f // tf
    once = dict(pipeline_mode=pl.Buffered(1))
    col = pl.BlockSpec((d, tf), lambda i, j: (0, j))
    return pl.pallas_call(
        _ffn_cast_body,
        grid=(1, nf),
        in_specs=[
            pl.BlockSpec((m, d), lambda i, j: (0, 0), **once),
            pl.BlockSpec((1, d), lambda i, j: (0, 0)),
            pl.BlockSpec((None, d, tf), lambda i, j: (layer, 0, j)),
            pl.BlockSpec((None, d, tf), lambda i, j: (layer, 0, j + nf)),
            pl.BlockSpec((None, tf, d), lambda i, j: (layer, j, 0)),
        ],
        out_specs=[pl.BlockSpec((m, d), lambda i, j: (0, 0)), col, col,
                   pl.BlockSpec((tf, d), lambda i, j: (j, 0))],
        out_shape=[jax.ShapeDtypeStruct((m, d), F32), jax.ShapeDtypeStruct((d, f), BF16),
                   jax.ShapeDtypeStruct((d, f), BF16), jax.ShapeDtypeStruct((f, d), BF16)],
        scratch_shapes=[pltpu.VMEM((m, d), BF16)],
        compiler_params=_params("arbitrary", "arbitrary"),
        name="ffn_half_step_casting",
    )(x, g.reshape(1, d), w_in, w_in, w_out)


def _rmsnorm_body(x_ref, g_ref, o_ref):
    o_ref[...] = _rms(x_ref[...], g_ref[...]).astype(o_ref.dtype)


def rmsnorm(x, g):
    m, d = x.shape
    tm = min(TM, m)
    return pl.pallas_call(
        _rmsnorm_body,
        grid=(m // tm,),
        in_specs=[pl.BlockSpec((tm, d), lambda i: (i, 0)),
                  pl.BlockSpec((1, d), lambda i: (0, 0))],
        out_specs=pl.BlockSpec((tm, d), lambda i: (i, 0)),
        out_shape=jax.ShapeDtypeStruct((m, d), F32),
        compiler_params=_params("parallel"),
        name="rmsnorm",
    )(x, g.reshape(1, d))


def _rmsnorm_tm_body(x_ref, g_ref, o_ref):
    h = _rms(x_ref[...], g_ref[...])
    o_ref[...] = pltpu.einshape("bld->lbd", h).reshape(o_ref.shape)


def rmsnorm_time_major(x, g, nb, length):
    d = x.shape[1]
    t = SSM_T
    tl = min(TM // nb, length)
    return pl.pallas_call(
        _rmsnorm_tm_body,
        grid=(length // tl,),
        in_specs=[pl.BlockSpec((nb, tl, d), lambda l: (0, l, 0)),
                  pl.BlockSpec((1, d), lambda l: (0, 0))],
        out_specs=pl.BlockSpec((tl // t, t, nb, d), lambda l: (l, 0, 0, 0)),
        out_shape=jax.ShapeDtypeStruct((length // t, t, nb, d), F32),
        compiler_params=_params("parallel"),
        name="rmsnorm_time_major",
    )(x.reshape(nb, length, d), g.reshape(1, d))


def _qkv_body(scale, has_forget, *refs):
    if has_forget:
        (x_ref, g_ref, wq_ref, wk_ref, wv_ref, wf_ref, bf_ref,
         q_ref, k32_ref, v32_ref, kb_ref, vb_ref, lf_ref, h_ref) = refs
    else:
        (x_ref, g_ref, wq_ref, wk_ref, wv_ref,
         q_ref, k32_ref, v32_ref, kb_ref, vb_ref, h_ref) = refs

    @pl.when(pl.program_id(1) == 0)
    def _():
        h = _rms(x_ref[...], g_ref[...]).astype(BF16)
        h_ref[...] = h
        if has_forget:
            z = _dot_nt(wf_ref[...], h) + bf_ref[...]
            lf_ref[...] = jnp.minimum(z, 0.0) - jnp.log(1.0 + jnp.exp(-jnp.abs(z)))

    h = h_ref[...]
    q_ref[...] = (_dot(h, wq_ref[...]) * scale).astype(BF16)
    k = _dot(h, wk_ref[...])
    k32_ref[...] = k
    kb_ref[...] = k.astype(BF16)
    v = _dot(h, wv_ref[...])
    v32_ref[...] = v
    vb_ref[...] = v.astype(BF16)


def qkv_project(x, g, w, layer, scale, wf_t=None, b_f=None):
    m, d = x.shape
    tm = min(TM_QKV, m)
    tn = min(TN, d)
    nq = d // tn
    has_forget = wf_t is not None
    in_specs = [
        pl.BlockSpec((tm, d), lambda i, n: (i, 0)),
        pl.BlockSpec((1, d), lambda i, n: (0, 0)),
        pl.BlockSpec((None, d, tn), lambda i, n: (layer, 0, n)),
        pl.BlockSpec((None, d, tn), lambda i, n: (layer, 0, n + nq)),
        pl.BlockSpec((None, d, tn), lambda i, n: (layer, 0, n + 2 * nq)),
    ]
    args = [x, g.reshape(1, d), w, w, w]
    tile = pl.BlockSpec((tm, tn), lambda i, n: (i, n))
    out_specs = [tile] * 5
    out_shape = [jax.ShapeDtypeStruct((m, d), BF16),
                 jax.ShapeDtypeStruct((m, d), F32), jax.ShapeDtypeStruct((m, d), F32),
                 jax.ShapeDtypeStruct((m, d), BF16), jax.ShapeDtypeStruct((m, d), BF16)]
    if has_forget:
        nh = wf_t.shape[0]
        in_specs += [pl.BlockSpec((nh, d), lambda i, n: (0, 0)),
                     pl.BlockSpec((nh, 1), lambda i, n: (0, 0))]
        args += [wf_t, b_f.reshape(nh, 1)]
        out_specs = out_specs + [pl.BlockSpec((nh, tm), lambda i, n: (0, i))]
        out_shape = out_shape + [jax.ShapeDtypeStruct((nh, m), F32)]
    return pl.pallas_call(
        functools.partial(_qkv_body, scale, has_forget),
        grid=(m // tm, nq),
        in_specs=in_specs,
        out_specs=out_specs,
        out_shape=out_shape,
        scratch_shapes=[pltpu.VMEM((tm, d), BF16)],
        compiler_params=_params("parallel", "arbitrary"),
        name="qkv_project",
    )(*args)


def _out_proj_body(x_ref, o_ref, w_ref, y_ref):
    y_ref[...] = x_ref[...] + _dot(o_ref[...], w_ref[...])


def out_project(x, o, w, layer):
    m, d = x.shape
    tm = min(TM, m)
    return pl.pallas_call(
        _out_proj_body,
        grid=(m // tm,),
        in_specs=[pl.BlockSpec((tm, d), lambda i: (i, 0)),
                  pl.BlockSpec((tm, d), lambda i: (i, 0)),
                  pl.BlockSpec((None, d, d), lambda i: (layer, 0, 0))],
        out_specs=pl.BlockSpec((tm, d), lambda i: (i, 0)),
        out_shape=jax.ShapeDtypeStruct((m, d), F32),
        compiler_params=_params("parallel"),
        name="out_project",
    )(x, o, w)


def _glu_body(a_ref, xt_ref, wv_ref, wg_ref, o_ref):
    nb, tl, tn = xt_ref.shape
    a = a_ref[...].reshape(tl * nb, a_ref.shape[3]).astype(BF16)
    val = _dot(a, wv_ref[...])
    gate = _dot(a, wg_ref[...])
    r = (val * jax.nn.sigmoid(gate)).reshape(tl, nb, tn)
    o_ref[...] = xt_ref[...] + pltpu.einshape("lbn->bln", r)


def ssm_glu_out(x, a4, w_glu, layer, nb, length):
    d = x.shape[1]
    t = SSM_T
    tl = min(TM // nb, length)
    tn = min(TN_GLU, d)
    nq = d // tn
    tile = pl.BlockSpec((nb, tl, tn), lambda n, l: (0, l, n))
    return pl.pallas_call(
        _glu_body,
        grid=(nq, length // tl),
        in_specs=[
            pl.BlockSpec((tl // t, t, nb, d), lambda n, l: (l, 0, 0, 0)),
            tile,
            pl.BlockSpec((None, d, tn), lambda n, l: (layer, 0, n)),
            pl.BlockSpec((None, d, tn), lambda n, l: (layer, 0, n + nq)),
        ],
        out_specs=tile,
        out_shape=jax.ShapeDtypeStruct((nb, length, d), F32),
        compiler_params=_params("parallel", "parallel"),
        name="ssm_glu_out",
    )(a4, x.reshape(nb, length, d), w_glu, w_glu).reshape(nb * length, d)


def _ssm_lag_body(ca_ref, bb_ref, k_ref):
    for i in range(ca_ref.shape[0]):
        k_ref[i] = jnp.dot(ca_ref[i], bb_ref[i], preferred_element_type=F32,
                           precision=lax.Precision.HIGHEST)


def ssm_lag_kernels(ca, bb, gpb):
    g, r, p2 = ca.shape
    c = bb.shape[2]
    return pl.pallas_call(
        _ssm_lag_body,
        grid=(g // gpb,),
        in_specs=[pl.BlockSpec((gpb, r, p2), lambda i: (i, 0, 0)),
                  pl.BlockSpec((gpb, p2, c), lambda i: (i, 0, 0))],
        out_specs=pl.BlockSpec((gpb, r, c), lambda i: (i, 0, 0)),
        out_shape=jax.ShapeDtypeStruct((g, r, c), F32),
        compiler_params=_params("parallel"),
        name="ssm_lag_kernels",
    )(ca, bb)


def _ssm_core_body(nb, x_ref, bdv_ref, ws_ref, wc_ref, at_ref, s0_ref, dsk_ref, y_ref, sf_ref,
                   ds_ref, dsw_ref, sst_ref, st_ref):
    nchp, t = x_ref.shape[0], x_ref.shape[1]
    m = nchp * nb
    ncol = at_ref.shape[2]
    half = LANES // 2

    def swap_halves(v):
        lane = lax.broadcasted_iota(jnp.int32, v.shape, 1)
        return jnp.where(lane % LANES < half, pltpu.roll(v, ncol - half, 1), pltpu.roll(v, half, 1))

    @pl.when(pl.program_id(1) == 0)
    def _():
        st_ref[...] = s0_ref[0]

    xcat = jnp.concatenate(
        [x_ref[:, tau].reshape(m, LANES).astype(BF16) for tau in range(t)], axis=1)
    ds = _dot(xcat, ws_ref[0])
    ds_ref[...] = ds
    dsw_ref[...] = swap_halves(ds)
    a1 = jnp.broadcast_to(at_ref[0, 0:1, :], (nb, ncol))
    a2 = jnp.broadcast_to(at_ref[0, 1:2, :], (nb, ncol))

    def step(k, carry):
        s, sw = carry
        rows = pl.ds(pl.multiple_of(k * nb, nb), nb)
        sst_ref[rows, :] = s
        ns = a1 * s + a2 * sw + ds_ref[rows, :]
        nsw = a1 * sw - a2 * s + dsw_ref[rows, :]
        return ns, nsw

    s0 = st_ref[...]
    s, _ = lax.fori_loop(0, nchp, step, (s0, swap_halves(s0)))
    st_ref[...] = s
    sf_ref[0] = s
    y_in = _dot(sst_ref[...].astype(BF16), wc_ref[0])
    dsk = dsk_ref[...]
    for tt in range(t):
        yt = y_in[:, tt * LANES:(tt + 1) * LANES] + _dot(
            xcat[:, :(tt + 1) * LANES], bdv_ref[0, (t - 1 - tt) * LANES:, :])
        yt = jax.nn.gelu(yt + dsk * x_ref[:, tt].reshape(m, LANES))
        y_ref[:, tt] = yt.reshape(nchp, nb, LANES)


def ssm_core(h4, mats, layer, s0, d_skip):
    bdv, ws, wc, at = mats
    nch, t, nb, d = h4.shape
    nblk = d // LANES
    nchp = max(min(SSM_ROWS // nb, nch), 1)
    m = nchp * nb
    ncol = ws.shape[2]
    blk0 = layer * nblk
    wblk = lambda *s: pl.BlockSpec((1,) + s, lambda c, k: (c + blk0, 0, 0))
    sblk = pl.BlockSpec((1, nb, ncol), lambda c, k: (c, 0, 0))
    seq = pl.BlockSpec((nchp, t, nb, LANES), lambda c, k: (k, 0, 0, c))
    y, sf = pl.pallas_call(
        functools.partial(_ssm_core_body, nb),
        grid=(nblk, nch // nchp),
        in_specs=[seq, wblk(t * LANES, LANES), wblk(t * LANES, ncol), wblk(ncol, t * LANES),
                  wblk(2, ncol), sblk, pl.BlockSpec((1, LANES), lambda c, k: (0, c))],
        out_specs=[seq, sblk],
        out_shape=[jax.ShapeDtypeStruct((nch, t, nb, d), F32),
                   jax.ShapeDtypeStruct((nblk, nb, ncol), F32)],
        scratch_shapes=[pltpu.VMEM((m, ncol), F32), pltpu.VMEM((m, ncol), F32), pltpu.VMEM((m, ncol), F32),
                        pltpu.VMEM((nb, ncol), F32)],
        compiler_params=_params("parallel", "arbitrary"),
        name="ssm_core",
    )(h4, bdv, ws, wc, at, s0, d_skip.reshape(1, d))
    return y, sf


def ssm_matrices(a_re, a_im, log_step, b_re, b_im, c_re, c_im):
    t = SSM_T
    g, p = a_re.shape
    c = b_re.shape[2]
    gpb = LANES // c
    nblk = g // gpb
    dt = jnp.exp(log_step)[:, None]
    lam_re, lam_im = a_re * dt, a_im * dt
    j = jnp.arange(t + 1, dtype=F32)[:, None, None]
    mag = jnp.exp(lam_re * j)
    pw_re, pw_im = mag * jnp.cos(lam_im * j), mag * jnp.sin(lam_im * j)
    ab_re, ab_im = pw_re[1], pw_im[1]
    den = a_re * a_re + a_im * a_im
    z_re = ((ab_re - 1.0) * a_re + ab_im * a_im) / den
    z_im = (ab_im * a_re - (ab_re - 1.0) * a_im) / den
    bb_re = z_re[..., None] * b_re - z_im[..., None] * b_im
    bb_im = z_re[..., None] * b_im + z_im[..., None] * b_re
    jr = (t - 1) - jnp.arange(t, dtype=F32)[:, None, None]
    mag_r = jnp.exp(lam_re * jr)
    rp_re, rp_im = mag_r * jnp.cos(lam_im * jr), mag_r * jnp.sin(lam_im * jr)
    ws_re = rp_re[:, :, :, None] * bb_re[None] - rp_im[:, :, :, None] * bb_im[None]
    ws_im = rp_re[:, :, :, None] * bb_im[None] + rp_im[:, :, :, None] * bb_re[None]
    wsc = jnp.concatenate([ws_re, ws_im], axis=2).transpose(1, 0, 3, 2).reshape(nblk, gpb, t * c, 2 * p)

    def out_weights(q_re, q_im):
        cr, ci = c_re[None], c_im[None]
        qr, qi = q_re[:, :, None, :], q_im[:, :, None, :]
        return cr * qr - ci * qi, -cr * qi - ci * qr

    wc_re, wc_im = out_weights(pw_re[1:], pw_im[1:])
    wcc = jnp.concatenate([wc_re, wc_im], axis=3).transpose(1, 3, 0, 2).reshape(nblk, gpb, 2 * p, t * c)

    ca_re, ca_im = out_weights(rp_re, rp_im)
    ca = jnp.concatenate([ca_re, ca_im], axis=-1).transpose(1, 0, 2, 3).reshape(g, t * c, 2 * p)
    bb = jnp.concatenate([bb_re, bb_im], axis=1)
    kmat = ssm_lag_kernels(ca, bb, gpb).reshape(g, t, c, c)
    kc = kmat.transpose(0, 1, 3, 2).reshape(nblk, gpb, t * c, c)

    place = np.zeros((gpb, t, gpb, c, t, c), np.float32)
    for h in range(gpb):
        place[h, :, h] = np.eye(t * c, dtype=np.float32).reshape(t, c, t, c)
    place = jnp.asarray(place.reshape(gpb, t * LANES, t * c), BF16)
    expand = functools.partial(jnp.einsum, preferred_element_type=F32)
    ws = expand("hrk,bhkn->brhn", place, wsc.astype(BF16)).reshape(nblk, t * LANES, gpb * 2 * p)
    wc = expand("bhnk,hrk->bhnr", wcc.astype(BF16), place).reshape(nblk, gpb * 2 * p, t * LANES)
    bdv = expand("hrk,bhkc->brhc", place, kc.astype(BF16)).reshape(nblk, t * LANES, LANES)

    at_re = pw_re[t].reshape(nblk, gpb, p)
    at_im = pw_im[t].reshape(nblk, gpb, p)
    a1 = jnp.concatenate([at_re, at_re], axis=2).reshape(nblk, gpb * 2 * p)
    a2 = jnp.concatenate([-at_im, at_im], axis=2).reshape(nblk, gpb * 2 * p)
    return bdv.astype(BF16), ws.astype(BF16), wc.astype(BF16), jnp.stack([a1, a2], axis=1)


def ssm_pack_state(s_re, s_im, nblk):
    nb, g, p = s_re.shape
    s = jnp.stack([s_re.reshape(nb, nblk, g // nblk, p), s_im.reshape(nb, nblk, g // nblk, p)], axis=3)
    return s.transpose(1, 0, 2, 3, 4).reshape(nblk, nb, -1)


def ssm_unpack_state(sf, g, p):
    nblk, nb, _ = sf.shape
    s = sf.reshape(nblk, nb, g // nblk, 2, p).transpose(1, 0, 2, 3, 4).reshape(nb, g, 2, p)
    return s[:, :, 0], s[:, :, 1]


def _split3(x):
    hi = x.astype(BF16)
    r = x - hi.astype(F32)
    mid = r.astype(BF16)
    lo = (r - mid.astype(F32)).astype(BF16)
    return hi, mid, lo


def _cumsum_body(x_ref, o_ref):
    nblk = x_ref.shape[2] // LANES
    rows = x_ref.shape[1]
    row = lax.broadcasted_iota(jnp.int32, (LANES, LANES), 0)
    col = lax.broadcasted_iota(jnp.int32, (LANES, LANES), 1)
    tri = jnp.where(row <= col, 1.0, 0.0).astype(BF16)
    carry = jnp.zeros((rows, 1), F32)
    for j in range(nblk):
        sl = slice(j * LANES, (j + 1) * LANES)
        hi, mid, lo = _split3(x_ref[0, :, sl])
        w = _dot(hi, tri) + _dot(mid, tri) + _dot(lo, tri)
        o_ref[0, :, sl] = w + carry
        carry = carry + jnp.sum(x_ref[0, :, sl], axis=1, keepdims=True)


def cumsum_time(x):
    n, h, length = x.shape
    return pl.pallas_call(
        _cumsum_body,
        grid=(n,),
        in_specs=[pl.BlockSpec((1, h, length), lambda i: (i, 0, 0))],
        out_specs=pl.BlockSpec((1, h, length), lambda i: (i, 0, 0)),
        out_shape=jax.ShapeDtypeStruct((n, h, length), F32),
        compiler_params=_params("parallel"),
        name="cumsum_time",
    )(x)


def _softmax_pv(parts):
    m = functools.reduce(jnp.maximum, [jnp.max(s, axis=-1, keepdims=True) for s, _ in parts])
    acc, den = None, None
    for s, v in parts:
        e = jnp.exp2(s - m)
        l = jnp.sum(e, axis=-1, keepdims=True)
        o = _dot(e.astype(BF16), v)
        acc = o if acc is None else acc + o
        den = l if den is None else den + l
    return acc / den


def _causal(n):
    rpos = lax.broadcasted_iota(jnp.int32, (n, n), 0)
    cpos = lax.broadcasted_iota(jnp.int32, (n, n), 1)
    return cpos <= rpos


def _fox_prompt_body(tq, hd, q_ref, k_ref, v_ref, c_ref, o_ref):
    length = q_ref.shape[0]
    keep = _causal(tq)
    for i in range(length // tq):
        lo, hi = i * tq, (i + 1) * tq
        for hh in range(q_ref.shape[1] // hd):
            cols = slice(hh * hd, (hh + 1) * hd)
            q = q_ref[lo:hi, cols]
            s = _dot_nt(q, k_ref[0:hi, cols]) - c_ref[hh, :, 0:hi]
            s_diag = jnp.where(keep, s[:, lo:hi], NEG_INF)
            m = jnp.max(s_diag, axis=-1, keepdims=True)
            if i > 0:
                m = jnp.maximum(m, jnp.max(s[:, 0:lo], axis=-1, keepdims=True))
                e = jnp.concatenate([jnp.exp2(s[:, 0:lo] - m), jnp.exp2(s_diag - m)], axis=1)
            else:
                e = jnp.exp2(s_diag - m)
            den = jnp.sum(e, axis=-1, keepdims=True)
            o_ref[lo:hi, cols] = (_dot(e.astype(BF16), v_ref[0:hi, cols]) / den).astype(o_ref.dtype)


def fox_prompt_attention(q, kb, vb, c, nb, nh):
    m, d = q.shape
    length = m // nb
    hd = d // nh
    tq = min(FOX_TQ, length)
    hps = min(ATTN_HEADS_PER_STEP, nh)
    ngrp = nh // hps
    seq = pl.BlockSpec((length, hps * hd), lambda b, h: (b, h))
    return pl.pallas_call(
        functools.partial(_fox_prompt_body, tq, hd),
        grid=(nb, ngrp),
        in_specs=[seq, seq, seq, pl.BlockSpec((hps, 1, length), lambda b, h: (b * ngrp + h, 0, 0))],
        out_specs=seq,
        out_shape=jax.ShapeDtypeStruct((m, d), BF16),
        compiler_params=_params("parallel", "parallel"),
        name="fox_prompt_attention",
    )(q, kb, vb, c)


def _fox_sample_body(nh, q_ref, kc_ref, vc_ref, kn_ref, vn_ref, cc_ref, cn_ref, o_ref,
                     m_ref, l_ref, acc_ref):
    j = pl.program_id(1)
    lq, d = q_ref.shape
    hd = d // nh

    @pl.when(j == 0)
    def _():
        m_ref[...] = jnp.full_like(m_ref, NEG_INF)
        l_ref[...] = jnp.zeros_like(l_ref)
        acc_ref[...] = jnp.zeros_like(acc_ref)

    heads = lambda ref: [ref[:, h * hd:(h + 1) * hd] for h in range(nh)]
    qs = heads(q_ref)

    def update(s, vs):
        m_old = m_ref[...]
        m_new = jnp.maximum(m_old, jnp.max(s, axis=-1, keepdims=True))
        alpha = jnp.exp2(m_old - m_new)
        e = jnp.exp2(s - m_new)
        l_ref[...] = alpha * l_ref[...] + jnp.sum(e, axis=-1, keepdims=True)
        e = e.astype(BF16)
        pv = jnp.concatenate([_dot(e[h * lq:(h + 1) * lq], vs[h]) for h in range(nh)], axis=0)
        acc_ref[...] = alpha * acc_ref[...] + pv
        m_ref[...] = m_new

    k_all = pltpu.einshape("lhd->hld", kc_ref[...])
    v_all = pltpu.einshape("lhd->hld", vc_ref[...])
    s = jnp.concatenate([_dot_nt(qs[h], k_all[h].astype(BF16)) - cc_ref[0, h:h + 1, :]
                         for h in range(nh)], axis=0)
    update(s, [v_all[h].astype(BF16) for h in range(nh)])

    @pl.when(j == pl.num_programs(1) - 1)
    def _():
        keep = _causal(lq)
        kn = heads(kn_ref)
        s = jnp.concatenate(
            [jnp.where(keep, _dot_nt(qs[h], kn[h]) - cn_ref[0, h:h + 1, 0:lq], NEG_INF)
             for h in range(nh)], axis=0)
        update(s, heads(vn_ref))
        o = acc_ref[...] / l_ref[...]
        for h in range(nh):
            o_ref[:, h * hd:(h + 1) * hd] = o[h * lq:(h + 1) * lq].astype(o_ref.dtype)


def fox_sample_attention(q, k_cache, v_cache, layer, kb, vb, c, nb, nh):
    m, d = q.shape
    lq = m // nb
    past, hd = k_cache.shape[2], k_cache.shape[4]
    tk = min(FOX_TK, past)
    nkv = past // tk
    new = pl.BlockSpec((lq, d), lambda b, j: (b, 0))
    cache = pl.BlockSpec((None, None, tk, nh, hd), lambda b, j: (layer, b, j, 0, 0))
    return pl.pallas_call(
        functools.partial(_fox_sample_body, nh),
        grid=(nb, nkv),
        in_specs=[new, cache, cache, new, new,
                  pl.BlockSpec((1, nh, tk), lambda b, j: (b, 0, j)),
                  pl.BlockSpec((1, nh, LANES), lambda b, j: (b, 0, past // LANES))],
        out_specs=new,
        out_shape=jax.ShapeDtypeStruct((m, d), BF16),
        scratch_shapes=[pltpu.VMEM((nh * lq, 1), F32), pltpu.VMEM((nh * lq, 1), F32),
                        pltpu.VMEM((nh * lq, hd), F32)],
        compiler_params=_params("parallel", "arbitrary"),
        name="fox_sample_attention",
    )(q, k_cache, v_cache, kb, vb, c, c)


def _band_prompt_body(tq, hd, q_ref, k_ref, v_ref, bias_ref, o_ref):
    length = q_ref.shape[0]
    span = bias_ref.shape[2]
    for i in range(length // tq):
        k0 = max((i + 1) * tq - span, 0)
        k1 = (i + 1) * tq
        for hh in range(q_ref.shape[1] // hd):
            cols = slice(hh * hd, (hh + 1) * hd)
            q = q_ref[i * tq:(i + 1) * tq, cols]
            s = _dot_nt(q, k_ref[k0:k1, cols]) + bias_ref[hh, :, span - (k1 - k0):span]
            o_ref[i * tq:(i + 1) * tq, cols] = _softmax_pv([(s, v_ref[k0:k1, cols])]).astype(o_ref.dtype)


def band_prompt_attention(q, kb, vb, bias, nb, nh):
    m, d = q.shape
    length = m // nb
    hd = d // nh
    tq = bias.shape[1]
    hps = min(ATTN_HEADS_PER_STEP, nh)
    seq = pl.BlockSpec((length, hps * hd), lambda b, h: (b, h))
    return pl.pallas_call(
        functools.partial(_band_prompt_body, tq, hd),
        grid=(nb, nh // hps),
        in_specs=[seq, seq, seq, pl.BlockSpec((hps,) + bias.shape[1:], lambda b, h: (h, 0, 0))],
        out_specs=seq,
        out_shape=jax.ShapeDtypeStruct((m, d), BF16),
        compiler_params=_params("parallel", "parallel"),
        name="band_prompt_attention",
    )(q, kb, vb, bias)


def _band_sample_body(nh, q_ref, kc_ref, vc_ref, kn_ref, vn_ref, bias_ref, o_ref):
    lq, d = q_ref.shape
    hd = d // nh
    lc = kc_ref.shape[0]
    k_all = pltpu.einshape("lhd->hld", kc_ref[...])
    v_all = pltpu.einshape("lhd->hld", vc_ref[...])
    heads = lambda ref: [ref[:, h * hd:(h + 1) * hd] for h in range(nh)]
    qs, kn, vn = heads(q_ref), heads(kn_ref), heads(vn_ref)
    s_c = jnp.concatenate([_dot_nt(qs[h], k_all[h].astype(BF16)) + bias_ref[h, :, 0:lc]
                           for h in range(nh)], axis=0)
    s_n = jnp.concatenate([_dot_nt(qs[h], kn[h]) + bias_ref[h, :, lc:lc + lq] for h in range(nh)], axis=0)
    m = jnp.maximum(jnp.max(s_c, axis=-1, keepdims=True), jnp.max(s_n, axis=-1, keepdims=True))
    e_c, e_n = jnp.exp2(s_c - m), jnp.exp2(s_n - m)
    inv = 1.0 / (jnp.sum(e_c, axis=-1, keepdims=True) + jnp.sum(e_n, axis=-1, keepdims=True))
    e_c, e_n = e_c.astype(BF16), e_n.astype(BF16)
    for h in range(nh):
        rows = slice(h * lq, (h + 1) * lq)
        o = _dot(e_c[rows], v_all[h].astype(BF16)) + _dot(e_n[rows], vn[h])
        o_ref[:, h * hd:(h + 1) * hd] = (o * inv[rows]).astype(o_ref.dtype)


def band_sample_attention(q, k_cache, v_cache, layer, kb, vb, bias, nb, nh):
    m, d = q.shape
    lq = m // nb
    lc, hd = k_cache.shape[2], k_cache.shape[4]
    new = pl.BlockSpec((lq, d), lambda b: (b, 0))
    cache = pl.BlockSpec((None, None, lc, nh, hd), lambda b: (layer, b, 0, 0, 0))
    return pl.pallas_call(
        functools.partial(_band_sample_body, nh),
        grid=(nb,),
        in_specs=[new, cache, cache, new, new, pl.BlockSpec(bias.shape, lambda b: (0, 0, 0))],
        out_specs=new,
        out_shape=jax.ShapeDtypeStruct((m, d), BF16),
        compiler_params=_params("parallel"),
        name="band_sample_attention",
    )(q, k_cache, v_cache, kb, vb, bias)


def band_bias_tile(rel_bias, tq, window):
    nh = rel_bias.shape[0]
    span = window + tq
    n = span + tq
    rel = (span - 1) - np.arange(n)
    diag = rel_bias.astype(F32)[:, np.clip(rel, -REL_CLIP, REL_CLIP) + REL_CLIP]
    skew = jnp.broadcast_to(diag[:, None, :], (nh, tq, n)).reshape(nh, tq * n)
    skew = skew[:, :tq * (n - 1)].reshape(nh, tq, n - 1)[:, :, tq - 1:tq - 1 + span]
    qpos = np.arange(tq)[:, None]
    kpos = np.arange(span)[None, :] - window
    qc, kc = qpos // CHUNK, kpos // CHUNK
    visible = (kc <= qc) & (kc >= qc - BAND_PREV)
    return jnp.where(visible[None], skew, NEG_INF)


def kernel(x_prompt, x_sample, state_ssm_re, state_ssm_im, cache_fox_k, cache_fox_v, cache_fox_logf,
           cache_band_k, cache_band_v, ffn1_norm, ffn1_w_in, ffn1_w_out, mix_norm, ffn2_norm, ffn2_w_in,
           ffn2_w_out, ssm_a_re, ssm_a_im, ssm_log_step, ssm_b_re, ssm_b_im, ssm_c_re, ssm_c_im, ssm_d,
           ssm_w_glu, fox_w_in, fox_b_f, fox_w_out, band_w_in, band_rel_bias, band_w_out, final_norm):
    bp, lp, d = x_prompt.shape
    bs, ls, _ = x_sample.shape
    depth = ffn1_norm.shape[0]
    nh = fox_b_f.shape[1]
    hd = d // nh
    scale = hd ** -0.5 * LOG2E
    past = cache_fox_k.shape[2]
    window = cache_band_k.shape[2]
    heads = lambda t, nb, length: t.reshape(nb, length, nh, hd)

    xp = x_prompt.reshape(bp * lp, d)
    xs = x_sample.reshape(bs * ls, d)
    outs = {name: [] for name in (
        "p_ssm_re", "p_ssm_im", "p_fox_k", "p_fox_v", "p_fox_logf", "p_band_k", "p_band_v",
        "s_ssm_re", "s_ssm_im", "s_fox_k", "s_fox_v", "s_fox_logf", "s_band_k", "s_band_v")}

    w_glu = ssm_w_glu.astype(BF16)
    fox_in, fox_out = fox_w_in.astype(BF16), fox_w_out.astype(BF16)
    band_in, band_out = band_w_in.astype(BF16), band_w_out.astype(BF16)

    def ffn(xp, xs, g, w_in, w_out, layer):
        xs, w_gate, w_up, w_down = ffn_half_step_casting(xs, g, w_in, w_out, layer)
        return ffn_half_step(xp, g, w_gate, w_up, w_down), xs

    g, p = ssm_a_re.shape[1:]
    fold = lambda a: a.reshape((-1,) + a.shape[2:])
    mats = ssm_matrices(fold(ssm_a_re), fold(ssm_a_im), fold(ssm_log_step), fold(ssm_b_re), fold(ssm_b_im),
                        fold(ssm_c_re), fold(ssm_c_im))

    for i in range(depth):
        kind, j = i % 3, i // 3
        xp, xs = ffn(xp, xs, ffn1_norm[i], ffn1_w_in, ffn1_w_out, i)
        if kind == 0:
            nblk = d // LANES
            zeros = jnp.zeros((bp, g, p), F32)
            streams = ((xp, bp, lp, zeros, zeros, "p"), (xs, bs, ls, state_ssm_re[j], state_ssm_im[j], "s"))
            new_x = []
            for x, nb, length, s_re, s_im, tag in streams:
                h_tm = rmsnorm_time_major(x, mix_norm[i], nb, length)
                a_tm, sf = ssm_core(h_tm, mats, j, ssm_pack_state(s_re, s_im, nblk), ssm_d[j])
                f_re, f_im = ssm_unpack_state(sf, g, p)
                outs[tag + "_ssm_re"].append(f_re); outs[tag + "_ssm_im"].append(f_im)
                new_x.append(ssm_glu_out(x, a_tm, w_glu, j, nb, length))
            xp, xs = new_x
        elif kind == 1:
            wf_t = fox_w_in[j][:, 3 * d:].T.astype(BF16)
            qp, kp, vp, kbp, vbp, lfp = qkv_project(xp, mix_norm[i], fox_in, j, scale, wf_t, fox_b_f[j])
            qs, ks, vs, kbs, vbs, lfs = qkv_project(xs, mix_norm[i], fox_in, j, scale, wf_t, fox_b_f[j])
            lfp = lfp.reshape(nh, bp, lp).transpose(1, 0, 2)
            lfs = lfs.reshape(nh, bs, ls).transpose(1, 0, 2)
            cp = (cumsum_time(lfp) * LOG2E).reshape(bp * nh, 1, lp)
            lf_all = jnp.concatenate([cache_fox_logf[j].astype(F32).transpose(0, 2, 1), lfs,
                                      jnp.zeros((bs, nh, LANES - ls), F32)], axis=2)
            cs = cumsum_time(lf_all) * LOG2E
            op = fox_prompt_attention(qp, kbp, vbp, cp, bp, nh)
            os_ = fox_sample_attention(qs, cache_fox_k, cache_fox_v, j, kbs, vbs, cs, bs, nh)
            xp = out_project(xp, op, fox_out, j)
            xs = out_project(xs, os_, fox_out, j)
            outs["p_fox_k"].append(heads(kp, bp, lp)); outs["p_fox_v"].append(heads(vp, bp, lp))
            outs["p_fox_logf"].append(lfp.transpose(0, 2, 1))
            outs["s_fox_k"].append(heads(ks, bs, ls)); outs["s_fox_v"].append(heads(vs, bs, ls))
            outs["s_fox_logf"].append(lfs.transpose(0, 2, 1))
        else:
            qp, kp, vp, kbp, vbp = qkv_project(xp, mix_norm[i], band_in, j, scale)
            qs, ks, vs, kbs, vbs = qkv_project(xs, mix_norm[i], band_in, j, scale)
            bw = BAND_PREV * CHUNK
            bias = band_bias_tile(band_rel_bias[j], BAND_TQ, bw) * LOG2E
            op = band_prompt_attention(qp, kbp, vbp, bias, bp, nh)
            os_ = band_sample_attention(qs, cache_band_k, cache_band_v, j, kbs, vbs,
                                        bias[:, :ls, bw - window:bw + ls], bs, nh)
            xp = out_project(xp, op, band_out, j)
            xs = out_project(xs, os_, band_out, j)
            keep = min(bw, lp)
            outs["p_band_k"].append(heads(kp, bp, lp)[:, lp - keep:])
            outs["p_band_v"].append(heads(vp, bp, lp)[:, lp - keep:])
            outs["s_band_k"].append(heads(ks, bs, ls)); outs["s_band_v"].append(heads(vs, bs, ls))
        xp, xs = ffn(xp, xs, ffn2_norm[i], ffn2_w_in, ffn2_w_out, i)

    y_prompt = rmsnorm(xp, final_norm).reshape(bp, lp, d)
    y_sample = rmsnorm(xs, final_norm).reshape(bs, ls, d)
    st = {k: jnp.stack(v) for k, v in outs.items()}
    return (y_prompt, y_sample, st["p_ssm_re"], st["p_ssm_im"], st["p_fox_k"], st["p_fox_v"],
            st["p_fox_logf"], st["p_band_k"], st["p_band_v"], st["s_ssm_re"], st["s_ssm_im"],
            st["s_fox_k"], st["s_fox_v"], st["s_fox_logf"], st["s_band_k"], st["s_band_v"])
```

```python
import functools

import jax
import jax.numpy as jnp
import numpy as np
from jax import lax
from jax.experimental import pallas as pl
from jax.experimental.pallas import tpu as pltpu

F32 = jnp.float32
BF16 = jnp.bfloat16

EPS = 1e-6
NEG_INF = -1e30
LOG2E = 1.4426950408889634
CHUNK = 64
BAND_PREV = 8
REL_CLIP = 256
SSM_GROUP = 16
SSM_T = 8
LANES = 128
VMEM_LIMIT = 56 * 1024 * 1024

TM = 512
TM_FFN = 1024
TF = 512
TF_CAST = 256
TM_QKV = 1024
TN = 512
TN_GLU = 1024
BAND_TQ = 256
FOX_TQ = 256
ATTN_HEADS_PER_STEP = 4
FOX_TK = 512
SSM_ROWS = 512


def _params(*sem):
    return pltpu.CompilerParams(dimension_semantics=sem, vmem_limit_bytes=VMEM_LIMIT)


def _rms(x, g):
    return x * lax.rsqrt(jnp.mean(x * x, axis=-1, keepdims=True) + EPS) * g


def _dot(a, b):
    return jnp.dot(a, b, preferred_element_type=F32)


def _dot_nt(a, b):
    return lax.dot_general(a, b, (((1,), (1,)), ((), ())), preferred_element_type=F32)


def _ffn_body(x_ref, g_ref, wg_ref, wu_ref, wo_ref, o_ref, h_ref):
    f = pl.program_id(1)

    @pl.when(f == 0)
    def _():
        x = x_ref[...]
        h_ref[...] = _rms(x, g_ref[...]).astype(BF16)
        o_ref[...] = x

    h = h_ref[...]
    gate = _dot(h, wg_ref[...])
    up = _dot(h, wu_ref[...])
    act = (gate * jax.nn.sigmoid(gate) * up).astype(BF16)
    o_ref[...] += _dot(act, wo_ref[...])


def ffn_half_step(x, g, w_gate, w_up, w_out):
    m, d = x.shape
    f = w_out.shape[0]
    tm = min(TM_FFN, m)
    tf = min(TF, f)
    return pl.pallas_call(
        _ffn_body,
        grid=(m // tm, f // tf),
        in_specs=[
            pl.BlockSpec((tm, d), lambda i, j: (i, 0)),
            pl.BlockSpec((1, d), lambda i, j: (0, 0)),
            pl.BlockSpec((d, tf), lambda i, j: (0, j)),
            pl.BlockSpec((d, tf), lambda i, j: (0, j)),
            pl.BlockSpec((tf, d), lambda i, j: (j, 0)),
        ],
        out_specs=pl.BlockSpec((tm, d), lambda i, j: (i, 0)),
        out_shape=jax.ShapeDtypeStruct((m, d), F32),
        scratch_shapes=[pltpu.VMEM((tm, d), BF16)],
        compiler_params=_params("parallel", "arbitrary"),
        name="ffn_half_step",
    )(x, g.reshape(1, d), w_gate, w_up, w_out)


def _ffn_cast_body(x_ref, g_ref, wg32_ref, wu32_ref, wo32_ref, o_ref, wg_ref, wu_ref, wo_ref, h_ref):
    wg_ref[...] = wg32_ref[...].astype(BF16)
    wu_ref[...] = wu32_ref[...].astype(BF16)
    wo_ref[...] = (0.5 * wo32_ref[...]).astype(BF16)
    _ffn_body(x_ref, g_ref, wg_ref, wu_ref, wo_ref, o_ref, h_ref)


def ffn_half_step_casting(x, g, w_in, w_out, layer):
    m, d = x.shape
    f = w_out.shape[1]
    tf = min(TF_CAST, f)
    nf = f // tf
    once = dict(pipeline_mode=pl.Buffered(1))
    col = pl.BlockSpec((d, tf), lambda i, j: (0, j))
    return pl.pallas_call(
        _ffn_cast_body,
        grid=(1, nf),
        in_specs=[
            pl.BlockSpec((m, d), lambda i, j: (0, 0), **once),
            pl.BlockSpec((1, d), lambda i, j: (0, 0)),
            pl.BlockSpec((None, d, tf), lambda i, j: (layer, 0, j)),
            pl.BlockSpec((None, d, tf), lambda i, j: (layer, 0, j + nf)),
            pl.BlockSpec((None, tf, d), lambda i, j: (layer, j, 0)),
        ],
        out_specs=[pl.BlockSpec((m, d), lambda i, j: (0, 0)), col, col,
                   pl.BlockSpec((tf, d), lambda i, j: (j, 0))],
        out_shape=[jax.ShapeDtypeStruct((m, d), F32), jax.ShapeDtypeStruct((d, f), BF16),
                   jax.ShapeDtypeStruct((d, f), BF16), jax.ShapeDtypeStruct((f, d), BF16)],
        scratch_shapes=[pltpu.VMEM((m, d), BF16)],
        compiler_params=_params("arbitrary", "arbitrary"),
        name="ffn_half_step_casting",
    )(x, g.reshape(1, d), w_in, w_in, w_out)


def _rmsnorm_body(x_ref, g_ref, o_ref):
    o_ref[...] = _rms(x_ref[...], g_ref[...]).astype(o_ref.dtype)


def rmsnorm(x, g):
    m, d = x.shape
    tm = min(TM, m)
    return pl.pallas_call(
        _rmsnorm_body,
        grid=(m // tm,),
        in_specs=[pl.BlockSpec((tm, d), lambda i: (i, 0)),
                  pl.BlockSpec((1, d), lambda i: (0, 0))],
        out_specs=pl.BlockSpec((tm, d), lambda i: (i, 0)),
        out_shape=jax.ShapeDtypeStruct((m, d), F32),
        compiler_params=_params("parallel"),
        name="rmsnorm",
    )(x, g.reshape(1, d))


def _rmsnorm_tm_body(x_ref, g_ref, o_ref):
    h = _rms(x_ref[...], g_ref[...])
    o_ref[...] = pltpu.einshape("bld->lbd", h).reshape(o_ref.shape)


def rmsnorm_time_major(x, g, nb, length):
    d = x.shape[1]
    t = SSM_T
    tl = min(TM // nb, length)
    return pl.pallas_call(
        _rmsnorm_tm_body,
        grid=(length // tl,),
        in_specs=[pl.BlockSpec((nb, tl, d), lambda l: (0, l, 0)),
                  pl.BlockSpec((1, d), lambda l: (0, 0))],
        out_specs=pl.BlockSpec((tl // t, t, nb, d), lambda l: (l, 0, 0, 0)),
        out_shape=jax.ShapeDtypeStruct((length // t, t, nb, d), F32),
        compiler_params=_params("parallel"),
        name="rmsnorm_time_major",
    )(x.reshape(nb, length, d), g.reshape(1, d))


def _qkv_body(scale, has_forget, *refs):
    if has_forget:
        (x_ref, g_ref, wq_ref, wk_ref, wv_ref, wf_ref, bf_ref,
         q_ref, k32_ref, v32_ref, kb_ref, vb_ref, lf_ref, h_ref) = refs
    else:
        (x_ref, g_ref, wq_ref, wk_ref, wv_ref,
         q_ref, k32_ref, v32_ref, kb_ref, vb_ref, h_ref) = refs

    @pl.when(pl.program_id(1) == 0)
    def _():
        h = _rms(x_ref[...], g_ref[...]).astype(BF16)
        h_ref[...] = h
        if has_forget:
            z = _dot_nt(wf_ref[...], h) + bf_ref[...]
            lf_ref[...] = jnp.minimum(z, 0.0) - jnp.log(1.0 + jnp.exp(-jnp.abs(z)))

    h = h_ref[...]
    q_ref[...] = (_dot(h, wq_ref[...]) * scale).astype(BF16)
    k = _dot(h, wk_ref[...])
    k32_ref[...] = k
    kb_ref[...] = k.astype(BF16)
    v = _dot(h, wv_ref[...])
    v32_ref[...] = v
    vb_ref[...] = v.astype(BF16)


def qkv_project(x, g, w, layer, scale, wf_t=None, b_f=None):
    m, d = x.shape
    tm = min(TM_QKV, m)
    tn = min(TN, d)
    nq = d // tn
    has_forget = wf_t is not None
    in_specs = [
        pl.BlockSpec((tm, d), lambda i, n: (i, 0)),
        pl.BlockSpec((1, d), lambda i, n: (0, 0)),
        pl.BlockSpec((None, d, tn), lambda i, n: (layer, 0, n)),
        pl.BlockSpec((None, d, tn), lambda i, n: (layer, 0, n + nq)),
        pl.BlockSpec((None, d, tn), lambda i, n: (layer, 0, n + 2 * nq)),
    ]
    args = [x, g.reshape(1, d), w, w, w]
    tile = pl.BlockSpec((tm, tn), lambda i, n: (i, n))
    out_specs = [tile] * 5
    out_shape = [jax.ShapeDtypeStruct((m, d), BF16),
                 jax.ShapeDtypeStruct((m, d), F32), jax.ShapeDtypeStruct((m, d), F32),
                 jax.ShapeDtypeStruct((m, d), BF16), jax.ShapeDtypeStruct((m, d), BF16)]
    if has_forget:
        nh = wf_t.shape[0]
        in_specs += [pl.BlockSpec((nh, d), lambda i, n: (0, 0)),
                     pl.BlockSpec((nh, 1), lambda i, n: (0, 0))]
        args += [wf_t, b_f.reshape(nh, 1)]
        out_specs = out_specs + [pl.BlockSpec((nh, tm), lambda i, n: (0, i))]
        out_shape = out_shape + [jax.ShapeDtypeStruct((nh, m), F32)]
    return pl.pallas_call(
        functools.partial(_qkv_body, scale, has_forget),
        grid=(m // tm, nq),
        in_specs=in_specs,
        out_specs=out_specs,
        out_shape=out_shape,
        scratch_shapes=[pltpu.VMEM((tm, d), BF16)],
        compiler_params=_params("parallel", "arbitrary"),
        name="qkv_project",
    )(*args)


def _out_proj_body(x_ref, o_ref, w_ref, y_ref):
    y_ref[...] = x_ref[...] + _dot(o_ref[...], w_ref[...])


def out_project(x, o, w, layer):
    m, d = x.shape
    tm = min(TM, m)
    return pl.pallas_call(
        _out_proj_body,
        grid=(m // tm,),
        in_specs=[pl.BlockSpec((tm, d), lambda i: (i, 0)),
                  pl.BlockSpec((tm, d), lambda i: (i, 0)),
                  pl.BlockSpec((None, d, d), lambda i: (layer, 0, 0))],
        out_specs=pl.BlockSpec((tm, d), lambda i: (i, 0)),
        out_shape=jax.ShapeDtypeStruct((m, d), F32),
        compiler_params=_params("parallel"),
        name="out_project",
    )(x, o, w)


def _glu_body(a_ref, xt_ref, wv_ref, wg_ref, o_ref):
    nb, tl, tn = xt_ref.shape
    a = a_ref[...].reshape(tl * nb, a_ref.shape[3]).astype(BF16)
    val = _dot(a, wv_ref[...])
    gate = _dot(a, wg_ref[...])
    r = (val * jax.nn.sigmoid(gate)).reshape(tl, nb, tn)
    o_ref[...] = xt_ref[...] + pltpu.einshape("lbn->bln", r)


def ssm_glu_out(x, a4, w_glu, layer, nb, length):
    d = x.shape[1]
    t = SSM_T
    tl = min(TM // nb, length)
    tn = min(TN_GLU, d)
    nq = d // tn
    tile = pl.BlockSpec((nb, tl, tn), lambda n, l: (0, l, n))
    return pl.pallas_call(
        _glu_body,
        grid=(nq, length // tl),
        in_specs=[
            pl.BlockSpec((tl // t, t, nb, d), lambda n, l: (l, 0, 0, 0)),
            tile,
            pl.BlockSpec((None, d, tn), lambda n, l: (layer, 0, n)),
            pl.BlockSpec((None, d, tn), lambda n, l: (layer, 0, n + nq)),
        ],
        out_specs=tile,
        out_shape=jax.ShapeDtypeStruct((nb, length, d), F32),
        compiler_params=_params("parallel", "parallel"),
        name="ssm_glu_out",
    )(a4, x.reshape(nb, length, d), w_glu, w_glu).reshape(nb * length, d)


def _ssm_lag_body(ca_ref, bb_ref, k_ref):
    for i in range(ca_ref.shape[0]):
        k_ref[i] = jnp.dot(ca_ref[i], bb_ref[i], preferred_element_type=F32,
                           precision=lax.Precision.HIGHEST)


def ssm_lag_kernels(ca, bb, gpb):
    g, r, p2 = ca.shape
    c = bb.shape[2]
    return pl.pallas_call(
        _ssm_lag_body,
        grid=(g // gpb,),
        in_specs=[pl.BlockSpec((gpb, r, p2), lambda i: (i, 0, 0)),
                  pl.BlockSpec((gpb, p2, c), lambda i: (i, 0, 0))],
        out_specs=pl.BlockSpec((gpb, r, c), lambda i: (i, 0, 0)),
        out_shape=jax.ShapeDtypeStruct((g, r, c), F32),
        compiler_params=_params("parallel"),
        name="ssm_lag_kernels",
    )(ca, bb)


def _ssm_core_body(nb, x_ref, bdv_ref, ws_ref, wc_ref, at_ref, s0_ref, dsk_ref, y_ref, sf_ref,
                   ds_ref, dsw_ref, sst_ref, st_ref):
    nchp, t = x_ref.shape[0], x_ref.shape[1]
    m = nchp * nb
    ncol = at_ref.shape[2]
    half = LANES // 2

    def swap_halves(v):
        lane = lax.broadcasted_iota(jnp.int32, v.shape, 1)
        return jnp.where(lane % LANES < half, pltpu.roll(v, ncol - half, 1), pltpu.roll(v, half, 1))

    @pl.when(pl.program_id(1) == 0)
    def _():
        st_ref[...] = s0_ref[0]

    xcat = jnp.concatenate(
        [x_ref[:, tau].reshape(m, LANES).astype(BF16) for tau in range(t)], axis=1)
    ds = _dot(xcat, ws_ref[0])
    ds_ref[...] = ds
    dsw_ref[...] = swap_halves(ds)
    a1 = jnp.broadcast_to(at_ref[0, 0:1, :], (nb, ncol))
    a2 = jnp.broadcast_to(at_ref[0, 1:2, :], (nb, ncol))

    def step(k, carry):
        s, sw = carry
        rows = pl.ds(pl.multiple_of(k * nb, nb), nb)
        sst_ref[rows, :] = s
        ns = a1 * s + a2 * sw + ds_ref[rows, :]
        nsw = a1 * sw - a2 * s + dsw_ref[rows, :]
        return ns, nsw

    s0 = st_ref[...]
    s, _ = lax.fori_loop(0, nchp, step, (s0, swap_halves(s0)))
    st_ref[...] = s
    sf_ref[0] = s
    y_in = _dot(sst_ref[...].astype(BF16), wc_ref[0])
    dsk = dsk_ref[...]
    for tt in range(t):
        yt = y_in[:, tt * LANES:(tt + 1) * LANES] + _dot(
            xcat[:, :(tt + 1) * LANES], bdv_ref[0, (t - 1 - tt) * LANES:, :])
        yt = jax.nn.gelu(yt + dsk * x_ref[:, tt].reshape(m, LANES))
        y_ref[:, tt] = yt.reshape(nchp, nb, LANES)


def ssm_core(h4, mats, layer, s0, d_skip):
    bdv, ws, wc, at = mats
    nch, t, nb, d = h4.shape
    nblk = d // LANES
    nchp = max(min(SSM_ROWS // nb, nch), 1)
    m = nchp * nb
    ncol = ws.shape[2]
    blk0 = layer * nblk
    wblk = lambda *s: pl.BlockSpec((1,) + s, lambda c, k: (c + blk0, 0, 0))
    sblk = pl.BlockSpec((1, nb, ncol), lambda c, k: (c, 0, 0))
    seq = pl.BlockSpec((nchp, t, nb, LANES), lambda c, k: (k, 0, 0, c))
    y, sf = pl.pallas_call(
        functools.partial(_ssm_core_body, nb),
        grid=(nblk, nch // nchp),
        in_specs=[seq, wblk(t * LANES, LANES), wblk(t * LANES, ncol), wblk(ncol, t * LANES),
                  wblk(2, ncol), sblk, pl.BlockSpec((1, LANES), lambda c, k: (0, c))],
        out_specs=[seq, sblk],
        out_shape=[jax.ShapeDtypeStruct((nch, t, nb, d), F32),
                   jax.ShapeDtypeStruct((nblk, nb, ncol), F32)],
        scratch_shapes=[pltpu.VMEM((m, ncol), F32), pltpu.VMEM((m, ncol), F32), pltpu.VMEM((m, ncol), F32),
                        pltpu.VMEM((nb, ncol), F32)],
        compiler_params=_params("parallel", "arbitrary"),
        name="ssm_core",
    )(h4, bdv, ws, wc, at, s0, d_skip.reshape(1, d))
    return y, sf


def _ssm_expand_body(place_ref, wsc_ref, wcc_ref, kcp_ref, ws_ref, wc_ref, bdv_ref):
    gpb = place_ref.shape[0]
    ws_ref[0] = jnp.concatenate(
        [_dot(place_ref[h], wsc_ref[0, h]) for h in range(gpb)], axis=1).astype(BF16)
    wc_ref[0] = jnp.concatenate(
        [_dot_nt(wcc_ref[0, h], place_ref[h]) for h in range(gpb)], axis=0).astype(BF16)
    bdv = _dot(place_ref[0], kcp_ref[0, 0])
    for h in range(1, gpb):
        bdv = bdv + _dot(place_ref[h], kcp_ref[0, h])
    bdv_ref[0] = bdv.astype(BF16)


def ssm_expand(place, wsc, wcc, kcp):
    nblk, gpb, tc, n = wsc.shape
    rows = place.shape[1]
    per_blk = lambda *s: pl.BlockSpec((1,) + s, lambda b: (b,) + (0,) * len(s))
    return pl.pallas_call(
        _ssm_expand_body,
        grid=(nblk,),
        in_specs=[pl.BlockSpec(place.shape, lambda b: (0, 0, 0)), per_blk(gpb, tc, n), per_blk(gpb, n, tc),
                  per_blk(gpb, tc, LANES)],
        out_specs=[per_blk(rows, gpb * n), per_blk(gpb * n, rows), per_blk(rows, LANES)],
        out_shape=[jax.ShapeDtypeStruct((nblk, rows, gpb * n), BF16),
                   jax.ShapeDtypeStruct((nblk, gpb * n, rows), BF16),
                   jax.ShapeDtypeStruct((nblk, rows, LANES), BF16)],
        compiler_params=_params("parallel"),
        name="ssm_expand",
    )(place, wsc, wcc, kcp)


def ssm_matrices(a_re, a_im, log_step, b_re, b_im, c_re, c_im):
    t = SSM_T
    g, p = a_re.shape
    c = b_re.shape[2]
    gpb = LANES // c
    nblk = g // gpb
    dt = jnp.exp(log_step)[:, None]
    lam_re, lam_im = a_re * dt, a_im * dt
    j = jnp.arange(t + 1, dtype=F32)[:, None, None]
    mag = jnp.exp(lam_re * j)
    pw_re, pw_im = mag * jnp.cos(lam_im * j), mag * jnp.sin(lam_im * j)
    ab_re, ab_im = pw_re[1], pw_im[1]
    den = a_re * a_re + a_im * a_im
    z_re = ((ab_re - 1.0) * a_re + ab_im * a_im) / den
    z_im = (ab_im * a_re - (ab_re - 1.0) * a_im) / den
    bb_re = z_re[..., None] * b_re - z_im[..., None] * b_im
    bb_im = z_re[..., None] * b_im + z_im[..., None] * b_re
    jr = (t - 1) - jnp.arange(t, dtype=F32)[:, None, None]
    mag_r = jnp.exp(lam_re * jr)
    rp_re, rp_im = mag_r * jnp.cos(lam_im * jr), mag_r * jnp.sin(lam_im * jr)
    ws_re = rp_re[:, :, :, None] * bb_re[None] - rp_im[:, :, :, None] * bb_im[None]
    ws_im = rp_re[:, :, :, None] * bb_im[None] + rp_im[:, :, :, None] * bb_re[None]
    wsc = jnp.concatenate([ws_re, ws_im], axis=2).transpose(1, 0, 3, 2).reshape(nblk, gpb, t * c, 2 * p)

    def out_weights(q_re, q_im):
        cr, ci = c_re[None], c_im[None]
        qr, qi = q_re[:, :, None, :], q_im[:, :, None, :]
        return cr * qr - ci * qi, -cr * qi - ci * qr

    wc_re, wc_im = out_weights(pw_re[1:], pw_im[1:])
    wcc = jnp.concatenate([wc_re, wc_im], axis=3).transpose(1, 3, 0, 2).reshape(nblk, gpb, 2 * p, t * c)

    ca_re, ca_im = out_weights(rp_re, rp_im)
    ca = jnp.concatenate([ca_re, ca_im], axis=-1).transpose(1, 0, 2, 3).reshape(g, t * c, 2 * p)
    bb = jnp.concatenate([bb_re, bb_im], axis=1)
    kmat = ssm_lag_kernels(ca, bb, gpb).reshape(g, t, c, c)
    kc = kmat.transpose(0, 1, 3, 2).reshape(nblk, gpb, t * c, c)

    place = np.zeros((gpb, t, gpb, c, t, c), np.float32)
    for h in range(gpb):
        place[h, :, h] = np.eye(t * c, dtype=np.float32).reshape(t, c, t, c)
    place = jnp.asarray(place.reshape(gpb, t * LANES, t * c), BF16)
    eye = jnp.eye(gpb, dtype=F32)
    kcp = (kc[:, :, :, None, :] * eye[None, :, None, :, None]).reshape(nblk, gpb, t * c, LANES)
    ws, wc, bdv = ssm_expand(place, wsc.astype(BF16), wcc.astype(BF16), kcp.astype(BF16))

    at_re = pw_re[t].reshape(nblk, gpb, p)
    at_im = pw_im[t].reshape(nblk, gpb, p)
    a1 = jnp.concatenate([at_re, at_re], axis=2).reshape(nblk, gpb * 2 * p)
    a2 = jnp.concatenate([-at_im, at_im], axis=2).reshape(nblk, gpb * 2 * p)
    return bdv, ws, wc, jnp.stack([a1, a2], axis=1)


def ssm_pack_state(s_re, s_im, nblk):
    nb, g, p = s_re.shape
    s = jnp.stack([s_re.reshape(nb, nblk, g // nblk, p), s_im.reshape(nb, nblk, g // nblk, p)], axis=3)
    return s.transpose(1, 0, 2, 3, 4).reshape(nblk, nb, -1)


def ssm_unpack_state(sf, g, p):
    nblk, nb, _ = sf.shape
    s = sf.reshape(nblk, nb, g // nblk, 2, p).transpose(1, 0, 2, 3, 4).reshape(nb, g, 2, p)
    return s[:, :, 0], s[:, :, 1]


def _split3(x):
    hi = x.astype(BF16)
    r = x - hi.astype(F32)
    mid = r.astype(BF16)
    lo = (r - mid.astype(F32)).astype(BF16)
    return hi, mid, lo


def _cumsum_body(x_ref, o_ref):
    nblk = x_ref.shape[2] // LANES
    rows = x_ref.shape[1]
    row = lax.broadcasted_iota(jnp.int32, (LANES, LANES), 0)
    col = lax.broadcasted_iota(jnp.int32, (LANES, LANES), 1)
    tri = jnp.where(row <= col, 1.0, 0.0).astype(BF16)
    carry = jnp.zeros((rows, 1), F32)
    for j in range(nblk):
        sl = slice(j * LANES, (j + 1) * LANES)
        hi, mid, lo = _split3(x_ref[0, :, sl])
        w = _dot(hi, tri) + _dot(mid, tri) + _dot(lo, tri)
        o_ref[0, :, sl] = w + carry
        carry = carry + jnp.sum(x_ref[0, :, sl], axis=1, keepdims=True)


def cumsum_time(x):
    n, h, length = x.shape
    return pl.pallas_call(
        _cumsum_body,
        grid=(n,),
        in_specs=[pl.BlockSpec((1, h, length), lambda i: (i, 0, 0))],
        out_specs=pl.BlockSpec((1, h, length), lambda i: (i, 0, 0)),
        out_shape=jax.ShapeDtypeStruct((n, h, length), F32),
        compiler_params=_params("parallel"),
        name="cumsum_time",
    )(x)


def _softmax_pv(parts):
    m = functools.reduce(jnp.maximum, [jnp.max(s, axis=-1, keepdims=True) for s, _ in parts])
    acc, den = None, None
    for s, v in parts:
        e = jnp.exp2(s - m)
        l = jnp.sum(e, axis=-1, keepdims=True)
        o = _dot(e.astype(BF16), v)
        acc = o if acc is None else acc + o
        den = l if den is None else den + l
    return acc / den


def _causal(n):
    rpos = lax.broadcasted_iota(jnp.int32, (n, n), 0)
    cpos = lax.broadcasted_iota(jnp.int32, (n, n), 1)
    return cpos <= rpos


def _fox_prompt_body(tq, hd, q_ref, k_ref, v_ref, c_ref, o_ref):
    length = q_ref.shape[0]
    keep = _causal(tq)
    for i in range(length // tq):
        lo, hi = i * tq, (i + 1) * tq
        for hh in range(q_ref.shape[1] // hd):
            cols = slice(hh * hd, (hh + 1) * hd)
            q = q_ref[lo:hi, cols]
            s = _dot_nt(q, k_ref[0:hi, cols]) - c_ref[hh, :, 0:hi]
            s_diag = jnp.where(keep, s[:, lo:hi], NEG_INF)
            m = jnp.max(s_diag, axis=-1, keepdims=True)
            if i > 0:
                m = jnp.maximum(m, jnp.max(s[:, 0:lo], axis=-1, keepdims=True))
                e = jnp.concatenate([jnp.exp2(s[:, 0:lo] - m), jnp.exp2(s_diag - m)], axis=1)
            else:
                e = jnp.exp2(s_diag - m)
            den = jnp.sum(e, axis=-1, keepdims=True)
            o_ref[lo:hi, cols] = (_dot(e.astype(BF16), v_ref[0:hi, cols]) / den).astype(o_ref.dtype)


def fox_prompt_attention(q, kb, vb, c, nb, nh):
    m, d = q.shape
    length = m // nb
    hd = d // nh
    tq = min(FOX_TQ, length)
    hps = min(ATTN_HEADS_PER_STEP, nh)
    ngrp = nh // hps
    seq = pl.BlockSpec((length, hps * hd), lambda b, h: (b, h))
    return pl.pallas_call(
        functools.partial(_fox_prompt_body, tq, hd),
        grid=(nb, ngrp),
        in_specs=[seq, seq, seq, pl.BlockSpec((hps, 1, length), lambda b, h: (b * ngrp + h, 0, 0))],
        out_specs=seq,
        out_shape=jax.ShapeDtypeStruct((m, d), BF16),
        compiler_params=_params("parallel", "parallel"),
        name="fox_prompt_attention",
    )(q, kb, vb, c)


def _fox_sample_body(nh, q_ref, kc_ref, vc_ref, kn_ref, vn_ref, cc_ref, cn_ref, o_ref,
                     m_ref, l_ref, acc_ref):
    j = pl.program_id(1)
    lq, d = q_ref.shape
    hd = d // nh

    @pl.when(j == 0)
    def _():
        m_ref[...] = jnp.full_like(m_ref, NEG_INF)
        l_ref[...] = jnp.zeros_like(l_ref)
        acc_ref[...] = jnp.zeros_like(acc_ref)

    heads = lambda ref: [ref[:, h * hd:(h + 1) * hd] for h in range(nh)]
    qs = heads(q_ref)

    def update(s, vs):
        m_old = m_ref[...]
        m_new = jnp.maximum(m_old, jnp.max(s, axis=-1, keepdims=True))
        alpha = jnp.exp2(m_old - m_new)
        e = jnp.exp2(s - m_new)
        l_ref[...] = alpha * l_ref[...] + jnp.sum(e, axis=-1, keepdims=True)
        e = e.astype(BF16)
        pv = jnp.concatenate([_dot(e[h * lq:(h + 1) * lq], vs[h]) for h in range(nh)], axis=0)
        acc_ref[...] = alpha * acc_ref[...] + pv
        m_ref[...] = m_new

    k_all = pltpu.einshape("lhd->hld", kc_ref[...])
    v_all = pltpu.einshape("lhd->hld", vc_ref[...])
    s = jnp.concatenate([_dot_nt(qs[h], k_all[h].astype(BF16)) - cc_ref[0, h:h + 1, :]
                         for h in range(nh)], axis=0)
    update(s, [v_all[h].astype(BF16) for h in range(nh)])

    @pl.when(j == pl.num_programs(1) - 1)
    def _():
        keep = _causal(lq)
        kn = heads(kn_ref)
        s = jnp.concatenate(
            [jnp.where(keep, _dot_nt(qs[h], kn[h]) - cn_ref[0, h:h + 1, 0:lq], NEG_INF)
             for h in range(nh)], axis=0)
        update(s, heads(vn_ref))
        o = acc_ref[...] / l_ref[...]
        for h in range(nh):
            o_ref[:, h * hd:(h + 1) * hd] = o[h * lq:(h + 1) * lq].astype(o_ref.dtype)


def fox_sample_attention(q, k_cache, v_cache, layer, kb, vb, c, nb, nh):
    m, d = q.shape
    lq = m // nb
    past, hd = k_cache.shape[2], k_cache.shape[4]
    tk = min(FOX_TK, past)
    nkv = past // tk
    new = pl.BlockSpec((lq, d), lambda b, j: (b, 0))
    cache = pl.BlockSpec((None, None, tk, nh, hd), lambda b, j: (layer, b, j, 0, 0))
    return pl.pallas_call(
        functools.partial(_fox_sample_body, nh),
        grid=(nb, nkv),
        in_specs=[new, cache, cache, new, new,
                  pl.BlockSpec((1, nh, tk), lambda b, j: (b, 0, j)),
                  pl.BlockSpec((1, nh, LANES), lambda b, j: (b, 0, past // LANES))],
        out_specs=new,
        out_shape=jax.ShapeDtypeStruct((m, d), BF16),
        scratch_shapes=[pltpu.VMEM((nh * lq, 1), F32), pltpu.VMEM((nh * lq, 1), F32),
                        pltpu.VMEM((nh * lq, hd), F32)],
        compiler_params=_params("parallel", "arbitrary"),
        name="fox_sample_attention",
    )(q, k_cache, v_cache, kb, vb, c, c)


def _band_prompt_body(tq, hd, q_ref, k_ref, v_ref, bias_ref, o_ref):
    length = q_ref.shape[0]
    span = bias_ref.shape[2]
    for i in range(length // tq):
        k0 = max((i + 1) * tq - span, 0)
        k1 = (i + 1) * tq
        for hh in range(q_ref.shape[1] // hd):
            cols = slice(hh * hd, (hh + 1) * hd)
            q = q_ref[i * tq:(i + 1) * tq, cols]
            s = _dot_nt(q, k_ref[k0:k1, cols]) + bias_ref[hh, :, span - (k1 - k0):span]
            o_ref[i * tq:(i + 1) * tq, cols] = _softmax_pv([(s, v_ref[k0:k1, cols])]).astype(o_ref.dtype)


def band_prompt_attention(q, kb, vb, bias, nb, nh):
    m, d = q.shape
    length = m // nb
    hd = d // nh
    tq = bias.shape[1]
    hps = min(ATTN_HEADS_PER_STEP, nh)
    seq = pl.BlockSpec((length, hps * hd), lambda b, h: (b, h))
    return pl.pallas_call(
        functools.partial(_band_prompt_body, tq, hd),
        grid=(nb, nh // hps),
        in_specs=[seq, seq, seq, pl.BlockSpec((hps,) + bias.shape[1:], lambda b, h: (h, 0, 0))],
        out_specs=seq,
        out_shape=jax.ShapeDtypeStruct((m, d), BF16),
        compiler_params=_params("parallel", "parallel"),
        name="band_prompt_attention",
    )(q, kb, vb, bias)


def _band_sample_body(nh, q_ref, kc_ref, vc_ref, kn_ref, vn_ref, bias_ref, o_ref):
    lq, d = q_ref.shape
    hd = d // nh
    lc = kc_ref.shape[0]
    k_all = pltpu.einshape("lhd->hld", kc_ref[...])
    v_all = pltpu.einshape("lhd->hld", vc_ref[...])
    heads = lambda ref: [ref[:, h * hd:(h + 1) * hd] for h in range(nh)]
    qs, kn, vn = heads(q_ref), heads(kn_ref), heads(vn_ref)
    s_c = jnp.concatenate([_dot_nt(qs[h], k_all[h].astype(BF16)) + bias_ref[h, :, 0:lc]
                           for h in range(nh)], axis=0)
    s_n = jnp.concatenate([_dot_nt(qs[h], kn[h]) + bias_ref[h, :, lc:lc + lq] for h in range(nh)], axis=0)
    m = jnp.maximum(jnp.max(s_c, axis=-1, keepdims=True), jnp.max(s_n, axis=-1, keepdims=True))
    e_c, e_n = jnp.exp2(s_c - m), jnp.exp2(s_n - m)
    inv = 1.0 / (jnp.sum(e_c, axis=-1, keepdims=True) + jnp.sum(e_n, axis=-1, keepdims=True))
    e_c, e_n = e_c.astype(BF16), e_n.astype(BF16)
    for h in range(nh):
        rows = slice(h * lq, (h + 1) * lq)
        o = _dot(e_c[rows], v_all[h].astype(BF16)) + _dot(e_n[rows], vn[h])
        o_ref[:, h * hd:(h + 1) * hd] = (o * inv[rows]).astype(o_ref.dtype)


def band_sample_attention(q, k_cache, v_cache, layer, kb, vb, bias, nb, nh):
    m, d = q.shape
    lq = m // nb
    lc, hd = k_cache.shape[2], k_cache.shape[4]
    new = pl.BlockSpec((lq, d), lambda b: (b, 0))
    cache = pl.BlockSpec((None, None, lc, nh, hd), lambda b: (layer, b, 0, 0, 0))
    return pl.pallas_call(
        functools.partial(_band_sample_body, nh),
        grid=(nb,),
        in_specs=[new, cache, cache, new, new, pl.BlockSpec(bias.shape, lambda b: (0, 0, 0))],
        out_specs=new,
        out_shape=jax.ShapeDtypeStruct((m, d), BF16),
        compiler_params=_params("parallel"),
        name="band_sample_attention",
    )(q, k_cache, v_cache, kb, vb, bias)


def band_bias_tile(rel_bias, tq, window):
    nh = rel_bias.shape[0]
    span = window + tq
    n = span + tq
    rel = (span - 1) - np.arange(n)
    diag = rel_bias.astype(F32)[:, np.clip(rel, -REL_CLIP, REL_CLIP) + REL_CLIP]
    skew = jnp.broadcast_to(diag[:, None, :], (nh, tq, n)).reshape(nh, tq * n)
    skew = skew[:, :tq * (n - 1)].reshape(nh, tq, n - 1)[:, :, tq - 1:tq - 1 + span]
    qpos = np.arange(tq)[:, None]
    kpos = np.arange(span)[None, :] - window
    qc, kc = qpos // CHUNK, kpos // CHUNK
    visible = (kc <= qc) & (kc >= qc - BAND_PREV)
    return jnp.where(visible[None], skew, NEG_INF)


def kernel(x_prompt, x_sample, state_ssm_re, state_ssm_im, cache_fox_k, cache_fox_v, cache_fox_logf,
           cache_band_k, cache_band_v, ffn1_norm, ffn1_w_in, ffn1_w_out, mix_norm, ffn2_norm, ffn2_w_in,
           ffn2_w_out, ssm_a_re, ssm_a_im, ssm_log_step, ssm_b_re, ssm_b_im, ssm_c_re, ssm_c_im, ssm_d,
           ssm_w_glu, fox_w_in, fox_b_f, fox_w_out, band_w_in, band_rel_bias, band_w_out, final_norm):
    bp, lp, d = x_prompt.shape
    bs, ls, _ = x_sample.shape
    depth = ffn1_norm.shape[0]
    nh = fox_b_f.shape[1]
    hd = d // nh
    scale = hd ** -0.5 * LOG2E
    window = cache_band_k.shape[2]
    heads = lambda t, nb, length: t.reshape(nb, length, nh, hd)

    xp = x_prompt.reshape(bp * lp, d)
    xs = x_sample.reshape(bs * ls, d)
    outs = {name: [] for name in (
        "p_ssm_re", "p_ssm_im", "p_fox_k", "p_fox_v", "p_fox_logf", "p_band_k", "p_band_v",
        "s_ssm_re", "s_ssm_im", "s_fox_k", "s_fox_v", "s_fox_logf", "s_band_k", "s_band_v")}

    w_glu = ssm_w_glu.astype(BF16)
    fox_in, fox_out = fox_w_in.astype(BF16), fox_w_out.astype(BF16)
    band_in, band_out = band_w_in.astype(BF16), band_w_out.astype(BF16)

    def ffn(xp, xs, g, w_in, w_out, layer):
        xs, w_gate, w_up, w_down = ffn_half_step_casting(xs, g, w_in, w_out, layer)
        return ffn_half_step(xp, g, w_gate, w_up, w_down), xs

    g, p = ssm_a_re.shape[1:]
    fold = lambda a: a.reshape((-1,) + a.shape[2:])
    mats = ssm_matrices(fold(ssm_a_re), fold(ssm_a_im), fold(ssm_log_step), fold(ssm_b_re), fold(ssm_b_im),
                        fold(ssm_c_re), fold(ssm_c_im))

    for i in range(depth):
        kind, j = i % 3, i // 3
        xp, xs = ffn(xp, xs, ffn1_norm[i], ffn1_w_in, ffn1_w_out, i)
        if kind == 0:
            nblk = d // LANES
            zeros = jnp.zeros((bp, g, p), F32)
            streams = ((xp, bp, lp, zeros, zeros, "p"), (xs, bs, ls, state_ssm_re[j], state_ssm_im[j], "s"))
            new_x = []
            for x, nb, length, s_re, s_im, tag in streams:
                h_tm = rmsnorm_time_major(x, mix_norm[i], nb, length)
                a_tm, sf = ssm_core(h_tm, mats, j, ssm_pack_state(s_re, s_im, nblk), ssm_d[j])
                f_re, f_im = ssm_unpack_state(sf, g, p)
                outs[tag + "_ssm_re"].append(f_re); outs[tag + "_ssm_im"].append(f_im)
                new_x.append(ssm_glu_out(x, a_tm, w_glu, j, nb, length))
            xp, xs = new_x
        elif kind == 1:
            wf_t = fox_w_in[j][:, 3 * d:].T.astype(BF16)
            qp, kp, vp, kbp, vbp, lfp = qkv_project(xp, mix_norm[i], fox_in, j, scale, wf_t, fox_b_f[j])
            qs, ks, vs, kbs, vbs, lfs = qkv_project(xs, mix_norm[i], fox_in, j, scale, wf_t, fox_b_f[j])
            lfp = lfp.reshape(nh, bp, lp).transpose(1, 0, 2)
            lfs = lfs.reshape(nh, bs, ls).transpose(1, 0, 2)
            cp = (cumsum_time(lfp) * LOG2E).reshape(bp * nh, 1, lp)
            lf_all = jnp.concatenate([cache_fox_logf[j].astype(F32).transpose(0, 2, 1), lfs,
                                      jnp.zeros((bs, nh, LANES - ls), F32)], axis=2)
            cs = cumsum_time(lf_all) * LOG2E
            op = fox_prompt_attention(qp, kbp, vbp, cp, bp, nh)
            os_ = fox_sample_attention(qs, cache_fox_k, cache_fox_v, j, kbs, vbs, cs, bs, nh)
            xp = out_project(xp, op, fox_out, j)
            xs = out_project(xs, os_, fox_out, j)
            outs["p_fox_k"].append(heads(kp, bp, lp)); outs["p_fox_v"].append(heads(vp, bp, lp))
            outs["p_fox_logf"].append(lfp.transpose(0, 2, 1))
            outs["s_fox_k"].append(heads(ks, bs, ls)); outs["s_fox_v"].append(heads(vs, bs, ls))
            outs["s_fox_logf"].append(lfs.transpose(0, 2, 1))
        else:
            qp, kp, vp, kbp, vbp = qkv_project(xp, mix_norm[i], band_in, j, scale)
            qs, ks, vs, kbs, vbs = qkv_project(xs, mix_norm[i], band_in, j, scale)
            bw = BAND_PREV * CHUNK
            bias = band_bias_tile(band_rel_bias[j], BAND_TQ, bw) * LOG2E
            op = band_prompt_attention(qp, kbp, vbp, bias, bp, nh)
            os_ = band_sample_attention(qs, cache_band_k, cache_band_v, j, kbs, vbs,
                                        bias[:, :ls, bw - window:bw + ls], bs, nh)
            xp = out_project(xp, op, band_out, j)
            xs = out_project(xs, os_, band_out, j)
            keep = min(bw, lp)
            outs["p_band_k"].append(heads(kp, bp, lp)[:, lp - keep:])
            outs["p_band_v"].append(heads(vp, bp, lp)[:, lp - keep:])
            outs["s_band_k"].append(heads(ks, bs, ls)); outs["s_band_v"].append(heads(vs, bs, ls))
        xp, xs = ffn(xp, xs, ffn2_norm[i], ffn2_w_in, ffn2_w_out, i)

    y_prompt = rmsnorm(xp, final_norm).reshape(bp, lp, d)
    y_sample = rmsnorm(xs, final_norm).reshape(bs, ls, d)
    st = {k: jnp.stack(v) for k, v in outs.items()}
    return (y_prompt, y_sample, st["p_ssm_re"], st["p_ssm_im"], st["p_fox_k"], st["p_fox_v"],
            st["p_fox_logf"], st["p_band_k"], st["p_band_v"], st["s_ssm_re"], st["s_ssm_im"],
            st["s_fox_k"], st["s_fox_v"], st["s_fox_logf"], st["s_band_k"], st["s_band_v"])
```

```python
import functools

import jax
import jax.numpy as jnp
import numpy as np
from jax import lax
from jax.experimental import pallas as pl
from jax.experimental.pallas import tpu as pltpu

F32 = jnp.float32
BF16 = jnp.bfloat16

EPS = 1e-6
NEG_INF = -1e30
LOG2E = 1.4426950408889634
CHUNK = 64
BAND_PREV = 8
REL_CLIP = 256
SSM_GROUP = 16
SSM_T = 8
LANES = 128
VMEM_LIMIT = 56 * 1024 * 1024

TM = 512
TM_FFN = 1024
TF = 512
TF_CAST = 256
TM_QKV = 1024
TN = 512
TN_GLU = 1024
BAND_TQ = 256
FOX_TQ = 256
ATTN_HEADS_PER_STEP = 4
FOX_TK = 512
SSM_ROWS = 512


def _params(*sem):
    return pltpu.CompilerParams(dimension_semantics=sem, vmem_limit_bytes=VMEM_LIMIT)


def _rms(x, g):
    return x * lax.rsqrt(jnp.mean(x * x, axis=-1, keepdims=True) + EPS) * g


def _dot(a, b):
    return jnp.dot(a, b, preferred_element_type=F32)


def _dot_nt(a, b):
    return lax.dot_general(a, b, (((1,), (1,)), ((), ())), preferred_element_type=F32)


def _ffn_body(x_ref, g_ref, wg_ref, wu_ref, wo_ref, o_ref, h_ref):
    f = pl.program_id(1)

    @pl.when(f == 0)
    def _():
        x = x_ref[...]
        h_ref[...] = _rms(x, g_ref[...]).astype(BF16)
        o_ref[...] = x

    h = h_ref[...]
    gate = _dot(h, wg_ref[...])
    up = _dot(h, wu_ref[...])
    act = (gate * jax.nn.sigmoid(gate) * up).astype(BF16)
    o_ref[...] += _dot(act, wo_ref[...])


def ffn_half_step(x, g, w_gate, w_up, w_out):
    m, d = x.shape
    f = w_out.shape[0]
    tm = min(TM_FFN, m)
    tf = min(TF, f)
    return pl.pallas_call(
        _ffn_body,
        grid=(m // tm, f // tf),
        in_specs=[
            pl.BlockSpec((tm, d), lambda i, j: (i, 0)),
            pl.BlockSpec((1, d), lambda i, j: (0, 0)),
            pl.BlockSpec((d, tf), lambda i, j: (0, j)),
            pl.BlockSpec((d, tf), lambda i, j: (0, j)),
            pl.BlockSpec((tf, d), lambda i, j: (j, 0)),
        ],
        out_specs=pl.BlockSpec((tm, d), lambda i, j: (i, 0)),
        out_shape=jax.ShapeDtypeStruct((m, d), F32),
        scratch_shapes=[pltpu.VMEM((tm, d), BF16)],
        compiler_params=_params("parallel", "arbitrary"),
        name="ffn_half_step",
    )(x, g.reshape(1, d), w_gate, w_up, w_out)


def _ffn_cast_body(x_ref, g_ref, wg32_ref, wu32_ref, wo32_ref, o_ref, wg_ref, wu_ref, wo_ref, h_ref):
    wg_ref[...] = wg32_ref[...].astype(BF16)
    wu_ref[...] = wu32_ref[...].astype(BF16)
    wo_ref[...] = (0.5 * wo32_ref[...]).astype(BF16)
    _ffn_body(x_ref, g_ref, wg_ref, wu_ref, wo_ref, o_ref, h_ref)


def ffn_half_step_casting(x, g, w_in, w_out, layer):
    m, d = x.shape
    f = w_out.shape[1]
    tf = min(TF_CAST, f)
    nf = f // tf
    once = dict(pipeline_mode=pl.Buffered(1))
    col = pl.BlockSpec((d, tf), lambda i, j: (0, j))
    return pl.pallas_call(
        _ffn_cast_body,
        grid=(1, nf),
        in_specs=[
            pl.BlockSpec((m, d), lambda i, j: (0, 0), **once),
            pl.BlockSpec((1, d), lambda i, j: (0, 0)),
            pl.BlockSpec((None, d, tf), lambda i, j: (layer, 0, j)),
            pl.BlockSpec((None, d, tf), lambda i, j: (layer, 0, j + nf)),
            pl.BlockSpec((None, tf, d), lambda i, j: (layer, j, 0)),
        ],
        out_specs=[pl.BlockSpec((m, d), lambda i, j: (0, 0)), col, col,
                   pl.BlockSpec((tf, d), lambda i, j: (j, 0))],
        out_shape=[jax.ShapeDtypeStruct((m, d), F32), jax.ShapeDtypeStruct((d, f), BF16),
                   jax.ShapeDtypeStruct((d, f), BF16), jax.ShapeDtypeStruct((f, d), BF16)],
        scratch_shapes=[pltpu.VMEM((m, d), BF16)],
        compiler_params=_params("arbitrary", "arbitrary"),
        name="ffn_half_step_casting",
    )(x, g.reshape(1, d), w_in, w_in, w_out)


def _rmsnorm_body(x_ref, g_ref, o_ref):
    o_ref[...] = _rms(x_ref[...], g_ref[...]).astype(o_ref.dtype)


def rmsnorm(x, g):
    m, d = x.shape
    tm = min(TM, m)
    return pl.pallas_call(
        _rmsnorm_body,
        grid=(m // tm,),
        in_specs=[pl.BlockSpec((tm, d), lambda i: (i, 0)),
                  pl.BlockSpec((1, d), lambda i: (0, 0))],
        out_specs=pl.BlockSpec((tm, d), lambda i: (i, 0)),
        out_shape=jax.ShapeDtypeStruct((m, d), F32),
        compiler_params=_params("parallel"),
        name="rmsnorm",
    )(x, g.reshape(1, d))


def _rmsnorm_tm_body(x_ref, g_ref, o_ref):
    h = _rms(x_ref[...], g_ref[...])
    o_ref[...] = pltpu.einshape("bld->lbd", h).reshape(o_ref.shape)


def rmsnorm_time_major(x, g, nb, length):
    d = x.shape[1]
    t = SSM_T
    tl = min(TM // nb, length)
    return pl.pallas_call(
        _rmsnorm_tm_body,
        grid=(length // tl,),
        in_specs=[pl.BlockSpec((nb, tl, d), lambda l: (0, l, 0)),
                  pl.BlockSpec((1, d), lambda l: (0, 0))],
        out_specs=pl.BlockSpec((tl // t, t, nb, d), lambda l: (l, 0, 0, 0)),
        out_shape=jax.ShapeDtypeStruct((length // t, t, nb, d), F32),
        compiler_params=_params("parallel"),
        name="rmsnorm_time_major",
    )(x.reshape(nb, length, d), g.reshape(1, d))


def _qkv_body(scale, has_forget, *refs):
    if has_forget:
        (x_ref, g_ref, wq_ref, wk_ref, wv_ref, wf_ref, bf_ref,
         q_ref, k32_ref, v32_ref, kb_ref, vb_ref, lf_ref, h_ref) = refs
    else:
        (x_ref, g_ref, wq_ref, wk_ref, wv_ref,
         q_ref, k32_ref, v32_ref, kb_ref, vb_ref, h_ref) = refs

    @pl.when(pl.program_id(1) == 0)
    def _():
        h = _rms(x_ref[...], g_ref[...]).astype(BF16)
        h_ref[...] = h
        if has_forget:
            z = _dot_nt(wf_ref[...], h) + bf_ref[...]
            lf_ref[...] = jnp.minimum(z, 0.0) - jnp.log(1.0 + jnp.exp(-jnp.abs(z)))

    h = h_ref[...]
    q_ref[...] = (_dot(h, wq_ref[...]) * scale).astype(BF16)
    k = _dot(h, wk_ref[...])
    k32_ref[...] = k
    kb_ref[...] = k.astype(BF16)
    v = _dot(h, wv_ref[...])
    v32_ref[...] = v
    vb_ref[...] = v.astype(BF16)


def qkv_project(x, g, w, layer, scale, wf_t=None, b_f=None):
    m, d = x.shape
    tm = min(TM_QKV, m)
    tn = min(TN, d)
    nq = d // tn
    has_forget = wf_t is not None
    in_specs = [
        pl.BlockSpec((tm, d), lambda i, n: (i, 0)),
        pl.BlockSpec((1, d), lambda i, n: (0, 0)),
        pl.BlockSpec((None, d, tn), lambda i, n: (layer, 0, n)),
        pl.BlockSpec((None, d, tn), lambda i, n: (layer, 0, n + nq)),
        pl.BlockSpec((None, d, tn), lambda i, n: (layer, 0, n + 2 * nq)),
    ]
    args = [x, g.reshape(1, d), w, w, w]
    tile = pl.BlockSpec((tm, tn), lambda i, n: (i, n))
    out_specs = [tile] * 5
    out_shape = [jax.ShapeDtypeStruct((m, d), BF16),
                 jax.ShapeDtypeStruct((m, d), F32), jax.ShapeDtypeStruct((m, d), F32),
                 jax.ShapeDtypeStruct((m, d), BF16), jax.ShapeDtypeStruct((m, d), BF16)]
    if has_forget:
        nh = wf_t.shape[0]
        in_specs += [pl.BlockSpec((nh, d), lambda i, n: (0, 0)),
                     pl.BlockSpec((nh, 1), lambda i, n: (0, 0))]
        args += [wf_t, b_f.reshape(nh, 1)]
        out_specs = out_specs + [pl.BlockSpec((nh, tm), lambda i, n: (0, i))]
        out_shape = out_shape + [jax.ShapeDtypeStruct((nh, m), F32)]
    return pl.pallas_call(
        functools.partial(_qkv_body, scale, has_forget),
        grid=(m // tm, nq),
        in_specs=in_specs,
        out_specs=out_specs,
        out_shape=out_shape,
        scratch_shapes=[pltpu.VMEM((tm, d), BF16)],
        compiler_params=_params("parallel", "arbitrary"),
        name="qkv_project",
    )(*args)


def _out_proj_body(x_ref, o_ref, w_ref, y_ref):
    y_ref[...] = x_ref[...] + _dot(o_ref[...], w_ref[...])


def out_project(x, o, w, layer):
    m, d = x.shape
    tm = min(TM, m)
    return pl.pallas_call(
        _out_proj_body,
        grid=(m // tm,),
        in_specs=[pl.BlockSpec((tm, d), lambda i: (i, 0)),
                  pl.BlockSpec((tm, d), lambda i: (i, 0)),
                  pl.BlockSpec((None, d, d), lambda i: (layer, 0, 0))],
        out_specs=pl.BlockSpec((tm, d), lambda i: (i, 0)),
        out_shape=jax.ShapeDtypeStruct((m, d), F32),
        compiler_params=_params("parallel"),
        name="out_project",
    )(x, o, w)


def _glu_body(a_ref, xt_ref, wv_ref, wg_ref, o_ref):
    nb, tl, tn = xt_ref.shape
    a = a_ref[...].reshape(tl * nb, a_ref.shape[3]).astype(BF16)
    val = _dot(a, wv_ref[...])
    gate = _dot(a, wg_ref[...])
    r = (val * jax.nn.sigmoid(gate)).reshape(tl, nb, tn)
    o_ref[...] = xt_ref[...] + pltpu.einshape("lbn->bln", r)


def ssm_glu_out(x, a4, w_glu, layer, nb, length):
    d = x.shape[1]
    t = SSM_T
    tl = min(TM // nb, length)
    tn = min(TN_GLU, d)
    nq = d // tn
    tile = pl.BlockSpec((nb, tl, tn), lambda n, l: (0, l, n))
    return pl.pallas_call(
        _glu_body,
        grid=(nq, length // tl),
        in_specs=[
            pl.BlockSpec((tl // t, t, nb, d), lambda n, l: (l, 0, 0, 0)),
            tile,
            pl.BlockSpec((None, d, tn), lambda n, l: (layer, 0, n)),
            pl.BlockSpec((None, d, tn), lambda n, l: (layer, 0, n + nq)),
        ],
        out_specs=tile,
        out_shape=jax.ShapeDtypeStruct((nb, length, d), F32),
        compiler_params=_params("parallel", "parallel"),
        name="ssm_glu_out",
    )(a4, x.reshape(nb, length, d), w_glu, w_glu).reshape(nb * length, d)


def _ssm_lag_body(ca_ref, bb_ref, k_ref):
    for i in range(ca_ref.shape[0]):
        k_ref[i] = jnp.dot(ca_ref[i], bb_ref[i], preferred_element_type=F32,
                           precision=lax.Precision.HIGHEST)


def ssm_lag_kernels(ca, bb, gpb):
    g, r, p2 = ca.shape
    c = bb.shape[2]
    return pl.pallas_call(
        _ssm_lag_body,
        grid=(g // gpb,),
        in_specs=[pl.BlockSpec((gpb, r, p2), lambda i: (i, 0, 0)),
                  pl.BlockSpec((gpb, p2, c), lambda i: (i, 0, 0))],
        out_specs=pl.BlockSpec((gpb, r, c), lambda i: (i, 0, 0)),
        out_shape=jax.ShapeDtypeStruct((g, r, c), F32),
        compiler_params=_params("parallel"),
        name="ssm_lag_kernels",
    )(ca, bb)


def _ssm_core_body(nb, x_ref, bdv_ref, ws_ref, wc_ref, at_ref, s0_ref, dsk_ref, y_ref, sf_ref,
                   ds_ref, dsw_ref, sst_ref, st_ref):
    nchp, t = x_ref.shape[0], x_ref.shape[1]
    m = nchp * nb
    ncol = at_ref.shape[2]
    half = LANES // 2

    def swap_halves(v):
        lane = lax.broadcasted_iota(jnp.int32, v.shape, 1)
        return jnp.where(lane % LANES < half, pltpu.roll(v, ncol - half, 1), pltpu.roll(v, half, 1))

    @pl.when(pl.program_id(1) == 0)
    def _():
        st_ref[...] = s0_ref[0]

    xcat = jnp.concatenate(
        [x_ref[:, tau].reshape(m, LANES).astype(BF16) for tau in range(t)], axis=1)
    ds = _dot(xcat, ws_ref[0])
    ds_ref[...] = ds
    dsw_ref[...] = swap_halves(ds)
    a1 = jnp.broadcast_to(at_ref[0, 0:1, :], (nb, ncol))
    a2 = jnp.broadcast_to(at_ref[0, 1:2, :], (nb, ncol))

    def step(k, carry):
        s, sw = carry
        rows = pl.ds(pl.multiple_of(k * nb, nb), nb)
        sst_ref[rows, :] = s
        ns = a1 * s + a2 * sw + ds_ref[rows, :]
        nsw = a1 * sw - a2 * s + dsw_ref[rows, :]
        return ns, nsw

    s0 = st_ref[...]
    s, _ = lax.fori_loop(0, nchp, step, (s0, swap_halves(s0)))
    st_ref[...] = s
    sf_ref[0] = s
    y_in = _dot(sst_ref[...].astype(BF16), wc_ref[0])
    dsk = dsk_ref[...]
    for tt in range(t):
        yt = y_in[:, tt * LANES:(tt + 1) * LANES] + _dot(
            xcat[:, :(tt + 1) * LANES], bdv_ref[0, (t - 1 - tt) * LANES:, :])
        yt = jax.nn.gelu(yt + dsk * x_ref[:, tt].reshape(m, LANES))
        y_ref[:, tt] = yt.reshape(nchp, nb, LANES)


def ssm_core(h4, mats, layer, s0, d_skip):
    bdv, ws, wc, at = mats
    nch, t, nb, d = h4.shape
    nblk = d // LANES
    nchp = max(min(SSM_ROWS // nb, nch), 1)
    m = nchp * nb
    ncol = ws.shape[2]
    blk0 = layer * nblk
    wblk = lambda *s: pl.BlockSpec((1,) + s, lambda c, k: (c + blk0, 0, 0))
    sblk = pl.BlockSpec((1, nb, ncol), lambda c, k: (c, 0, 0))
    seq = pl.BlockSpec((nchp, t, nb, LANES), lambda c, k: (k, 0, 0, c))
    y, sf = pl.pallas_call(
        functools.partial(_ssm_core_body, nb),
        grid=(nblk, nch // nchp),
        in_specs=[seq, wblk(t * LANES, LANES), wblk(t * LANES, ncol), wblk(ncol, t * LANES),
                  wblk(2, ncol), sblk, pl.BlockSpec((1, LANES), lambda c, k: (0, c))],
        out_specs=[seq, sblk],
        out_shape=[jax.ShapeDtypeStruct((nch, t, nb, d), F32),
                   jax.ShapeDtypeStruct((nblk, nb, ncol), F32)],
        scratch_shapes=[pltpu.VMEM((m, ncol), F32), pltpu.VMEM((m, ncol), F32), pltpu.VMEM((m, ncol), F32),
                        pltpu.VMEM((nb, ncol), F32)],
        compiler_params=_params("parallel", "arbitrary"),
        name="ssm_core",
    )(h4, bdv, ws, wc, at, s0, d_skip.reshape(1, d))
    return y, sf


def _ssm_expand_body(place_ref, wsc_ref, wcc_ref, kcp_ref, ws_ref, wc_ref, bdv_ref):
    gpb = place_ref.shape[0]
    ws_ref[0] = jnp.concatenate(
        [_dot(place_ref[h], wsc_ref[0, h]) for h in range(gpb)], axis=1).astype(BF16)
    wc_ref[0] = jnp.concatenate(
        [_dot_nt(wcc_ref[0, h], place_ref[h]) for h in range(gpb)], axis=0).astype(BF16)
    bdv = _dot(place_ref[0], kcp_ref[0, 0])
    for h in range(1, gpb):
        bdv = bdv + _dot(place_ref[h], kcp_ref[0, h])
    bdv_ref[0] = bdv.astype(BF16)


def ssm_expand(place, wsc, wcc, kcp):
    nblk, gpb, tc, n = wsc.shape
    rows = place.shape[1]
    per_blk = lambda *s: pl.BlockSpec((1,) + s, lambda b: (b,) + (0,) * len(s))
    return pl.pallas_call(
        _ssm_expand_body,
        grid=(nblk,),
        in_specs=[pl.BlockSpec(place.shape, lambda b: (0, 0, 0)), per_blk(gpb, tc, n), per_blk(gpb, n, tc),
                  per_blk(gpb, tc, LANES)],
        out_specs=[per_blk(rows, gpb * n), per_blk(gpb * n, rows), per_blk(rows, LANES)],
        out_shape=[jax.ShapeDtypeStruct((nblk, rows, gpb * n), BF16),
                   jax.ShapeDtypeStruct((nblk, gpb * n, rows), BF16),
                   jax.ShapeDtypeStruct((nblk, rows, LANES), BF16)],
        compiler_params=_params("parallel"),
        name="ssm_expand",
    )(place, wsc, wcc, kcp)


def ssm_matrices(a_re, a_im, log_step, b_re, b_im, c_re, c_im):
    t = SSM_T
    g, p = a_re.shape
    c = b_re.shape[2]
    gpb = LANES // c
    nblk = g // gpb
    dt = jnp.exp(log_step)[:, None]
    lam_re, lam_im = a_re * dt, a_im * dt
    j = jnp.arange(t + 1, dtype=F32)[:, None, None]
    mag = jnp.exp(lam_re * j)
    pw_re, pw_im = mag * jnp.cos(lam_im * j), mag * jnp.sin(lam_im * j)
    ab_re, ab_im = pw_re[1], pw_im[1]
    den = a_re * a_re + a_im * a_im
    z_re = ((ab_re - 1.0) * a_re + ab_im * a_im) / den
    z_im = (ab_im * a_re - (ab_re - 1.0) * a_im) / den
    bb_re = z_re[..., None] * b_re - z_im[..., None] * b_im
    bb_im = z_re[..., None] * b_im + z_im[..., None] * b_re
    jr = (t - 1) - jnp.arange(t, dtype=F32)[:, None, None]
    mag_r = jnp.exp(lam_re * jr)
    rp_re, rp_im = mag_r * jnp.cos(lam_im * jr), mag_r * jnp.sin(lam_im * jr)
    ws_re = rp_re[:, :, :, None] * bb_re[None] - rp_im[:, :, :, None] * bb_im[None]
    ws_im = rp_re[:, :, :, None] * bb_im[None] + rp_im[:, :, :, None] * bb_re[None]
    wsc = jnp.concatenate([ws_re, ws_im], axis=2).transpose(1, 0, 3, 2).reshape(nblk, gpb, t * c, 2 * p)

    def out_weights(q_re, q_im):
        cr, ci = c_re[None], c_im[None]
        qr, qi = q_re[:, :, None, :], q_im[:, :, None, :]
        return cr * qr - ci * qi, -cr * qi - ci * qr

    wc_re, wc_im = out_weights(pw_re[1:], pw_im[1:])
    wcc = jnp.concatenate([wc_re, wc_im], axis=3).transpose(1, 3, 0, 2).reshape(nblk, gpb, 2 * p, t * c)

    ca_re, ca_im = out_weights(rp_re, rp_im)
    ca = jnp.concatenate([ca_re, ca_im], axis=-1).transpose(1, 0, 2, 3).reshape(g, t * c, 2 * p)
    bb = jnp.concatenate([bb_re, bb_im], axis=1)
    kmat = ssm_lag_kernels(ca, bb, gpb).reshape(g, t, c, c)
    kc = kmat.transpose(0, 1, 3, 2).reshape(nblk, gpb, t * c, c)

    place = np.zeros((gpb, t, gpb, c, t, c), np.float32)
    for h in range(gpb):
        place[h, :, h] = np.eye(t * c, dtype=np.float32).reshape(t, c, t, c)
    place = jnp.asarray(place.reshape(gpb, t * LANES, t * c), BF16)
    eye = jnp.eye(gpb, dtype=F32)
    kcp = (kc[:, :, :, None, :] * eye[None, :, None, :, None]).reshape(nblk, gpb, t * c, LANES)
    ws, wc, bdv = ssm_expand(place, wsc.astype(BF16), wcc.astype(BF16), kcp.astype(BF16))

    at_re = pw_re[t].reshape(nblk, gpb, p)
    at_im = pw_im[t].reshape(nblk, gpb, p)
    a1 = jnp.concatenate([at_re, at_re], axis=2).reshape(nblk, gpb * 2 * p)
    a2 = jnp.concatenate([-at_im, at_im], axis=2).reshape(nblk, gpb * 2 * p)
    return bdv, ws, wc, jnp.stack([a1, a2], axis=1)


def ssm_pack_state(s_re, s_im, nblk):
    nb, g, p = s_re.shape
    s = jnp.stack([s_re.reshape(nb, nblk, g // nblk, p), s_im.reshape(nb, nblk, g // nblk, p)], axis=3)
    return s.transpose(1, 0, 2, 3, 4).reshape(nblk, nb, -1)


def ssm_unpack_state(sf, g, p):
    nblk, nb, _ = sf.shape
    s = sf.reshape(nblk, nb, g // nblk, 2, p).transpose(1, 0, 2, 3, 4).reshape(nb, g, 2, p)
    return s[:, :, 0], s[:, :, 1]


def _split3(x):
    hi = x.astype(BF16)
    r = x - hi.astype(F32)
    mid = r.astype(BF16)
    lo = (r - mid.astype(F32)).astype(BF16)
    return hi, mid, lo


def _cumsum_body(x_ref, o_ref):
    nblk = x_ref.shape[2] // LANES
    rows = x_ref.shape[1]
    row = lax.broadcasted_iota(jnp.int32, (LANES, LANES), 0)
    col = lax.broadcasted_iota(jnp.int32, (LANES, LANES), 1)
    tri = jnp.where(row <= col, 1.0, 0.0).astype(BF16)
    carry = jnp.zeros((rows, 1), F32)
    for j in range(nblk):
        sl = slice(j * LANES, (j + 1) * LANES)
        hi, mid, lo = _split3(x_ref[0, :, sl])
        w = _dot(hi, tri) + _dot(mid, tri) + _dot(lo, tri)
        o_ref[0, :, sl] = w + carry
        carry = carry + jnp.sum(x_ref[0, :, sl], axis=1, keepdims=True)


def cumsum_time(x):
    n, h, length = x.shape
    return pl.pallas_call(
        _cumsum_body,
        grid=(n,),
        in_specs=[pl.BlockSpec((1, h, length), lambda i: (i, 0, 0))],
        out_specs=pl.BlockSpec((1, h, length), lambda i: (i, 0, 0)),
        out_shape=jax.ShapeDtypeStruct((n, h, length), F32),
        compiler_params=_params("parallel"),
        name="cumsum_time",
    )(x)


def _softmax_pv(parts):
    m = functools.reduce(jnp.maximum, [jnp.max(s, axis=-1, keepdims=True) for s, _ in parts])
    acc, den = None, None
    for s, v in parts:
        e = jnp.exp2(s - m)
        l = jnp.sum(e, axis=-1, keepdims=True)
        o = _dot(e.astype(BF16), v)
        acc = o if acc is None else acc + o
        den = l if den is None else den + l
    return acc / den


def _causal(n):
    rpos = lax.broadcasted_iota(jnp.int32, (n, n), 0)
    cpos = lax.broadcasted_iota(jnp.int32, (n, n), 1)
    return cpos <= rpos


def _fox_prompt_body(tq, hd, q_ref, k_ref, v_ref, c_ref, o_ref):
    length = q_ref.shape[0]
    keep = _causal(tq)
    for i in range(length // tq):
        lo, hi = i * tq, (i + 1) * tq
        for hh in range(q_ref.shape[1] // hd):
            cols = slice(hh * hd, (hh + 1) * hd)
            q = q_ref[lo:hi, cols]
            s = _dot_nt(q, k_ref[0:hi, cols]) - c_ref[hh, :, 0:hi]
            s_diag = jnp.where(keep, s[:, lo:hi], NEG_INF)
            m = jnp.max(s_diag, axis=-1, keepdims=True)
            if i > 0:
                m = jnp.maximum(m, jnp.max(s[:, 0:lo], axis=-1, keepdims=True))
                e = jnp.concatenate([jnp.exp2(s[:, 0:lo] - m), jnp.exp2(s_diag - m)], axis=1)
            else:
                e = jnp.exp2(s_diag - m)
            den = jnp.sum(e, axis=-1, keepdims=True)
            o_ref[lo:hi, cols] = (_dot(e.astype(BF16), v_ref[0:hi, cols]) / den).astype(o_ref.dtype)


def fox_prompt_attention(q, kb, vb, c, nb, nh):
    m, d = q.shape
    length = m // nb
    hd = d // nh
    tq = min(FOX_TQ, length)
    hps = min(ATTN_HEADS_PER_STEP, nh)
    ngrp = nh // hps
    seq = pl.BlockSpec((length, hps * hd), lambda b, h: (b, h))
    return pl.pallas_call(
        functools.partial(_fox_prompt_body, tq, hd),
        grid=(nb, ngrp),
        in_specs=[seq, seq, seq, pl.BlockSpec((hps, 1, length), lambda b, h: (b * ngrp + h, 0, 0))],
        out_specs=seq,
        out_shape=jax.ShapeDtypeStruct((m, d), BF16),
        compiler_params=_params("parallel", "parallel"),
        name="fox_prompt_attention",
    )(q, kb, vb, c)


def _fox_sample_body(nh, q_ref, kc_ref, vc_ref, kn_ref, vn_ref, cc_ref, cn_ref, o_ref,
                     m_ref, l_ref, acc_ref):
    j = pl.program_id(1)
    lq, d = q_ref.shape
    hd = d // nh

    @pl.when(j == 0)
    def _():
        m_ref[...] = jnp.full_like(m_ref, NEG_INF)
        l_ref[...] = jnp.zeros_like(l_ref)
        acc_ref[...] = jnp.zeros_like(acc_ref)

    heads = lambda ref: [ref[:, h * hd:(h + 1) * hd] for h in range(nh)]
    qs = heads(q_ref)

    def update(s, vs):
        m_old = m_ref[...]
        m_new = jnp.maximum(m_old, jnp.max(s, axis=-1, keepdims=True))
        alpha = jnp.exp2(m_old - m_new)
        e = jnp.exp2(s - m_new)
        l_ref[...] = alpha * l_ref[...] + jnp.sum(e, axis=-1, keepdims=True)
        e = e.astype(BF16)
        pv = jnp.concatenate([_dot(e[h * lq:(h + 1) * lq], vs[h]) for h in range(nh)], axis=0)
        acc_ref[...] = alpha * acc_ref[...] + pv
        m_ref[...] = m_new

    k_all = pltpu.einshape("lhd->hld", kc_ref[...])
    v_all = pltpu.einshape("lhd->hld", vc_ref[...])
    s = jnp.concatenate([_dot_nt(qs[h], k_all[h].astype(BF16)) - cc_ref[0, h:h + 1, :]
                         for h in range(nh)], axis=0)
    update(s, [v_all[h].astype(BF16) for h in range(nh)])

    @pl.when(j == pl.num_programs(1) - 1)
    def _():
        keep = _causal(lq)
        kn = heads(kn_ref)
        s = jnp.concatenate(
            [jnp.where(keep, _dot_nt(qs[h], kn[h]) - cn_ref[0, h:h + 1, 0:lq], NEG_INF)
             for h in range(nh)], axis=0)
        update(s, heads(vn_ref))
        o = acc_ref[...] / l_ref[...]
        for h in range(nh):
            o_ref[:, h * hd:(h + 1) * hd] = o[h * lq:(h + 1) * lq].astype(o_ref.dtype)


def fox_sample_attention(q, k_cache, v_cache, layer, kb, vb, c, nb, nh):
    m, d = q.shape
    lq = m // nb
    past, hd = k_cache.shape[2], k_cache.shape[4]
    tk = min(FOX_TK, past)
    nkv = past // tk
    new = pl.BlockSpec((lq, d), lambda b, j: (b, 0))
    cache = pl.BlockSpec((None, None, tk, nh, hd), lambda b, j: (layer, b, j, 0, 0))
    return pl.pallas_call(
        functools.partial(_fox_sample_body, nh),
        grid=(nb, nkv),
        in_specs=[new, cache, cache, new, new,
                  pl.BlockSpec((1, nh, tk), lambda b, j: (b, 0, j)),
                  pl.BlockSpec((1, nh, LANES), lambda b, j: (b, 0, past // LANES))],
        out_specs=new,
        out_shape=jax.ShapeDtypeStruct((m, d), BF16),
        scratch_shapes=[pltpu.VMEM((nh * lq, 1), F32), pltpu.VMEM((nh * lq, 1), F32),
                        pltpu.VMEM((nh * lq, hd), F32)],
        compiler_params=_params("parallel", "arbitrary"),
        name="fox_sample_attention",
    )(q, k_cache, v_cache, kb, vb, c, c)


def _band_prompt_body(tq, hd, q_ref, k_ref, v_ref, bias_ref, o_ref):
    length = q_ref.shape[0]
    span = bias_ref.shape[2]
    for i in range(length // tq):
        k0 = max((i + 1) * tq - span, 0)
        k1 = (i + 1) * tq
        for hh in range(q_ref.shape[1] // hd):
            cols = slice(hh * hd, (hh + 1) * hd)
            q = q_ref[i * tq:(i + 1) * tq, cols]
            s = _dot_nt(q, k_ref[k0:k1, cols]) + bias_ref[hh, :, span - (k1 - k0):span]
            o_ref[i * tq:(i + 1) * tq, cols] = _softmax_pv([(s, v_ref[k0:k1, cols])]).astype(o_ref.dtype)


def band_prompt_attention(q, kb, vb, bias, nb, nh):
    m, d = q.shape
    length = m // nb
    hd = d // nh
    tq = bias.shape[1]
    hps = min(ATTN_HEADS_PER_STEP, nh)
    seq = pl.BlockSpec((length, hps * hd), lambda b, h: (b, h))
    return pl.pallas_call(
        functools.partial(_band_prompt_body, tq, hd),
        grid=(nb, nh // hps),
        in_specs=[seq, seq, seq, pl.BlockSpec((hps,) + bias.shape[1:], lambda b, h: (h, 0, 0))],
        out_specs=seq,
        out_shape=jax.ShapeDtypeStruct((m, d), BF16),
        compiler_params=_params("parallel", "parallel"),
        name="band_prompt_attention",
    )(q, kb, vb, bias)


def _band_sample_body(nh, q_ref, kc_ref, vc_ref, kn_ref, vn_ref, bias_ref, o_ref):
    lq, d = q_ref.shape
    hd = d // nh
    lc = kc_ref.shape[0]
    k_all = pltpu.einshape("lhd->hld", kc_ref[...])
    v_all = pltpu.einshape("lhd->hld", vc_ref[...])
    heads = lambda ref: [ref[:, h * hd:(h + 1) * hd] for h in range(nh)]
    qs, kn, vn = heads(q_ref), heads(kn_ref), heads(vn_ref)
    s_c = jnp.concatenate([_dot_nt(qs[h], k_all[h].astype(BF16)) + bias_ref[h, :, 0:lc]
                           for h in range(nh)], axis=0)
    s_n = jnp.concatenate([_dot_nt(qs[h], kn[h]) + bias_ref[h, :, lc:lc + lq] for h in range(nh)], axis=0)
    m = jnp.maximum(jnp.max(s_c, axis=-1, keepdims=True), jnp.max(s_n, axis=-1, keepdims=True))
    e_c, e_n = jnp.exp2(s_c - m), jnp.exp2(s_n - m)
    inv = 1.0 / (jnp.sum(e_c, axis=-1, keepdims=True) + jnp.sum(e_n, axis=-1, keepdims=True))
    e_c, e_n = e_c.astype(BF16), e_n.astype(BF16)
    for h in range(nh):
        rows = slice(h * lq, (h + 1) * lq)
        o = _dot(e_c[rows], v_all[h].astype(BF16)) + _dot(e_n[rows], vn[h])
        o_ref[:, h * hd:(h + 1) * hd] = (o * inv[rows]).astype(o_ref.dtype)


def band_sample_attention(q, k_cache, v_cache, layer, kb, vb, bias, nb, nh):
    m, d = q.shape
    lq = m // nb
    lc, hd = k_cache.shape[2], k_cache.shape[4]
    new = pl.BlockSpec((lq, d), lambda b: (b, 0))
    cache = pl.BlockSpec((None, None, lc, nh, hd), lambda b: (layer, b, 0, 0, 0))
    return pl.pallas_call(
        functools.partial(_band_sample_body, nh),
        grid=(nb,),
        in_specs=[new, cache, cache, new, new, pl.BlockSpec(bias.shape, lambda b: (0, 0, 0))],
        out_specs=new,
        out_shape=jax.ShapeDtypeStruct((m, d), BF16),
        compiler_params=_params("parallel"),
        name="band_sample_attention",
    )(q, k_cache, v_cache, kb, vb, bias)


def _band_bias_body(window, scale, diag_ref, o_ref):
    tq, span = o_ref.shape[1], o_ref.shape[2]
    n = diag_ref.shape[2]
    rows = jnp.broadcast_to(diag_ref[0], (tq, n))
    tile = pltpu.roll(rows, n - (tq - 1), 1, stride=1, stride_axis=0)[:, 0:span]
    qc = lax.broadcasted_iota(jnp.int32, (tq, span), 0) // CHUNK
    kc = lax.broadcasted_iota(jnp.int32, (tq, span), 1) // CHUNK - window // CHUNK
    visible = (kc <= qc) & (kc >= qc - BAND_PREV)
    o_ref[0] = jnp.where(visible, tile * scale, NEG_INF)


def band_bias_tile(rel_bias, tq, window, scale):
    nh = rel_bias.shape[0]
    span = window + tq
    n = span + tq
    rel = (span - 1) - np.arange(n)
    diag = rel_bias.astype(F32)[:, np.clip(rel, -REL_CLIP, REL_CLIP) + REL_CLIP].reshape(nh, 1, n)
    return pl.pallas_call(
        functools.partial(_band_bias_body, window, scale),
        grid=(nh,),
        in_specs=[pl.BlockSpec((1, 1, n), lambda h: (h, 0, 0))],
        out_specs=pl.BlockSpec((1, tq, span), lambda h: (h, 0, 0)),
        out_shape=jax.ShapeDtypeStruct((nh, tq, span), F32),
        compiler_params=_params("parallel"),
        name="band_bias_tile",
    )(diag)


def kernel(x_prompt, x_sample, state_ssm_re, state_ssm_im, cache_fox_k, cache_fox_v, cache_fox_logf,
           cache_band_k, cache_band_v, ffn1_norm, ffn1_w_in, ffn1_w_out, mix_norm, ffn2_norm, ffn2_w_in,
           ffn2_w_out, ssm_a_re, ssm_a_im, ssm_log_step, ssm_b_re, ssm_b_im, ssm_c_re, ssm_c_im, ssm_d,
           ssm_w_glu, fox_w_in, fox_b_f, fox_w_out, band_w_in, band_rel_bias, band_w_out, final_norm):
    bp, lp, d = x_prompt.shape
    bs, ls, _ = x_sample.shape
    depth = ffn1_norm.shape[0]
    nh = fox_b_f.shape[1]
    hd = d // nh
    scale = hd ** -0.5 * LOG2E
    window = cache_band_k.shape[2]
    heads = lambda t, nb, length: t.reshape(nb, length, nh, hd)

    xp = x_prompt.reshape(bp * lp, d)
    xs = x_sample.reshape(bs * ls, d)
    outs = {name: [] for name in (
        "p_ssm_re", "p_ssm_im", "p_fox_k", "p_fox_v", "p_fox_logf", "p_band_k", "p_band_v",
        "s_ssm_re", "s_ssm_im", "s_fox_k", "s_fox_v", "s_fox_logf", "s_band_k", "s_band_v")}

    w_glu = ssm_w_glu.astype(BF16)
    fox_in, fox_out = fox_w_in.astype(BF16), fox_w_out.astype(BF16)
    band_in, band_out = band_w_in.astype(BF16), band_w_out.astype(BF16)

    def ffn(xp, xs, g, w_in, w_out, layer):
        xs, w_gate, w_up, w_down = ffn_half_step_casting(xs, g, w_in, w_out, layer)
        return ffn_half_step(xp, g, w_gate, w_up, w_down), xs

    g, p = ssm_a_re.shape[1:]
    fold = lambda a: a.reshape((-1,) + a.shape[2:])
    mats = ssm_matrices(fold(ssm_a_re), fold(ssm_a_im), fold(ssm_log_step), fold(ssm_b_re), fold(ssm_b_im),
                        fold(ssm_c_re), fold(ssm_c_im))

    for i in range(depth):
        kind, j = i % 3, i // 3
        xp, xs = ffn(xp, xs, ffn1_norm[i], ffn1_w_in, ffn1_w_out, i)
        if kind == 0:
            nblk = d // LANES
            zeros = jnp.zeros((bp, g, p), F32)
            streams = ((xp, bp, lp, zeros, zeros, "p"), (xs, bs, ls, state_ssm_re[j], state_ssm_im[j], "s"))
            new_x = []
            for x, nb, length, s_re, s_im, tag in streams:
                h_tm = rmsnorm_time_major(x, mix_norm[i], nb, length)
                a_tm, sf = ssm_core(h_tm, mats, j, ssm_pack_state(s_re, s_im, nblk), ssm_d[j])
                f_re, f_im = ssm_unpack_state(sf, g, p)
                outs[tag + "_ssm_re"].append(f_re); outs[tag + "_ssm_im"].append(f_im)
                new_x.append(ssm_glu_out(x, a_tm, w_glu, j, nb, length))
            xp, xs = new_x
        elif kind == 1:
            wf_t = fox_w_in[j][:, 3 * d:].T.astype(BF16)
            qp, kp, vp, kbp, vbp, lfp = qkv_project(xp, mix_norm[i], fox_in, j, scale, wf_t, fox_b_f[j])
            qs, ks, vs, kbs, vbs, lfs = qkv_project(xs, mix_norm[i], fox_in, j, scale, wf_t, fox_b_f[j])
            lfp = lfp.reshape(nh, bp, lp).transpose(1, 0, 2)
            lfs = lfs.reshape(nh, bs, ls).transpose(1, 0, 2)
            cp = (cumsum_time(lfp) * LOG2E).reshape(bp * nh, 1, lp)
            lf_all = jnp.concatenate([cache_fox_logf[j].astype(F32).transpose(0, 2, 1), lfs,
                                      jnp.zeros((bs, nh, LANES - ls), F32)], axis=2)
            cs = cumsum_time(lf_all) * LOG2E
            op = fox_prompt_attention(qp, kbp, vbp, cp, bp, nh)
            os_ = fox_sample_attention(qs, cache_fox_k, cache_fox_v, j, kbs, vbs, cs, bs, nh)
            xp = out_project(xp, op, fox_out, j)
            xs = out_project(xs, os_, fox_out, j)
            outs["p_fox_k"].append(heads(kp, bp, lp)); outs["p_fox_v"].append(heads(vp, bp, lp))
            outs["p_fox_logf"].append(lfp.transpose(0, 2, 1))
            outs["s_fox_k"].append(heads(ks, bs, ls)); outs["s_fox_v"].append(heads(vs, bs, ls))
            outs["s_fox_logf"].append(lfs.transpose(0, 2, 1))
        else:
            qp, kp, vp, kbp, vbp = qkv_project(xp, mix_norm[i], band_in, j, scale)
            qs, ks, vs, kbs, vbs = qkv_project(xs, mix_norm[i], band_in, j, scale)
            bw = BAND_PREV * CHUNK
            bias = band_bias_tile(band_rel_bias[j], BAND_TQ, bw, LOG2E)
            op = band_prompt_attention(qp, kbp, vbp, bias, bp, nh)
            os_ = band_sample_attention(qs, cache_band_k, cache_band_v, j, kbs, vbs,
                                        bias[:, :ls, bw - window:bw + ls], bs, nh)
            xp = out_project(xp, op, band_out, j)
            xs = out_project(xs, os_, band_out, j)
            keep = min(bw, lp)
            outs["p_band_k"].append(heads(kp, bp, lp)[:, lp - keep:])
            outs["p_band_v"].append(heads(vp, bp, lp)[:, lp - keep:])
            outs["s_band_k"].append(heads(ks, bs, ls)); outs["s_band_v"].append(heads(vs, bs, ls))
        xp, xs = ffn(xp, xs, ffn2_norm[i], ffn2_w_in, ffn2_w_out, i)

    y_prompt = rmsnorm(xp, final_norm).reshape(bp, lp, d)
    y_sample = rmsnorm(xs, final_norm).reshape(bs, ls, d)
    st = {k: jnp.stack(v) for k, v in outs.items()}
    return (y_prompt, y_sample, st["p_ssm_re"], st["p_ssm_im"], st["p_fox_k"], st["p_fox_v"],
            st["p_fox_logf"], st["p_band_k"], st["p_band_v"], st["s_ssm_re"], st["s_ssm_im"],
            st["s_fox_k"], st["s_fox_v"], st["s_fox_logf"], st["s_band_k"], st["s_band_v"])
```

```python
import functools

import jax
import jax.numpy as jnp
import numpy as np
from jax import lax
from jax.experimental import pallas as pl
from jax.experimental.pallas import tpu as pltpu

F32 = jnp.float32
BF16 = jnp.bfloat16

EPS = 1e-6
NEG_INF = -1e30
LOG2E = 1.4426950408889634
CHUNK = 64
BAND_PREV = 8
REL_CLIP = 256
SSM_GROUP = 16
SSM_T = 8
LANES = 128
VMEM_LIMIT = 56 * 1024 * 1024

TM = 512
TM_FFN = 1024
TF = 512
TF_CAST = 256
TM_QKV = 1024
TN = 512
TN_GLU = 1024
BAND_TQ = 256
FOX_TQ = 256
ATTN_HEADS_PER_STEP = 4
FOX_TK = 512
SSM_ROWS = 512


def _params(*sem):
    return pltpu.CompilerParams(dimension_semantics=sem, vmem_limit_bytes=VMEM_LIMIT)


def _rms(x, g):
    return x * lax.rsqrt(jnp.mean(x * x, axis=-1, keepdims=True) + EPS) * g


def _dot(a, b):
    return jnp.dot(a, b, preferred_element_type=F32)


def _dot_nt(a, b):
    return lax.dot_general(a, b, (((1,), (1,)), ((), ())), preferred_element_type=F32)


def _ffn_body(x_ref, g_ref, wg_ref, wu_ref, wo_ref, o_ref, h_ref):
    f = pl.program_id(1)

    @pl.when(f == 0)
    def _():
        x = x_ref[...]
        h_ref[...] = _rms(x, g_ref[...]).astype(BF16)
        o_ref[...] = x

    h = h_ref[...]
    gate = _dot(h, wg_ref[...])
    up = _dot(h, wu_ref[...])
    act = (gate * jax.nn.sigmoid(gate) * up).astype(BF16)
    o_ref[...] += _dot(act, wo_ref[...])


def ffn_half_step(x, g, w_gate, w_up, w_out):
    m, d = x.shape
    f = w_out.shape[0]
    tm = min(TM_FFN, m)
    tf = min(TF, f)
    return pl.pallas_call(
        _ffn_body,
        grid=(m // tm, f // tf),
        in_specs=[
            pl.BlockSpec((tm, d), lambda i, j: (i, 0)),
            pl.BlockSpec((1, d), lambda i, j: (0, 0)),
            pl.BlockSpec((d, tf), lambda i, j: (0, j)),
            pl.BlockSpec((d, tf), lambda i, j: (0, j)),
            pl.BlockSpec((tf, d), lambda i, j: (j, 0)),
        ],
        out_specs=pl.BlockSpec((tm, d), lambda i, j: (i, 0)),
        out_shape=jax.ShapeDtypeStruct((m, d), F32),
        scratch_shapes=[pltpu.VMEM((tm, d), BF16)],
        compiler_params=_params("parallel", "arbitrary"),
        name="ffn_half_step",
    )(x, g.reshape(1, d), w_gate, w_up, w_out)


def _ffn_cast_body(x_ref, g_ref, wg32_ref, wu32_ref, wo32_ref, o_ref, wg_ref, wu_ref, wo_ref, h_ref):
    wg_ref[...] = wg32_ref[...].astype(BF16)
    wu_ref[...] = wu32_ref[...].astype(BF16)
    wo_ref[...] = (0.5 * wo32_ref[...]).astype(BF16)
    _ffn_body(x_ref, g_ref, wg_ref, wu_ref, wo_ref, o_ref, h_ref)


def ffn_half_step_casting(x, g, w_in, w_out, layer):
    m, d = x.shape
    f = w_out.shape[1]
    tf = min(TF_CAST, f)
    nf = f // tf
    once = dict(pipeline_mode=pl.Buffered(1))
    col = pl.BlockSpec((d, tf), lambda i, j: (0, j))
    return pl.pallas_call(
        _ffn_cast_body,
        grid=(1, nf),
        in_specs=[
            pl.BlockSpec((m, d), lambda i, j: (0, 0), **once),
            pl.BlockSpec((1, d), lambda i, j: (0, 0)),
            pl.BlockSpec((None, d, tf), lambda i, j: (layer, 0, j)),
            pl.BlockSpec((None, d, tf), lambda i, j: (layer, 0, j + nf)),
            pl.BlockSpec((None, tf, d), lambda i, j: (layer, j, 0)),
        ],
        out_specs=[pl.BlockSpec((m, d), lambda i, j: (0, 0)), col, col,
                   pl.BlockSpec((tf, d), lambda i, j: (j, 0))],
        out_shape=[jax.ShapeDtypeStruct((m, d), F32), jax.ShapeDtypeStruct((d, f), BF16),
                   jax.ShapeDtypeStruct((d, f), BF16), jax.ShapeDtypeStruct((f, d), BF16)],
        scratch_shapes=[pltpu.VMEM((m, d), BF16)],
        compiler_params=_params("arbitrary", "arbitrary"),
        name="ffn_half_step_casting",
    )(x, g.reshape(1, d), w_in, w_in, w_out)


def _rmsnorm_body(x_ref, g_ref, o_ref):
    o_ref[...] = _rms(x_ref[...], g_ref[...]).astype(o_ref.dtype)


def rmsnorm(x, g):
    m, d = x.shape
    tm = min(TM, m)
    return pl.pallas_call(
        _rmsnorm_body,
        grid=(m // tm,),
        in_specs=[pl.BlockSpec((tm, d), lambda i: (i, 0)),
                  pl.BlockSpec((1, d), lambda i: (0, 0))],
        out_specs=pl.BlockSpec((tm, d), lambda i: (i, 0)),
        out_shape=jax.ShapeDtypeStruct((m, d), F32),
        compiler_params=_params("parallel"),
        name="rmsnorm",
    )(x, g.reshape(1, d))


def _rmsnorm_tm_body(x_ref, g_ref, o_ref):
    h = _rms(x_ref[...], g_ref[...])
    o_ref[...] = pltpu.einshape("bld->lbd", h).reshape(o_ref.shape)


def rmsnorm_time_major(x, g, nb, length):
    d = x.shape[1]
    t = SSM_T
    tl = min(TM // nb, length)
    return pl.pallas_call(
        _rmsnorm_tm_body,
        grid=(length // tl,),
        in_specs=[pl.BlockSpec((nb, tl, d), lambda l: (0, l, 0)),
                  pl.BlockSpec((1, d), lambda l: (0, 0))],
        out_specs=pl.BlockSpec((tl // t, t, nb, d), lambda l: (l, 0, 0, 0)),
        out_shape=jax.ShapeDtypeStruct((length // t, t, nb, d), F32),
        compiler_params=_params("parallel"),
        name="rmsnorm_time_major",
    )(x.reshape(nb, length, d), g.reshape(1, d))


def _qkv_body(scale, has_forget, *refs):
    if has_forget:
        (x_ref, g_ref, wq_ref, wk_ref, wv_ref, wf_ref, bf_ref,
         q_ref, k32_ref, v32_ref, kb_ref, vb_ref, lf_ref, h_ref) = refs
    else:
        (x_ref, g_ref, wq_ref, wk_ref, wv_ref,
         q_ref, k32_ref, v32_ref, kb_ref, vb_ref, h_ref) = refs

    @pl.when(pl.program_id(1) == 0)
    def _():
        h = _rms(x_ref[...], g_ref[...]).astype(BF16)
        h_ref[...] = h
        if has_forget:
            z = _dot_nt(wf_ref[...], h) + bf_ref[...]
            lf_ref[...] = jnp.minimum(z, 0.0) - jnp.log(1.0 + jnp.exp(-jnp.abs(z)))

    h = h_ref[...]
    q_ref[...] = (_dot(h, wq_ref[...]) * scale).astype(BF16)
    k = _dot(h, wk_ref[...])
    k32_ref[...] = k
    kb_ref[...] = k.astype(BF16)
    v = _dot(h, wv_ref[...])
    v32_ref[...] = v
    vb_ref[...] = v.astype(BF16)


def qkv_project(x, g, w, layer, scale, wf_t=None, b_f=None):
    m, d = x.shape
    tm = min(TM_QKV, m)
    tn = min(TN, d)
    nq = d // tn
    has_forget = wf_t is not None
    in_specs = [
        pl.BlockSpec((tm, d), lambda i, n: (i, 0)),
        pl.BlockSpec((1, d), lambda i, n: (0, 0)),
        pl.BlockSpec((None, d, tn), lambda i, n: (layer, 0, n)),
        pl.BlockSpec((None, d, tn), lambda i, n: (layer, 0, n + nq)),
        pl.BlockSpec((None, d, tn), lambda i, n: (layer, 0, n + 2 * nq)),
    ]
    args = [x, g.reshape(1, d), w, w, w]
    tile = pl.BlockSpec((tm, tn), lambda i, n: (i, n))
    out_specs = [tile] * 5
    out_shape = [jax.ShapeDtypeStruct((m, d), BF16),
                 jax.ShapeDtypeStruct((m, d), F32), jax.ShapeDtypeStruct((m, d), F32),
                 jax.ShapeDtypeStruct((m, d), BF16), jax.ShapeDtypeStruct((m, d), BF16)]
    if has_forget:
        nh = wf_t.shape[0]
        in_specs += [pl.BlockSpec((nh, d), lambda i, n: (0, 0)),
                     pl.BlockSpec((nh, 1), lambda i, n: (0, 0))]
        args += [wf_t, b_f.reshape(nh, 1)]
        out_specs = out_specs + [pl.BlockSpec((nh, tm), lambda i, n: (0, i))]
        out_shape = out_shape + [jax.ShapeDtypeStruct((nh, m), F32)]
    return pl.pallas_call(
        functools.partial(_qkv_body, scale, has_forget),
        grid=(m // tm, nq),
        in_specs=in_specs,
        out_specs=out_specs,
        out_shape=out_shape,
        scratch_shapes=[pltpu.VMEM((tm, d), BF16)],
        compiler_params=_params("parallel", "arbitrary"),
        name="qkv_project",
    )(*args)


def _out_proj_body(x_ref, o_ref, w_ref, y_ref):
    y_ref[...] = x_ref[...] + _dot(o_ref[...], w_ref[...])


def out_project(x, o, w, layer):
    m, d = x.shape
    tm = min(TM, m)
    return pl.pallas_call(
        _out_proj_body,
        grid=(m // tm,),
        in_specs=[pl.BlockSpec((tm, d), lambda i: (i, 0)),
                  pl.BlockSpec((tm, d), lambda i: (i, 0)),
                  pl.BlockSpec((None, d, d), lambda i: (layer, 0, 0))],
        out_specs=pl.BlockSpec((tm, d), lambda i: (i, 0)),
        out_shape=jax.ShapeDtypeStruct((m, d), F32),
        compiler_params=_params("parallel"),
        name="out_project",
    )(x, o, w)


def _glu_body(a_ref, xt_ref, wv_ref, wg_ref, o_ref):
    nb, tl, tn = xt_ref.shape
    a = a_ref[...].reshape(tl * nb, a_ref.shape[2])
    val = _dot(a, wv_ref[...])
    gate = _dot(a, wg_ref[...])
    r = (val * jax.nn.sigmoid(gate)).reshape(tl, nb, tn)
    o_ref[...] = xt_ref[...] + pltpu.einshape("lbn->bln", r)


def ssm_glu_out(x, a4, w_glu, layer, nb, length):
    d = x.shape[1]
    t = SSM_T
    tl = min(TM // nb, length)
    tn = min(TN_GLU, d)
    nq = d // tn
    tile = pl.BlockSpec((nb, tl, tn), lambda n, l: (0, l, n))
    return pl.pallas_call(
        _glu_body,
        grid=(nq, length // tl),
        in_specs=[
            pl.BlockSpec((tl // t, t * nb, d), lambda n, l: (l, 0, 0)),
            tile,
            pl.BlockSpec((None, d, tn), lambda n, l: (layer, 0, n)),
            pl.BlockSpec((None, d, tn), lambda n, l: (layer, 0, n + nq)),
        ],
        out_specs=tile,
        out_shape=jax.ShapeDtypeStruct((nb, length, d), F32),
        compiler_params=_params("parallel", "parallel"),
        name="ssm_glu_out",
    )(a4, x.reshape(nb, length, d), w_glu, w_glu).reshape(nb * length, d)


def _ssm_lag_body(ca_ref, bb_ref, k_ref):
    for i in range(ca_ref.shape[0]):
        k_ref[i] = jnp.dot(ca_ref[i], bb_ref[i], preferred_element_type=F32,
                           precision=lax.Precision.HIGHEST)


def ssm_lag_kernels(ca, bb, gpb):
    g, r, p2 = ca.shape
    c = bb.shape[2]
    return pl.pallas_call(
        _ssm_lag_body,
        grid=(g // gpb,),
        in_specs=[pl.BlockSpec((gpb, r, p2), lambda i: (i, 0, 0)),
                  pl.BlockSpec((gpb, p2, c), lambda i: (i, 0, 0))],
        out_specs=pl.BlockSpec((gpb, r, c), lambda i: (i, 0, 0)),
        out_shape=jax.ShapeDtypeStruct((g, r, c), F32),
        compiler_params=_params("parallel"),
        name="ssm_lag_kernels",
    )(ca, bb)


def _ssm_core_body(nb, x_ref, bdv_ref, ws_ref, wc_ref, at_ref, s0_ref, dsk_ref, y_ref, sf_ref,
                   ds_ref, dsw_ref, sst_ref, st_ref):
    nchp, t = x_ref.shape[0], x_ref.shape[1]
    m = nchp * nb
    ncol = at_ref.shape[2]
    half = LANES // 2

    def swap_halves(v):
        lane = lax.broadcasted_iota(jnp.int32, v.shape, 1)
        return jnp.where(lane % LANES < half, pltpu.roll(v, ncol - half, 1), pltpu.roll(v, half, 1))

    @pl.when(pl.program_id(1) == 0)
    def _():
        st_ref[...] = s0_ref[0]

    xcat = jnp.concatenate(
        [x_ref[:, tau].reshape(m, LANES).astype(BF16) for tau in range(t)], axis=1)
    ds = _dot(xcat, ws_ref[0])
    ds_ref[...] = ds
    dsw_ref[...] = swap_halves(ds)
    a1 = jnp.broadcast_to(at_ref[0, 0:1, :], (nb, ncol))
    a2 = jnp.broadcast_to(at_ref[0, 1:2, :], (nb, ncol))

    def step(k, carry):
        s, sw = carry
        rows = pl.ds(pl.multiple_of(k * nb, nb), nb)
        sst_ref[rows, :] = s
        ns = a1 * s + a2 * sw + ds_ref[rows, :]
        nsw = a1 * sw - a2 * s + dsw_ref[rows, :]
        return ns, nsw

    s0 = st_ref[...]
    s, _ = lax.fori_loop(0, nchp, step, (s0, swap_halves(s0)))
    st_ref[...] = s
    sf_ref[0] = s
    y_in = _dot(sst_ref[...].astype(BF16), wc_ref[0])
    dsk = dsk_ref[...]
    def out_step(tt):
        yt = y_in[:, tt * LANES:(tt + 1) * LANES] + _dot(
            xcat[:, :(tt + 1) * LANES], bdv_ref[0, (t - 1 - tt) * LANES:, :])
        return jax.nn.gelu(yt + dsk * x_ref[:, tt].reshape(m, LANES)).reshape(nchp, nb, LANES)

    for tt in range(0, t, 2):
        pair = jnp.concatenate([out_step(tt), out_step(tt + 1)], axis=1)
        y_ref[:, tt * nb:(tt + 2) * nb, :] = pair.astype(y_ref.dtype)


def ssm_core(h4, mats, layer, s0, d_skip):
    bdv, ws, wc, at = mats
    nch, t, nb, d = h4.shape
    nblk = d // LANES
    nchp = max(min(SSM_ROWS // nb, nch), 1)
    m = nchp * nb
    ncol = ws.shape[2]
    blk0 = layer * nblk
    wblk = lambda *s: pl.BlockSpec((1,) + s, lambda c, k: (c + blk0, 0, 0))
    sblk = pl.BlockSpec((1, nb, ncol), lambda c, k: (c, 0, 0))
    seq = pl.BlockSpec((nchp, t, nb, LANES), lambda c, k: (k, 0, 0, c))
    y, sf = pl.pallas_call(
        functools.partial(_ssm_core_body, nb),
        grid=(nblk, nch // nchp),
        in_specs=[seq, wblk(t * LANES, LANES), wblk(t * LANES, ncol), wblk(ncol, t * LANES),
                  wblk(2, ncol), sblk, pl.BlockSpec((1, LANES), lambda c, k: (0, c))],
        out_specs=[pl.BlockSpec((nchp, t * nb, LANES), lambda c, k: (k, 0, c)), sblk],
        out_shape=[jax.ShapeDtypeStruct((nch, t * nb, d), BF16),
                   jax.ShapeDtypeStruct((nblk, nb, ncol), F32)],
        scratch_shapes=[pltpu.VMEM((m, ncol), F32), pltpu.VMEM((m, ncol), F32), pltpu.VMEM((m, ncol), F32),
                        pltpu.VMEM((nb, ncol), F32)],
        compiler_params=_params("parallel", "arbitrary"),
        name="ssm_core",
    )(h4, bdv, ws, wc, at, s0, d_skip.reshape(1, d))
    return y, sf


def _ssm_expand_body(place_ref, wsc_ref, wcc_ref, kcp_ref, ws_ref, wc_ref, bdv_ref):
    gpb = place_ref.shape[0]
    ws_ref[0] = jnp.concatenate(
        [_dot(place_ref[h], wsc_ref[0, h]) for h in range(gpb)], axis=1).astype(BF16)
    wc_ref[0] = jnp.concatenate(
        [_dot_nt(wcc_ref[0, h], place_ref[h]) for h in range(gpb)], axis=0).astype(BF16)
    bdv = _dot(place_ref[0], kcp_ref[0, 0])
    for h in range(1, gpb):
        bdv = bdv + _dot(place_ref[h], kcp_ref[0, h])
    bdv_ref[0] = bdv.astype(BF16)


def ssm_expand(place, wsc, wcc, kcp):
    nblk, gpb, tc, n = wsc.shape
    rows = place.shape[1]
    per_blk = lambda *s: pl.BlockSpec((1,) + s, lambda b: (b,) + (0,) * len(s))
    return pl.pallas_call(
        _ssm_expand_body,
        grid=(nblk,),
        in_specs=[pl.BlockSpec(place.shape, lambda b: (0, 0, 0)), per_blk(gpb, tc, n), per_blk(gpb, n, tc),
                  per_blk(gpb, tc, LANES)],
        out_specs=[per_blk(rows, gpb * n), per_blk(gpb * n, rows), per_blk(rows, LANES)],
        out_shape=[jax.ShapeDtypeStruct((nblk, rows, gpb * n), BF16),
                   jax.ShapeDtypeStruct((nblk, gpb * n, rows), BF16),
                   jax.ShapeDtypeStruct((nblk, rows, LANES), BF16)],
        compiler_params=_params("parallel"),
        name="ssm_expand",
    )(place, wsc, wcc, kcp)


def ssm_matrices(a_re, a_im, log_step, b_re, b_im, c_re, c_im):
    t = SSM_T
    g, p = a_re.shape
    c = b_re.shape[2]
    gpb = LANES // c
    nblk = g // gpb
    dt = jnp.exp(log_step)[:, None]
    lam_re, lam_im = a_re * dt, a_im * dt
    j = jnp.arange(t + 1, dtype=F32)[:, None, None]
    mag = jnp.exp(lam_re * j)
    pw_re, pw_im = mag * jnp.cos(lam_im * j), mag * jnp.sin(lam_im * j)
    ab_re, ab_im = pw_re[1], pw_im[1]
    den = a_re * a_re + a_im * a_im
    z_re = ((ab_re - 1.0) * a_re + ab_im * a_im) / den
    z_im = (ab_im * a_re - (ab_re - 1.0) * a_im) / den
    bb_re = z_re[..., None] * b_re - z_im[..., None] * b_im
    bb_im = z_re[..., None] * b_im + z_im[..., None] * b_re
    jr = (t - 1) - jnp.arange(t, dtype=F32)[:, None, None]
    mag_r = jnp.exp(lam_re * jr)
    rp_re, rp_im = mag_r * jnp.cos(lam_im * jr), mag_r * jnp.sin(lam_im * jr)
    ws_re = rp_re[:, :, :, None] * bb_re[None] - rp_im[:, :, :, None] * bb_im[None]
    ws_im = rp_re[:, :, :, None] * bb_im[None] + rp_im[:, :, :, None] * bb_re[None]
    wsc = jnp.concatenate([ws_re, ws_im], axis=2).transpose(1, 0, 3, 2).reshape(nblk, gpb, t * c, 2 * p)

    def out_weights(q_re, q_im):
        cr, ci = c_re[None], c_im[None]
        qr, qi = q_re[:, :, None, :], q_im[:, :, None, :]
        return cr * qr - ci * qi, -cr * qi - ci * qr

    wc_re, wc_im = out_weights(pw_re[1:], pw_im[1:])
    wcc = jnp.concatenate([wc_re, wc_im], axis=3).transpose(1, 3, 0, 2).reshape(nblk, gpb, 2 * p, t * c)

    ca_re, ca_im = out_weights(rp_re, rp_im)
    ca = jnp.concatenate([ca_re, ca_im], axis=-1).transpose(1, 0, 2, 3).reshape(g, t * c, 2 * p)
    bb = jnp.concatenate([bb_re, bb_im], axis=1)
    kmat = ssm_lag_kernels(ca, bb, gpb).reshape(g, t, c, c)
    kc = kmat.transpose(0, 1, 3, 2).reshape(nblk, gpb, t * c, c)

    place = np.zeros((gpb, t, gpb, c, t, c), np.float32)
    for h in range(gpb):
        place[h, :, h] = np.eye(t * c, dtype=np.float32).reshape(t, c, t, c)
    place = jnp.asarray(place.reshape(gpb, t * LANES, t * c), BF16)
    eye = jnp.eye(gpb, dtype=F32)
    kcp = (kc[:, :, :, None, :] * eye[None, :, None, :, None]).reshape(nblk, gpb, t * c, LANES)
    ws, wc, bdv = ssm_expand(place, wsc.astype(BF16), wcc.astype(BF16), kcp.astype(BF16))

    at_re = pw_re[t].reshape(nblk, gpb, p)
    at_im = pw_im[t].reshape(nblk, gpb, p)
    a1 = jnp.concatenate([at_re, at_re], axis=2).reshape(nblk, gpb * 2 * p)
    a2 = jnp.concatenate([-at_im, at_im], axis=2).reshape(nblk, gpb * 2 * p)
    return bdv, ws, wc, jnp.stack([a1, a2], axis=1)


def ssm_pack_state(s_re, s_im, nblk):
    nb, g, p = s_re.shape
    s = jnp.stack([s_re.reshape(nb, nblk, g // nblk, p), s_im.reshape(nb, nblk, g // nblk, p)], axis=3)
    return s.transpose(1, 0, 2, 3, 4).reshape(nblk, nb, -1)


def ssm_unpack_state(sf, g, p):
    nblk, nb, _ = sf.shape
    s = sf.reshape(nblk, nb, g // nblk, 2, p).transpose(1, 0, 2, 3, 4).reshape(nb, g, 2, p)
    return s[:, :, 0], s[:, :, 1]


def _split3(x):
    hi = x.astype(BF16)
    r = x - hi.astype(F32)
    mid = r.astype(BF16)
    lo = (r - mid.astype(F32)).astype(BF16)
    return hi, mid, lo


def _cumsum_body(x_ref, o_ref):
    nblk = x_ref.shape[2] // LANES
    rows = x_ref.shape[1]
    row = lax.broadcasted_iota(jnp.int32, (LANES, LANES), 0)
    col = lax.broadcasted_iota(jnp.int32, (LANES, LANES), 1)
    tri = jnp.where(row <= col, 1.0, 0.0).astype(BF16)
    carry = jnp.zeros((rows, 1), F32)
    for j in range(nblk):
        sl = slice(j * LANES, (j + 1) * LANES)
        hi, mid, lo = _split3(x_ref[0, :, sl])
        w = _dot(hi, tri) + _dot(mid, tri) + _dot(lo, tri)
        o_ref[0, :, sl] = w + carry
        carry = carry + jnp.sum(x_ref[0, :, sl], axis=1, keepdims=True)


def cumsum_time(x):
    n, h, length = x.shape
    return pl.pallas_call(
        _cumsum_body,
        grid=(n,),
        in_specs=[pl.BlockSpec((1, h, length), lambda i: (i, 0, 0))],
        out_specs=pl.BlockSpec((1, h, length), lambda i: (i, 0, 0)),
        out_shape=jax.ShapeDtypeStruct((n, h, length), F32),
        compiler_params=_params("parallel"),
        name="cumsum_time",
    )(x)


def _softmax_pv(parts):
    m = functools.reduce(jnp.maximum, [jnp.max(s, axis=-1, keepdims=True) for s, _ in parts])
    acc, den = None, None
    for s, v in parts:
        e = jnp.exp2(s - m)
        l = jnp.sum(e, axis=-1, keepdims=True)
        o = _dot(e.astype(BF16), v)
        acc = o if acc is None else acc + o
        den = l if den is None else den + l
    return acc / den


def _causal(n):
    rpos = lax.broadcasted_iota(jnp.int32, (n, n), 0)
    cpos = lax.broadcasted_iota(jnp.int32, (n, n), 1)
    return cpos <= rpos


def _fox_prompt_body(tq, hd, q_ref, k_ref, v_ref, c_ref, o_ref):
    length = q_ref.shape[0]
    keep = _causal(tq)
    for i in range(length // tq):
        lo, hi = i * tq, (i + 1) * tq
        for hh in range(q_ref.shape[1] // hd):
            cols = slice(hh * hd, (hh + 1) * hd)
            q = q_ref[lo:hi, cols]
            s = _dot_nt(q, k_ref[0:hi, cols]) - c_ref[hh, :, 0:hi]
            s_diag = jnp.where(keep, s[:, lo:hi], NEG_INF)
            m = jnp.max(s_diag, axis=-1, keepdims=True)
            if i > 0:
                m = jnp.maximum(m, jnp.max(s[:, 0:lo], axis=-1, keepdims=True))
                e = jnp.concatenate([jnp.exp2(s[:, 0:lo] - m), jnp.exp2(s_diag - m)], axis=1)
            else:
                e = jnp.exp2(s_diag - m)
            den = jnp.sum(e, axis=-1, keepdims=True)
            o_ref[lo:hi, cols] = (_dot(e.astype(BF16), v_ref[0:hi, cols]) / den).astype(o_ref.dtype)


def fox_prompt_attention(q, kb, vb, c, nb, nh):
    m, d = q.shape
    length = m // nb
    hd = d // nh
    tq = min(FOX_TQ, length)
    hps = min(ATTN_HEADS_PER_STEP, nh)
    ngrp = nh // hps
    seq = pl.BlockSpec((length, hps * hd), lambda b, h: (b, h))
    return pl.pallas_call(
        functools.partial(_fox_prompt_body, tq, hd),
        grid=(nb, ngrp),
        in_specs=[seq, seq, seq, pl.BlockSpec((hps, 1, length), lambda b, h: (b * ngrp + h, 0, 0))],
        out_specs=seq,
        out_shape=jax.ShapeDtypeStruct((m, d), BF16),
        compiler_params=_params("parallel", "parallel"),
        name="fox_prompt_attention",
    )(q, kb, vb, c)


def _fox_sample_body(nh, q_ref, kc_ref, vc_ref, kn_ref, vn_ref, cc_ref, cn_ref, o_ref,
                     m_ref, l_ref, acc_ref):
    j = pl.program_id(1)
    lq, d = q_ref.shape
    hd = d // nh

    @pl.when(j == 0)
    def _():
        m_ref[...] = jnp.full_like(m_ref, NEG_INF)
        l_ref[...] = jnp.zeros_like(l_ref)
        acc_ref[...] = jnp.zeros_like(acc_ref)

    heads = lambda ref: [ref[:, h * hd:(h + 1) * hd] for h in range(nh)]
    qs = heads(q_ref)

    def update(s, vs):
        m_old = m_ref[...]
        m_new = jnp.maximum(m_old, jnp.max(s, axis=-1, keepdims=True))
        alpha = jnp.exp2(m_old - m_new)
        e = jnp.exp2(s - m_new)
        l_ref[...] = alpha * l_ref[...] + jnp.sum(e, axis=-1, keepdims=True)
        e = e.astype(BF16)
        pv = jnp.concatenate([_dot(e[h * lq:(h + 1) * lq], vs[h]) for h in range(nh)], axis=0)
        acc_ref[...] = alpha * acc_ref[...] + pv
        m_ref[...] = m_new

    k_all = pltpu.einshape("lhd->hld", kc_ref[...])
    v_all = pltpu.einshape("lhd->hld", vc_ref[...])
    s = jnp.concatenate([_dot_nt(qs[h], k_all[h].astype(BF16)) - cc_ref[0, h:h + 1, :]
                         for h in range(nh)], axis=0)
    update(s, [v_all[h].astype(BF16) for h in range(nh)])

    @pl.when(j == pl.num_programs(1) - 1)
    def _():
        keep = _causal(lq)
        kn = heads(kn_ref)
        s = jnp.concatenate(
            [jnp.where(keep, _dot_nt(qs[h], kn[h]) - cn_ref[0, h:h + 1, 0:lq], NEG_INF)
             for h in range(nh)], axis=0)
        update(s, heads(vn_ref))
        o = acc_ref[...] / l_ref[...]
        for h in range(nh):
            o_ref[:, h * hd:(h + 1) * hd] = o[h * lq:(h + 1) * lq].astype(o_ref.dtype)


def fox_sample_attention(q, k_cache, v_cache, layer, kb, vb, c, nb, nh):
    m, d = q.shape
    lq = m // nb
    past, hd = k_cache.shape[2], k_cache.shape[4]
    tk = min(FOX_TK, past)
    nkv = past // tk
    new = pl.BlockSpec((lq, d), lambda b, j: (b, 0))
    cache = pl.BlockSpec((None, None, tk, nh, hd), lambda b, j: (layer, b, j, 0, 0))
    return pl.pallas_call(
        functools.partial(_fox_sample_body, nh),
        grid=(nb, nkv),
        in_specs=[new, cache, cache, new, new,
                  pl.BlockSpec((1, nh, tk), lambda b, j: (b, 0, j)),
                  pl.BlockSpec((1, nh, LANES), lambda b, j: (b, 0, past // LANES))],
        out_specs=new,
        out_shape=jax.ShapeDtypeStruct((m, d), BF16),
        scratch_shapes=[pltpu.VMEM((nh * lq, 1), F32), pltpu.VMEM((nh * lq, 1), F32),
                        pltpu.VMEM((nh * lq, hd), F32)],
        compiler_params=_params("parallel", "arbitrary"),
        name="fox_sample_attention",
    )(q, k_cache, v_cache, kb, vb, c, c)


def _band_prompt_body(tq, hd, q_ref, k_ref, v_ref, bias_ref, o_ref):
    length = q_ref.shape[0]
    span = bias_ref.shape[2]
    for i in range(length // tq):
        k0 = max((i + 1) * tq - span, 0)
        k1 = (i + 1) * tq
        for hh in range(q_ref.shape[1] // hd):
            cols = slice(hh * hd, (hh + 1) * hd)
            q = q_ref[i * tq:(i + 1) * tq, cols]
            s = _dot_nt(q, k_ref[k0:k1, cols]) + bias_ref[hh, :, span - (k1 - k0):span]
            o_ref[i * tq:(i + 1) * tq, cols] = _softmax_pv([(s, v_ref[k0:k1, cols])]).astype(o_ref.dtype)


def band_prompt_attention(q, kb, vb, bias, nb, nh):
    m, d = q.shape
    length = m // nb
    hd = d // nh
    tq = bias.shape[1]
    hps = min(ATTN_HEADS_PER_STEP, nh)
    seq = pl.BlockSpec((length, hps * hd), lambda b, h: (b, h))
    return pl.pallas_call(
        functools.partial(_band_prompt_body, tq, hd),
        grid=(nb, nh // hps),
        in_specs=[seq, seq, seq, pl.BlockSpec((hps,) + bias.shape[1:], lambda b, h: (h, 0, 0))],
        out_specs=seq,
        out_shape=jax.ShapeDtypeStruct((m, d), BF16),
        compiler_params=_params("parallel", "parallel"),
        name="band_prompt_attention",
    )(q, kb, vb, bias)


def _band_sample_body(nh, q_ref, kc_ref, vc_ref, kn_ref, vn_ref, bias_ref, o_ref):
    lq, d = q_ref.shape
    hd = d // nh
    lc = kc_ref.shape[0]
    k_all = pltpu.einshape("lhd->hld", kc_ref[...])
    v_all = pltpu.einshape("lhd->hld", vc_ref[...])
    heads = lambda ref: [ref[:, h * hd:(h + 1) * hd] for h in range(nh)]
    qs, kn, vn = heads(q_ref), heads(kn_ref), heads(vn_ref)
    s_c = jnp.concatenate([_dot_nt(qs[h], k_all[h].astype(BF16)) + bias_ref[h, :, 0:lc]
                           for h in range(nh)], axis=0)
    s_n = jnp.concatenate([_dot_nt(qs[h], kn[h]) + bias_ref[h, :, lc:lc + lq] for h in range(nh)], axis=0)
    m = jnp.maximum(jnp.max(s_c, axis=-1, keepdims=True), jnp.max(s_n, axis=-1, keepdims=True))
    e_c, e_n = jnp.exp2(s_c - m), jnp.exp2(s_n - m)
    inv = 1.0 / (jnp.sum(e_c, axis=-1, keepdims=True) + jnp.sum(e_n, axis=-1, keepdims=True))
    e_c, e_n = e_c.astype(BF16), e_n.astype(BF16)
    for h in range(nh):
        rows = slice(h * lq, (h + 1) * lq)
        o = _dot(e_c[rows], v_all[h].astype(BF16)) + _dot(e_n[rows], vn[h])
        o_ref[:, h * hd:(h + 1) * hd] = (o * inv[rows]).astype(o_ref.dtype)


def band_sample_attention(q, k_cache, v_cache, layer, kb, vb, bias, nb, nh):
    m, d = q.shape
    lq = m // nb
    lc, hd = k_cache.shape[2], k_cache.shape[4]
    new = pl.BlockSpec((lq, d), lambda b: (b, 0))
    cache = pl.BlockSpec((None, None, lc, nh, hd), lambda b: (layer, b, 0, 0, 0))
    return pl.pallas_call(
        functools.partial(_band_sample_body, nh),
        grid=(nb,),
        in_specs=[new, cache, cache, new, new, pl.BlockSpec(bias.shape, lambda b: (0, 0, 0))],
        out_specs=new,
        out_shape=jax.ShapeDtypeStruct((m, d), BF16),
        compiler_params=_params("parallel"),
        name="band_sample_attention",
    )(q, k_cache, v_cache, kb, vb, bias)


def _band_bias_body(window, scale, diag_ref, o_ref):
    tq, span = o_ref.shape[1], o_ref.shape[2]
    n = diag_ref.shape[2]
    rows = jnp.broadcast_to(diag_ref[0], (tq, n))
    tile = pltpu.roll(rows, n - (tq - 1), 1, stride=1, stride_axis=0)[:, 0:span]
    qc = lax.broadcasted_iota(jnp.int32, (tq, span), 0) // CHUNK
    kc = lax.broadcasted_iota(jnp.int32, (tq, span), 1) // CHUNK - window // CHUNK
    visible = (kc <= qc) & (kc >= qc - BAND_PREV)
    o_ref[0] = jnp.where(visible, tile * scale, NEG_INF)


def band_bias_tile(rel_bias, tq, window, scale):
    nh = rel_bias.shape[0]
    span = window + tq
    n = span + tq
    rel = (span - 1) - np.arange(n)
    diag = rel_bias.astype(F32)[:, np.clip(rel, -REL_CLIP, REL_CLIP) + REL_CLIP].reshape(nh, 1, n)
    return pl.pallas_call(
        functools.partial(_band_bias_body, window, scale),
        grid=(nh,),
        in_specs=[pl.BlockSpec((1, 1, n), lambda h: (h, 0, 0))],
        out_specs=pl.BlockSpec((1, tq, span), lambda h: (h, 0, 0)),
        out_shape=jax.ShapeDtypeStruct((nh, tq, span), F32),
        compiler_params=_params("parallel"),
        name="band_bias_tile",
    )(diag)


def kernel(x_prompt, x_sample, state_ssm_re, state_ssm_im, cache_fox_k, cache_fox_v, cache_fox_logf,
           cache_band_k, cache_band_v, ffn1_norm, ffn1_w_in, ffn1_w_out, mix_norm, ffn2_norm, ffn2_w_in,
           ffn2_w_out, ssm_a_re, ssm_a_im, ssm_log_step, ssm_b_re, ssm_b_im, ssm_c_re, ssm_c_im, ssm_d,
           ssm_w_glu, fox_w_in, fox_b_f, fox_w_out, band_w_in, band_rel_bias, band_w_out, final_norm):
    bp, lp, d = x_prompt.shape
    bs, ls, _ = x_sample.shape
    depth = ffn1_norm.shape[0]
    nh = fox_b_f.shape[1]
    hd = d // nh
    scale = hd ** -0.5 * LOG2E
    window = cache_band_k.shape[2]
    heads = lambda t, nb, length: t.reshape(nb, length, nh, hd)

    xp = x_prompt.reshape(bp * lp, d)
    xs = x_sample.reshape(bs * ls, d)
    outs = {name: [] for name in (
        "p_ssm_re", "p_ssm_im", "p_fox_k", "p_fox_v", "p_fox_logf", "p_band_k", "p_band_v",
        "s_ssm_re", "s_ssm_im", "s_fox_k", "s_fox_v", "s_fox_logf", "s_band_k", "s_band_v")}

    w_glu = ssm_w_glu.astype(BF16)
    fox_in, fox_out = fox_w_in.astype(BF16), fox_w_out.astype(BF16)
    band_in, band_out = band_w_in.astype(BF16), band_w_out.astype(BF16)

    def ffn(xp, xs, g, w_in, w_out, layer):
        xs, w_gate, w_up, w_down = ffn_half_step_casting(xs, g, w_in, w_out, layer)
        return ffn_half_step(xp, g, w_gate, w_up, w_down), xs

    g, p = ssm_a_re.shape[1:]
    fold = lambda a: a.reshape((-1,) + a.shape[2:])
    mats = ssm_matrices(fold(ssm_a_re), fold(ssm_a_im), fold(ssm_log_step), fold(ssm_b_re), fold(ssm_b_im),
                        fold(ssm_c_re), fold(ssm_c_im))

    for i in range(depth):
        kind, j = i % 3, i // 3
        xp, xs = ffn(xp, xs, ffn1_norm[i], ffn1_w_in, ffn1_w_out, i)
        if kind == 0:
            nblk = d // LANES
            zeros = jnp.zeros((bp, g, p), F32)
            streams = ((xp, bp, lp, zeros, zeros, "p"), (xs, bs, ls, state_ssm_re[j], state_ssm_im[j], "s"))
            new_x = []
            for x, nb, length, s_re, s_im, tag in streams:
                h_tm = rmsnorm_time_major(x, mix_norm[i], nb, length)
                a_tm, sf = ssm_core(h_tm, mats, j, ssm_pack_state(s_re, s_im, nblk), ssm_d[j])
                f_re, f_im = ssm_unpack_state(sf, g, p)
                outs[tag + "_ssm_re"].append(f_re); outs[tag + "_ssm_im"].append(f_im)
                new_x.append(ssm_glu_out(x, a_tm, w_glu, j, nb, length))
            xp, xs = new_x
        elif kind == 1:
            wf_t = fox_w_in[j][:, 3 * d:].T.astype(BF16)
            qp, kp, vp, kbp, vbp, lfp = qkv_project(xp, mix_norm[i], fox_in, j, scale, wf_t, fox_b_f[j])
            qs, ks, vs, kbs, vbs, lfs = qkv_project(xs, mix_norm[i], fox_in, j, scale, wf_t, fox_b_f[j])
            lfp = lfp.reshape(nh, bp, lp).transpose(1, 0, 2)
            lfs = lfs.reshape(nh, bs, ls).transpose(1, 0, 2)
            cp = (cumsum_time(lfp) * LOG2E).reshape(bp * nh, 1, lp)
            lf_all = jnp.concatenate([cache_fox_logf[j].astype(F32).transpose(0, 2, 1), lfs,
                                      jnp.zeros((bs, nh, LANES - ls), F32)], axis=2)
            cs = cumsum_time(lf_all) * LOG2E
            op = fox_prompt_attention(qp, kbp, vbp, cp, bp, nh)
            os_ = fox_sample_attention(qs, cache_fox_k, cache_fox_v, j, kbs, vbs, cs, bs, nh)
            xp = out_project(xp, op, fox_out, j)
            xs = out_project(xs, os_, fox_out, j)
            outs["p_fox_k"].append(heads(kp, bp, lp)); outs["p_fox_v"].append(heads(vp, bp, lp))
            outs["p_fox_logf"].append(lfp.transpose(0, 2, 1))
            outs["s_fox_k"].append(heads(ks, bs, ls)); outs["s_fox_v"].append(heads(vs, bs, ls))
            outs["s_fox_logf"].append(lfs.transpose(0, 2, 1))
        else:
            qp, kp, vp, kbp, vbp = qkv_project(xp, mix_norm[i], band_in, j, scale)
            qs, ks, vs, kbs, vbs = qkv_project(xs, mix_norm[i], band_in, j, scale)
            bw = BAND_PREV * CHUNK
            bias = band_bias_tile(band_rel_bias[j], BAND_TQ, bw, LOG2E)
            op = band_prompt_attention(qp, kbp, vbp, bias, bp, nh)
            os_ = band_sample_attention(qs, cache_band_k, cache_band_v, j, kbs, vbs,
                                        bias[:, :ls, bw - window:bw + ls], bs, nh)
            xp = out_project(xp, op, band_out, j)
            xs = out_project(xs, os_, band_out, j)
            keep = min(bw, lp)
            outs["p_band_k"].append(heads(kp, bp, lp)[:, lp - keep:])
            outs["p_band_v"].append(heads(vp, bp, lp)[:, lp - keep:])
            outs["s_band_k"].append(heads(ks, bs, ls)); outs["s_band_v"].append(heads(vs, bs, ls))
        xp, xs = ffn(xp, xs, ffn2_norm[i], ffn2_w_in, ffn2_w_out, i)

    y_prompt = rmsnorm(xp, final_norm).reshape(bp, lp, d)
    y_sample = rmsnorm(xs, final_norm).reshape(bs, ls, d)
    st = {k: jnp.stack(v) for k, v in outs.items()}
    return (y_prompt, y_sample, st["p_ssm_re"], st["p_ssm_im"], st["p_fox_k"], st["p_fox_v"],
            st["p_fox_logf"], st["p_band_k"], st["p_band_v"], st["s_ssm_re"], st["s_ssm_im"],
            st["s_fox_k"], st["s_fox_v"], st["s_fox_logf"], st["s_band_k"], st["s_band_v"])
```

```python
import functools

import jax
import jax.numpy as jnp
import numpy as np
from jax import lax
from jax.experimental import pallas as pl
from jax.experimental.pallas import tpu as pltpu

F32 = jnp.float32
BF16 = jnp.bfloat16

EPS = 1e-6
NEG_INF = -1e30
LOG2E = 1.4426950408889634
CHUNK = 64
BAND_PREV = 8
REL_CLIP = 256
SSM_T = 8
LANES = 128
VMEM_LIMIT = 56 * 1024 * 1024

TM = 512
TM_FFN = 1024
TF = 512
TF_CAST = 256
TM_QKV = 1024
TN = 512
TN_GLU = 1024
BAND_TQ = 256
FOX_TQ = 256
ATTN_HEADS_PER_STEP = 4
FOX_TK = 512
SSM_ROWS = 1024


def _params(*sem):
    return pltpu.CompilerParams(dimension_semantics=sem, vmem_limit_bytes=VMEM_LIMIT)


def _rms(x, g):
    return x * lax.rsqrt(jnp.mean(x * x, axis=-1, keepdims=True) + EPS) * g


def _dot(a, b):
    return jnp.dot(a, b, preferred_element_type=F32)


def _dot_nt(a, b):
    return lax.dot_general(a, b, (((1,), (1,)), ((), ())), preferred_element_type=F32)


def _ffn_body(x_ref, g_ref, wg_ref, wu_ref, wo_ref, o_ref, h_ref):
    f = pl.program_id(1)

    @pl.when(f == 0)
    def _():
        x = x_ref[...]
        h_ref[...] = _rms(x, g_ref[...]).astype(BF16)
        o_ref[...] = x

    h = h_ref[...]
    gate = _dot(h, wg_ref[...])
    up = _dot(h, wu_ref[...])
    act = (gate * jax.nn.sigmoid(gate) * up).astype(BF16)
    o_ref[...] += _dot(act, wo_ref[...])


def ffn_half_step(x, g, w_gate, w_up, w_out):
    m, d = x.shape
    f = w_out.shape[0]
    tm = min(TM_FFN, m)
    tf = min(TF, f)
    return pl.pallas_call(
        _ffn_body,
        grid=(m // tm, f // tf),
        in_specs=[
            pl.BlockSpec((tm, d), lambda i, j: (i, 0)),
            pl.BlockSpec((1, d), lambda i, j: (0, 0)),
            pl.BlockSpec((d, tf), lambda i, j: (0, j)),
            pl.BlockSpec((d, tf), lambda i, j: (0, j)),
            pl.BlockSpec((tf, d), lambda i, j: (j, 0)),
        ],
        out_specs=pl.BlockSpec((tm, d), lambda i, j: (i, 0)),
        out_shape=jax.ShapeDtypeStruct((m, d), F32),
        scratch_shapes=[pltpu.VMEM((tm, d), BF16)],
        compiler_params=_params("parallel", "arbitrary"),
        name="ffn_half_step",
    )(x, g.reshape(1, d), w_gate, w_up, w_out)


def _ffn_cast_body(x_ref, g_ref, wg32_ref, wu32_ref, wo32_ref, o_ref, wg_ref, wu_ref, wo_ref, h_ref):
    wg_ref[...] = wg32_ref[...].astype(BF16)
    wu_ref[...] = wu32_ref[...].astype(BF16)
    wo_ref[...] = (0.5 * wo32_ref[...]).astype(BF16)
    _ffn_body(x_ref, g_ref, wg_ref, wu_ref, wo_ref, o_ref, h_ref)


def ffn_half_step_casting(x, g, w_in, w_out, layer):
    m, d = x.shape
    f = w_out.shape[1]
    tf = min(TF_CAST, f)
    nf = f // tf
    once = dict(pipeline_mode=pl.Buffered(1))
    col = pl.BlockSpec((d, tf), lambda i, j: (0, j))
    return pl.pallas_call(
        _ffn_cast_body,
        grid=(1, nf),
        in_specs=[
            pl.BlockSpec((m, d), lambda i, j: (0, 0), **once),
            pl.BlockSpec((1, d), lambda i, j: (0, 0)),
            pl.BlockSpec((None, d, tf), lambda i, j: (layer, 0, j)),
            pl.BlockSpec((None, d, tf), lambda i, j: (layer, 0, j + nf)),
            pl.BlockSpec((None, tf, d), lambda i, j: (layer, j, 0)),
        ],
        out_specs=[pl.BlockSpec((m, d), lambda i, j: (0, 0)), col, col,
                   pl.BlockSpec((tf, d), lambda i, j: (j, 0))],
        out_shape=[jax.ShapeDtypeStruct((m, d), F32), jax.ShapeDtypeStruct((d, f), BF16),
                   jax.ShapeDtypeStruct((d, f), BF16), jax.ShapeDtypeStruct((f, d), BF16)],
        scratch_shapes=[pltpu.VMEM((m, d), BF16)],
        compiler_params=_params("arbitrary", "arbitrary"),
        name="ffn_half_step_casting",
    )(x, g.reshape(1, d), w_in, w_in, w_out)


def _rmsnorm_body(x_ref, g_ref, o_ref):
    o_ref[...] = _rms(x_ref[...], g_ref[...]).astype(o_ref.dtype)


def rmsnorm(x, g):
    m, d = x.shape
    tm = min(TM, m)
    return pl.pallas_call(
        _rmsnorm_body,
        grid=(m // tm,),
        in_specs=[pl.BlockSpec((tm, d), lambda i: (i, 0)),
                  pl.BlockSpec((1, d), lambda i: (0, 0))],
        out_specs=pl.BlockSpec((tm, d), lambda i: (i, 0)),
        out_shape=jax.ShapeDtypeStruct((m, d), F32),
        compiler_params=_params("parallel"),
        name="rmsnorm",
    )(x, g.reshape(1, d))


def _rmsnorm_tm_body(x_ref, g_ref, o_ref):
    h = _rms(x_ref[...], g_ref[...])
    o_ref[...] = pltpu.einshape("bld->lbd", h).reshape(o_ref.shape)


def rmsnorm_time_major(x, g, nb, length):
    d = x.shape[1]
    t = SSM_T
    tl = min(TM // nb, length)
    return pl.pallas_call(
        _rmsnorm_tm_body,
        grid=(length // tl,),
        in_specs=[pl.BlockSpec((nb, tl, d), lambda l: (0, l, 0)),
                  pl.BlockSpec((1, d), lambda l: (0, 0))],
        out_specs=pl.BlockSpec((tl // t, t, nb, d), lambda l: (l, 0, 0, 0)),
        out_shape=jax.ShapeDtypeStruct((length // t, t, nb, d), F32),
        compiler_params=_params("parallel"),
        name="rmsnorm_time_major",
    )(x.reshape(nb, length, d), g.reshape(1, d))


def _qkv_body(scale, has_forget, *refs):
    if has_forget:
        (x_ref, g_ref, wq_ref, wk_ref, wv_ref, wf_ref, bf_ref,
         q_ref, k32_ref, v32_ref, kb_ref, vb_ref, lf_ref, h_ref) = refs
    else:
        (x_ref, g_ref, wq_ref, wk_ref, wv_ref,
         q_ref, k32_ref, v32_ref, kb_ref, vb_ref, h_ref) = refs

    @pl.when(pl.program_id(1) == 0)
    def _():
        h = _rms(x_ref[...], g_ref[...]).astype(BF16)
        h_ref[...] = h
        if has_forget:
            z = _dot_nt(wf_ref[...], h) + bf_ref[...]
            lf_ref[...] = jnp.minimum(z, 0.0) - jnp.log(1.0 + jnp.exp(-jnp.abs(z)))

    h = h_ref[...]
    q_ref[...] = (_dot(h, wq_ref[...]) * scale).astype(BF16)
    k = _dot(h, wk_ref[...])
    k32_ref[...] = k
    kb_ref[...] = k.astype(BF16)
    v = _dot(h, wv_ref[...])
    v32_ref[...] = v
    vb_ref[...] = v.astype(BF16)


def qkv_project(x, g, w, layer, scale, wf_t=None, b_f=None):
    m, d = x.shape
    tm = min(TM_QKV, m)
    tn = min(TN, d)
    nq = d // tn
    has_forget = wf_t is not None
    in_specs = [
        pl.BlockSpec((tm, d), lambda i, n: (i, 0)),
        pl.BlockSpec((1, d), lambda i, n: (0, 0)),
        pl.BlockSpec((None, d, tn), lambda i, n: (layer, 0, n)),
        pl.BlockSpec((None, d, tn), lambda i, n: (layer, 0, n + nq)),
        pl.BlockSpec((None, d, tn), lambda i, n: (layer, 0, n + 2 * nq)),
    ]
    args = [x, g.reshape(1, d), w, w, w]
    tile = pl.BlockSpec((tm, tn), lambda i, n: (i, n))
    out_specs = [tile] * 5
    out_shape = [jax.ShapeDtypeStruct((m, d), BF16),
                 jax.ShapeDtypeStruct((m, d), F32), jax.ShapeDtypeStruct((m, d), F32),
                 jax.ShapeDtypeStruct((m, d), BF16), jax.ShapeDtypeStruct((m, d), BF16)]
    if has_forget:
        nh = wf_t.shape[0]
        in_specs += [pl.BlockSpec((nh, d), lambda i, n: (0, 0)),
                     pl.BlockSpec((nh, 1), lambda i, n: (0, 0))]
        args += [wf_t, b_f.reshape(nh, 1)]
        out_specs = out_specs + [pl.BlockSpec((nh, tm), lambda i, n: (0, i))]
        out_shape = out_shape + [jax.ShapeDtypeStruct((nh, m), F32)]
    return pl.pallas_call(
        functools.partial(_qkv_body, scale, has_forget),
        grid=(m // tm, nq),
        in_specs=in_specs,
        out_specs=out_specs,
        out_shape=out_shape,
        scratch_shapes=[pltpu.VMEM((tm, d), BF16)],
        compiler_params=_params("parallel", "arbitrary"),
        name="qkv_project",
    )(*args)


def _out_proj_body(x_ref, o_ref, w_ref, y_ref):
    y_ref[...] = x_ref[...] + _dot(o_ref[...], w_ref[...])


def out_project(x, o, w, layer):
    m, d = x.shape
    tm = min(TM, m)
    return pl.pallas_call(
        _out_proj_body,
        grid=(m // tm,),
        in_specs=[pl.BlockSpec((tm, d), lambda i: (i, 0)),
                  pl.BlockSpec((tm, d), lambda i: (i, 0)),
                  pl.BlockSpec((None, d, d), lambda i: (layer, 0, 0))],
        out_specs=pl.BlockSpec((tm, d), lambda i: (i, 0)),
        out_shape=jax.ShapeDtypeStruct((m, d), F32),
        compiler_params=_params("parallel"),
        name="out_project",
    )(x, o, w)


def _glu_body(a_ref, xt_ref, wv_ref, wg_ref, o_ref):
    nb, tl, tn = xt_ref.shape
    a = a_ref[...].reshape(tl * nb, a_ref.shape[2])
    val = _dot(a, wv_ref[...])
    gate = _dot(a, wg_ref[...])
    r = (val * jax.nn.sigmoid(gate)).reshape(tl, nb, tn)
    o_ref[...] = xt_ref[...] + pltpu.einshape("lbn->bln", r)


def ssm_glu_out(x, a4, w_glu, layer, nb, length):
    d = x.shape[1]
    t = SSM_T
    tl = min(TM // nb, length)
    tn = min(TN_GLU, d)
    nq = d // tn
    tile = pl.BlockSpec((nb, tl, tn), lambda n, l: (0, l, n))
    return pl.pallas_call(
        _glu_body,
        grid=(nq, length // tl),
        in_specs=[
            pl.BlockSpec((tl // t, t * nb, d), lambda n, l: (l, 0, 0)),
            tile,
            pl.BlockSpec((None, d, tn), lambda n, l: (layer, 0, n)),
            pl.BlockSpec((None, d, tn), lambda n, l: (layer, 0, n + nq)),
        ],
        out_specs=tile,
        out_shape=jax.ShapeDtypeStruct((nb, length, d), F32),
        compiler_params=_params("parallel", "parallel"),
        name="ssm_glu_out",
    )(a4, x.reshape(nb, length, d), w_glu, w_glu).reshape(nb * length, d)


def _ssm_lag_body(ca_ref, bb_ref, k_ref):
    for i in range(ca_ref.shape[0]):
        k_ref[i] = jnp.dot(ca_ref[i], bb_ref[i], preferred_element_type=F32,
                           precision=lax.Precision.HIGHEST)


def ssm_lag_kernels(ca, bb, gpb):
    g, r, p2 = ca.shape
    c = bb.shape[2]
    return pl.pallas_call(
        _ssm_lag_body,
        grid=(g // gpb,),
        in_specs=[pl.BlockSpec((gpb, r, p2), lambda i: (i, 0, 0)),
                  pl.BlockSpec((gpb, p2, c), lambda i: (i, 0, 0))],
        out_specs=pl.BlockSpec((gpb, r, c), lambda i: (i, 0, 0)),
        out_shape=jax.ShapeDtypeStruct((g, r, c), F32),
        compiler_params=_params("parallel"),
        name="ssm_lag_kernels",
    )(ca, bb)


def _ssm_core_body(nb, x_ref, bdv_ref, ws_ref, wc_ref, at_ref, s0_ref, dsk_ref, y_ref, sf_ref,
                   ds_ref, dsw_ref, sst_ref, st_ref):
    nchp, t = x_ref.shape[0], x_ref.shape[1]
    m = nchp * nb
    ncol = at_ref.shape[2]
    half = LANES // 2

    def swap_halves(v):
        lane = lax.broadcasted_iota(jnp.int32, v.shape, 1)
        return jnp.where(lane % LANES < half, pltpu.roll(v, ncol - half, 1), pltpu.roll(v, half, 1))

    @pl.when(pl.program_id(1) == 0)
    def _():
        st_ref[...] = s0_ref[0]

    xcat = jnp.concatenate(
        [x_ref[:, tau].reshape(m, LANES).astype(BF16) for tau in range(t)], axis=1)
    ds = _dot(xcat, ws_ref[0])
    ds_ref[...] = ds
    dsw_ref[...] = swap_halves(ds)
    a1 = jnp.broadcast_to(at_ref[0, 0:1, :], (nb, ncol))
    a2 = jnp.broadcast_to(at_ref[0, 1:2, :], (nb, ncol))

    def step(k, carry):
        s, sw = carry
        rows = pl.ds(pl.multiple_of(k * nb, nb), nb)
        sst_ref[rows, :] = s
        ns = a1 * s + a2 * sw + ds_ref[rows, :]
        nsw = a1 * sw - a2 * s + dsw_ref[rows, :]
        return ns, nsw

    s0 = st_ref[...]
    s, _ = lax.fori_loop(0, nchp, step, (s0, swap_halves(s0)))
    st_ref[...] = s
    sf_ref[0] = s
    y_in = _dot(sst_ref[...].astype(BF16), wc_ref[0])
    dsk = dsk_ref[...]
    def out_step(tt):
        yt = y_in[:, tt * LANES:(tt + 1) * LANES] + _dot(
            xcat[:, :(tt + 1) * LANES], bdv_ref[0, (t - 1 - tt) * LANES:, :])
        return jax.nn.gelu(yt + dsk * x_ref[:, tt].reshape(m, LANES)).reshape(nchp, nb, LANES)

    for tt in range(0, t, 2):
        pair = jnp.concatenate([out_step(tt), out_step(tt + 1)], axis=1)
        y_ref[:, tt * nb:(tt + 2) * nb, :] = pair.astype(y_ref.dtype)


def ssm_core(h4, mats, layer, s0, d_skip):
    bdv, ws, wc, at = mats
    nch, t, nb, d = h4.shape
    nblk = d // LANES
    nchp = max(min(SSM_ROWS // nb, nch), 1)
    m = nchp * nb
    ncol = ws.shape[2]
    blk0 = layer * nblk
    wblk = lambda *s: pl.BlockSpec((1,) + s, lambda c, k: (c + blk0, 0, 0))
    sblk = pl.BlockSpec((1, nb, ncol), lambda c, k: (c, 0, 0))
    seq = pl.BlockSpec((nchp, t, nb, LANES), lambda c, k: (k, 0, 0, c))
    y, sf = pl.pallas_call(
        functools.partial(_ssm_core_body, nb),
        grid=(nblk, nch // nchp),
        in_specs=[seq, wblk(t * LANES, LANES), wblk(t * LANES, ncol), wblk(ncol, t * LANES),
                  wblk(2, ncol), sblk, pl.BlockSpec((1, LANES), lambda c, k: (0, c))],
        out_specs=[pl.BlockSpec((nchp, t * nb, LANES), lambda c, k: (k, 0, c)), sblk],
        out_shape=[jax.ShapeDtypeStruct((nch, t * nb, d), BF16),
                   jax.ShapeDtypeStruct((nblk, nb, ncol), F32)],
        scratch_shapes=[pltpu.VMEM((m, ncol), F32), pltpu.VMEM((m, ncol), F32), pltpu.VMEM((m, ncol), F32),
                        pltpu.VMEM((nb, ncol), F32)],
        compiler_params=_params("parallel", "arbitrary"),
        name="ssm_core",
    )(h4, bdv, ws, wc, at, s0, d_skip.reshape(1, d))
    return y, sf


def _ssm_expand_body(place_ref, wsc_ref, wcc_ref, kcp_ref, ws_ref, wc_ref, bdv_ref):
    gpb = place_ref.shape[0]
    ws_ref[0] = jnp.concatenate(
        [_dot(place_ref[h], wsc_ref[0, h]) for h in range(gpb)], axis=1).astype(BF16)
    wc_ref[0] = jnp.concatenate(
        [_dot_nt(wcc_ref[0, h], place_ref[h]) for h in range(gpb)], axis=0).astype(BF16)
    bdv = _dot(place_ref[0], kcp_ref[0, 0])
    for h in range(1, gpb):
        bdv = bdv + _dot(place_ref[h], kcp_ref[0, h])
    bdv_ref[0] = bdv.astype(BF16)


def ssm_expand(place, wsc, wcc, kcp):
    nblk, gpb, tc, n = wsc.shape
    rows = place.shape[1]
    per_blk = lambda *s: pl.BlockSpec((1,) + s, lambda b: (b,) + (0,) * len(s))
    return pl.pallas_call(
        _ssm_expand_body,
        grid=(nblk,),
        in_specs=[pl.BlockSpec(place.shape, lambda b: (0, 0, 0)), per_blk(gpb, tc, n), per_blk(gpb, n, tc),
                  per_blk(gpb, tc, LANES)],
        out_specs=[per_blk(rows, gpb * n), per_blk(gpb * n, rows), per_blk(rows, LANES)],
        out_shape=[jax.ShapeDtypeStruct((nblk, rows, gpb * n), BF16),
                   jax.ShapeDtypeStruct((nblk, gpb * n, rows), BF16),
                   jax.ShapeDtypeStruct((nblk, rows, LANES), BF16)],
        compiler_params=_params("parallel"),
        name="ssm_expand",
    )(place, wsc, wcc, kcp)


def ssm_matrices(a_re, a_im, log_step, b_re, b_im, c_re, c_im):
    t = SSM_T
    g, p = a_re.shape
    c = b_re.shape[2]
    gpb = LANES // c
    nblk = g // gpb
    dt = jnp.exp(log_step)[:, None]
    lam_re, lam_im = a_re * dt, a_im * dt
    j = jnp.arange(t + 1, dtype=F32)[:, None, None]
    mag = jnp.exp(lam_re * j)
    pw_re, pw_im = mag * jnp.cos(lam_im * j), mag * jnp.sin(lam_im * j)
    ab_re, ab_im = pw_re[1], pw_im[1]
    den = a_re * a_re + a_im * a_im
    z_re = ((ab_re - 1.0) * a_re + ab_im * a_im) / den
    z_im = (ab_im * a_re - (ab_re - 1.0) * a_im) / den
    bb_re = z_re[..., None] * b_re - z_im[..., None] * b_im
    bb_im = z_re[..., None] * b_im + z_im[..., None] * b_re
    jr = (t - 1) - jnp.arange(t, dtype=F32)[:, None, None]
    mag_r = jnp.exp(lam_re * jr)
    rp_re, rp_im = mag_r * jnp.cos(lam_im * jr), mag_r * jnp.sin(lam_im * jr)
    ws_re = rp_re[:, :, :, None] * bb_re[None] - rp_im[:, :, :, None] * bb_im[None]
    ws_im = rp_re[:, :, :, None] * bb_im[None] + rp_im[:, :, :, None] * bb_re[None]
    wsc = jnp.concatenate([ws_re, ws_im], axis=2).transpose(1, 0, 3, 2).reshape(nblk, gpb, t * c, 2 * p)

    def out_weights(q_re, q_im):
        cr, ci = c_re[None], c_im[None]
        qr, qi = q_re[:, :, None, :], q_im[:, :, None, :]
        return cr * qr - ci * qi, -cr * qi - ci * qr

    wc_re, wc_im = out_weights(pw_re[1:], pw_im[1:])
    wcc = jnp.concatenate([wc_re, wc_im], axis=3).transpose(1, 3, 0, 2).reshape(nblk, gpb, 2 * p, t * c)

    ca_re, ca_im = out_weights(rp_re, rp_im)
    ca = jnp.concatenate([ca_re, ca_im], axis=-1).transpose(1, 0, 2, 3).reshape(g, t * c, 2 * p)
    bb = jnp.concatenate([bb_re, bb_im], axis=1)
    kmat = ssm_lag_kernels(ca, bb, gpb).reshape(g, t, c, c)
    kc = kmat.transpose(0, 1, 3, 2).reshape(nblk, gpb, t * c, c)

    place = np.zeros((gpb, t, gpb, c, t, c), np.float32)
    for h in range(gpb):
        place[h, :, h] = np.eye(t * c, dtype=np.float32).reshape(t, c, t, c)
    place = jnp.asarray(place.reshape(gpb, t * LANES, t * c), BF16)
    eye = jnp.eye(gpb, dtype=F32)
    kcp = (kc[:, :, :, None, :] * eye[None, :, None, :, None]).reshape(nblk, gpb, t * c, LANES)
    ws, wc, bdv = ssm_expand(place, wsc.astype(BF16), wcc.astype(BF16), kcp.astype(BF16))

    at_re = pw_re[t].reshape(nblk, gpb, p)
    at_im = pw_im[t].reshape(nblk, gpb, p)
    a1 = jnp.concatenate([at_re, at_re], axis=2).reshape(nblk, gpb * 2 * p)
    a2 = jnp.concatenate([-at_im, at_im], axis=2).reshape(nblk, gpb * 2 * p)
    return bdv, ws, wc, jnp.stack([a1, a2], axis=1)


def ssm_pack_state(s_re, s_im, nblk):
    nb, g, p = s_re.shape
    s = jnp.stack([s_re.reshape(nb, nblk, g // nblk, p), s_im.reshape(nb, nblk, g // nblk, p)], axis=3)
    return s.transpose(1, 0, 2, 3, 4).reshape(nblk, nb, -1)


def ssm_unpack_state(sf, g, p):
    nblk, nb, _ = sf.shape
    s = sf.reshape(nblk, nb, g // nblk, 2, p).transpose(1, 0, 2, 3, 4).reshape(nb, g, 2, p)
    return s[:, :, 0], s[:, :, 1]


def _split3(x):
    hi = x.astype(BF16)
    r = x - hi.astype(F32)
    mid = r.astype(BF16)
    lo = (r - mid.astype(F32)).astype(BF16)
    return hi, mid, lo


def _cumsum_body(x_ref, o_ref):
    nblk = x_ref.shape[2] // LANES
    rows = x_ref.shape[1]
    row = lax.broadcasted_iota(jnp.int32, (LANES, LANES), 0)
    col = lax.broadcasted_iota(jnp.int32, (LANES, LANES), 1)
    tri = jnp.where(row <= col, 1.0, 0.0).astype(BF16)
    carry = jnp.zeros((rows, 1), F32)
    for j in range(nblk):
        sl = slice(j * LANES, (j + 1) * LANES)
        hi, mid, lo = _split3(x_ref[0, :, sl])
        w = _dot(hi, tri) + _dot(mid, tri) + _dot(lo, tri)
        o_ref[0, :, sl] = w + carry
        carry = carry + jnp.sum(x_ref[0, :, sl], axis=1, keepdims=True)


def cumsum_time(x):
    n, h, length = x.shape
    return pl.pallas_call(
        _cumsum_body,
        grid=(n,),
        in_specs=[pl.BlockSpec((1, h, length), lambda i: (i, 0, 0))],
        out_specs=pl.BlockSpec((1, h, length), lambda i: (i, 0, 0)),
        out_shape=jax.ShapeDtypeStruct((n, h, length), F32),
        compiler_params=_params("parallel"),
        name="cumsum_time",
    )(x)


def _softmax_pv(parts):
    m = functools.reduce(jnp.maximum, [jnp.max(s, axis=-1, keepdims=True) for s, _ in parts])
    acc, den = None, None
    for s, v in parts:
        e = jnp.exp2(s - m)
        l = jnp.sum(e, axis=-1, keepdims=True)
        o = _dot(e.astype(BF16), v)
        acc = o if acc is None else acc + o
        den = l if den is None else den + l
    return acc / den


def _causal(n):
    rpos = lax.broadcasted_iota(jnp.int32, (n, n), 0)
    cpos = lax.broadcasted_iota(jnp.int32, (n, n), 1)
    return cpos <= rpos


def _fox_prompt_body(tq, hd, q_ref, k_ref, v_ref, c_ref, o_ref):
    length = q_ref.shape[0]
    keep = _causal(tq)
    for i in range(length // tq):
        lo, hi = i * tq, (i + 1) * tq
        for hh in range(q_ref.shape[1] // hd):
            cols = slice(hh * hd, (hh + 1) * hd)
            q = q_ref[lo:hi, cols]
            s = _dot_nt(q, k_ref[0:hi, cols]) - c_ref[hh, :, 0:hi]
            s_diag = jnp.where(keep, s[:, lo:hi], NEG_INF)
            m = jnp.max(s_diag, axis=-1, keepdims=True)
            if i > 0:
                m = jnp.maximum(m, jnp.max(s[:, 0:lo], axis=-1, keepdims=True))
                e = jnp.concatenate([jnp.exp2(s[:, 0:lo] - m), jnp.exp2(s_diag - m)], axis=1)
            else:
                e = jnp.exp2(s_diag - m)
            den = jnp.sum(e, axis=-1, keepdims=True)
            o_ref[lo:hi, cols] = (_dot(e.astype(BF16), v_ref[0:hi, cols]) / den).astype(o_ref.dtype)


def fox_prompt_attention(q, kb, vb, c, nb, nh):
    m, d = q.shape
    length = m // nb
    hd = d // nh
    tq = min(FOX_TQ, length)
    hps = min(ATTN_HEADS_PER_STEP, nh)
    ngrp = nh // hps
    seq = pl.BlockSpec((length, hps * hd), lambda b, h: (b, h))
    return pl.pallas_call(
        functools.partial(_fox_prompt_body, tq, hd),
        grid=(nb, ngrp),
        in_specs=[seq, seq, seq, pl.BlockSpec((hps, 1, length), lambda b, h: (b * ngrp + h, 0, 0))],
        out_specs=seq,
        out_shape=jax.ShapeDtypeStruct((m, d), BF16),
        compiler_params=_params("parallel", "parallel"),
        name="fox_prompt_attention",
    )(q, kb, vb, c)


def _fox_sample_body(nh, q_ref, kc_ref, vc_ref, kn_ref, vn_ref, cc_ref, cn_ref, o_ref,
                     m_ref, l_ref, acc_ref):
    j = pl.program_id(1)
    lq, d = q_ref.shape
    hd = d // nh

    @pl.when(j == 0)
    def _():
        m_ref[...] = jnp.full_like(m_ref, NEG_INF)
        l_ref[...] = jnp.zeros_like(l_ref)
        acc_ref[...] = jnp.zeros_like(acc_ref)

    heads = lambda ref: [ref[:, h * hd:(h + 1) * hd] for h in range(nh)]
    qs = heads(q_ref)

    def update(s, vs):
        m_old = m_ref[...]
        m_new = jnp.maximum(m_old, jnp.max(s, axis=-1, keepdims=True))
        alpha = jnp.exp2(m_old - m_new)
        e = jnp.exp2(s - m_new)
        l_ref[...] = alpha * l_ref[...] + jnp.sum(e, axis=-1, keepdims=True)
        e = e.astype(BF16)
        pv = jnp.concatenate([_dot(e[h * lq:(h + 1) * lq], vs[h]) for h in range(nh)], axis=0)
        acc_ref[...] = alpha * acc_ref[...] + pv
        m_ref[...] = m_new

    k_all = pltpu.einshape("lhd->hld", kc_ref[...])
    v_all = pltpu.einshape("lhd->hld", vc_ref[...])
    s = jnp.concatenate([_dot_nt(qs[h], k_all[h].astype(BF16)) - cc_ref[0, h:h + 1, :]
                         for h in range(nh)], axis=0)
    update(s, [v_all[h].astype(BF16) for h in range(nh)])

    @pl.when(j == pl.num_programs(1) - 1)
    def _():
        keep = _causal(lq)
        kn = heads(kn_ref)
        s = jnp.concatenate(
            [jnp.where(keep, _dot_nt(qs[h], kn[h]) - cn_ref[0, h:h + 1, 0:lq], NEG_INF)
             for h in range(nh)], axis=0)
        update(s, heads(vn_ref))
        o = acc_ref[...] / l_ref[...]
        for h in range(nh):
            o_ref[:, h * hd:(h + 1) * hd] = o[h * lq:(h + 1) * lq].astype(o_ref.dtype)


def fox_sample_attention(q, k_cache, v_cache, layer, kb, vb, c, nb, nh):
    m, d = q.shape
    lq = m // nb
    past, hd = k_cache.shape[2], k_cache.shape[4]
    tk = min(FOX_TK, past)
    nkv = past // tk
    new = pl.BlockSpec((lq, d), lambda b, j: (b, 0))
    cache = pl.BlockSpec((None, None, tk, nh, hd), lambda b, j: (layer, b, j, 0, 0))
    return pl.pallas_call(
        functools.partial(_fox_sample_body, nh),
        grid=(nb, nkv),
        in_specs=[new, cache, cache, new, new,
                  pl.BlockSpec((1, nh, tk), lambda b, j: (b, 0, j)),
                  pl.BlockSpec((1, nh, LANES), lambda b, j: (b, 0, past // LANES))],
        out_specs=new,
        out_shape=jax.ShapeDtypeStruct((m, d), BF16),
        scratch_shapes=[pltpu.VMEM((nh * lq, 1), F32), pltpu.VMEM((nh * lq, 1), F32),
                        pltpu.VMEM((nh * lq, hd), F32)],
        compiler_params=_params("parallel", "arbitrary"),
        name="fox_sample_attention",
    )(q, k_cache, v_cache, kb, vb, c, c)


def _band_prompt_body(tq, hd, q_ref, k_ref, v_ref, bias_ref, o_ref):
    length = q_ref.shape[0]
    span = bias_ref.shape[2]
    for i in range(length // tq):
        k0 = max((i + 1) * tq - span, 0)
        k1 = (i + 1) * tq
        for hh in range(q_ref.shape[1] // hd):
            cols = slice(hh * hd, (hh + 1) * hd)
            q = q_ref[i * tq:(i + 1) * tq, cols]
            s = _dot_nt(q, k_ref[k0:k1, cols]) + bias_ref[hh, :, span - (k1 - k0):span]
            o_ref[i * tq:(i + 1) * tq, cols] = _softmax_pv([(s, v_ref[k0:k1, cols])]).astype(o_ref.dtype)


def band_prompt_attention(q, kb, vb, bias, nb, nh):
    m, d = q.shape
    length = m // nb
    hd = d // nh
    tq = bias.shape[1]
    hps = min(ATTN_HEADS_PER_STEP, nh)
    seq = pl.BlockSpec((length, hps * hd), lambda b, h: (b, h))
    return pl.pallas_call(
        functools.partial(_band_prompt_body, tq, hd),
        grid=(nb, nh // hps),
        in_specs=[seq, seq, seq, pl.BlockSpec((hps,) + bias.shape[1:], lambda b, h: (h, 0, 0))],
        out_specs=seq,
        out_shape=jax.ShapeDtypeStruct((m, d), BF16),
        compiler_params=_params("parallel", "parallel"),
        name="band_prompt_attention",
    )(q, kb, vb, bias)


def _band_sample_body(nh, q_ref, kc_ref, vc_ref, kn_ref, vn_ref, bias_ref, o_ref):
    lq, d = q_ref.shape
    hd = d // nh
    lc = kc_ref.shape[0]
    k_all = pltpu.einshape("lhd->hld", kc_ref[...])
    v_all = pltpu.einshape("lhd->hld", vc_ref[...])
    heads = lambda ref: [ref[:, h * hd:(h + 1) * hd] for h in range(nh)]
    qs, kn, vn = heads(q_ref), heads(kn_ref), heads(vn_ref)
    s_c = jnp.concatenate([_dot_nt(qs[h], k_all[h].astype(BF16)) + bias_ref[h, :, 0:lc]
                           for h in range(nh)], axis=0)
    s_n = jnp.concatenate([_dot_nt(qs[h], kn[h]) + bias_ref[h, :, lc:lc + lq] for h in range(nh)], axis=0)
    m = jnp.maximum(jnp.max(s_c, axis=-1, keepdims=True), jnp.max(s_n, axis=-1, keepdims=True))
    e_c, e_n = jnp.exp2(s_c - m), jnp.exp2(s_n - m)
    inv = 1.0 / (jnp.sum(e_c, axis=-1, keepdims=True) + jnp.sum(e_n, axis=-1, keepdims=True))
    e_c, e_n = e_c.astype(BF16), e_n.astype(BF16)
    for h in range(nh):
        rows = slice(h * lq, (h + 1) * lq)
        o = _dot(e_c[rows], v_all[h].astype(BF16)) + _dot(e_n[rows], vn[h])
        o_ref[:, h * hd:(h + 1) * hd] = (o * inv[rows]).astype(o_ref.dtype)


def band_sample_attention(q, k_cache, v_cache, layer, kb, vb, bias, nb, nh):
    m, d = q.shape
    lq = m // nb
    lc, hd = k_cache.shape[2], k_cache.shape[4]
    new = pl.BlockSpec((lq, d), lambda b: (b, 0))
    cache = pl.BlockSpec((None, None, lc, nh, hd), lambda b: (layer, b, 0, 0, 0))
    return pl.pallas_call(
        functools.partial(_band_sample_body, nh),
        grid=(nb,),
        in_specs=[new, cache, cache, new, new, pl.BlockSpec(bias.shape, lambda b: (0, 0, 0))],
        out_specs=new,
        out_shape=jax.ShapeDtypeStruct((m, d), BF16),
        compiler_params=_params("parallel"),
        name="band_sample_attention",
    )(q, k_cache, v_cache, kb, vb, bias)


def _band_bias_body(window, scale, diag_ref, o_ref):
    tq, span = o_ref.shape[1], o_ref.shape[2]
    n = diag_ref.shape[2]
    rows = jnp.broadcast_to(diag_ref[0], (tq, n))
    tile = pltpu.roll(rows, n - (tq - 1), 1, stride=1, stride_axis=0)[:, 0:span]
    qc = lax.broadcasted_iota(jnp.int32, (tq, span), 0) // CHUNK
    kc = lax.broadcasted_iota(jnp.int32, (tq, span), 1) // CHUNK - window // CHUNK
    visible = (kc <= qc) & (kc >= qc - BAND_PREV)
    o_ref[0] = jnp.where(visible, tile * scale, NEG_INF)


def band_bias_tile(rel_bias, tq, window, scale):
    nh = rel_bias.shape[0]
    span = window + tq
    n = span + tq
    rel = (span - 1) - np.arange(n)
    diag = rel_bias.astype(F32)[:, np.clip(rel, -REL_CLIP, REL_CLIP) + REL_CLIP].reshape(nh, 1, n)
    return pl.pallas_call(
        functools.partial(_band_bias_body, window, scale),
        grid=(nh,),
        in_specs=[pl.BlockSpec((1, 1, n), lambda h: (h, 0, 0))],
        out_specs=pl.BlockSpec((1, tq, span), lambda h: (h, 0, 0)),
        out_shape=jax.ShapeDtypeStruct((nh, tq, span), F32),
        compiler_params=_params("parallel"),
        name="band_bias_tile",
    )(diag)


def kernel(x_prompt, x_sample, state_ssm_re, state_ssm_im, cache_fox_k, cache_fox_v, cache_fox_logf,
           cache_band_k, cache_band_v, ffn1_norm, ffn1_w_in, ffn1_w_out, mix_norm, ffn2_norm, ffn2_w_in,
           ffn2_w_out, ssm_a_re, ssm_a_im, ssm_log_step, ssm_b_re, ssm_b_im, ssm_c_re, ssm_c_im, ssm_d,
           ssm_w_glu, fox_w_in, fox_b_f, fox_w_out, band_w_in, band_rel_bias, band_w_out, final_norm):
    bp, lp, d = x_prompt.shape
    bs, ls, _ = x_sample.shape
    depth = ffn1_norm.shape[0]
    nh = fox_b_f.shape[1]
    hd = d // nh
    scale = hd ** -0.5 * LOG2E
    window = cache_band_k.shape[2]
    heads = lambda t, nb, length: t.reshape(nb, length, nh, hd)

    xp = x_prompt.reshape(bp * lp, d)
    xs = x_sample.reshape(bs * ls, d)
    outs = {name: [] for name in (
        "p_ssm_re", "p_ssm_im", "p_fox_k", "p_fox_v", "p_fox_logf", "p_band_k", "p_band_v",
        "s_ssm_re", "s_ssm_im", "s_fox_k", "s_fox_v", "s_fox_logf", "s_band_k", "s_band_v")}

    w_glu = ssm_w_glu.astype(BF16)
    fox_in, fox_out = fox_w_in.astype(BF16), fox_w_out.astype(BF16)
    band_in, band_out = band_w_in.astype(BF16), band_w_out.astype(BF16)

    def ffn(xp, xs, g, w_in, w_out, layer):
        xs, w_gate, w_up, w_down = ffn_half_step_casting(xs, g, w_in, w_out, layer)
        return ffn_half_step(xp, g, w_gate, w_up, w_down), xs

    g, p = ssm_a_re.shape[1:]
    fold = lambda a: a.reshape((-1,) + a.shape[2:])
    mats = ssm_matrices(fold(ssm_a_re), fold(ssm_a_im), fold(ssm_log_step), fold(ssm_b_re), fold(ssm_b_im),
                        fold(ssm_c_re), fold(ssm_c_im))

    for i in range(depth):
        kind, j = i % 3, i // 3
        xp, xs = ffn(xp, xs, ffn1_norm[i], ffn1_w_in, ffn1_w_out, i)
        if kind == 0:
            nblk = d // LANES
            zeros = jnp.zeros((bp, g, p), F32)
            streams = ((xp, bp, lp, zeros, zeros, "p"), (xs, bs, ls, state_ssm_re[j], state_ssm_im[j], "s"))
            new_x = []
            for x, nb, length, s_re, s_im, tag in streams:
                h_tm = rmsnorm_time_major(x, mix_norm[i], nb, length)
                a_tm, sf = ssm_core(h_tm, mats, j, ssm_pack_state(s_re, s_im, nblk), ssm_d[j])
                f_re, f_im = ssm_unpack_state(sf, g, p)
                outs[tag + "_ssm_re"].append(f_re); outs[tag + "_ssm_im"].append(f_im)
                new_x.append(ssm_glu_out(x, a_tm, w_glu, j, nb, length))
            xp, xs = new_x
        elif kind == 1:
            wf_t = fox_w_in[j][:, 3 * d:].T.astype(BF16)
            qp, kp, vp, kbp, vbp, lfp = qkv_project(xp, mix_norm[i], fox_in, j, scale, wf_t, fox_b_f[j])
            qs, ks, vs, kbs, vbs, lfs = qkv_project(xs, mix_norm[i], fox_in, j, scale, wf_t, fox_b_f[j])
            lfp = lfp.reshape(nh, bp, lp).transpose(1, 0, 2)
            lfs = lfs.reshape(nh, bs, ls).transpose(1, 0, 2)
            cp = (cumsum_time(lfp) * LOG2E).reshape(bp * nh, 1, lp)
            lf_all = jnp.concatenate([cache_fox_logf[j].astype(F32).transpose(0, 2, 1), lfs,
                                      jnp.zeros((bs, nh, LANES - ls), F32)], axis=2)
            cs = cumsum_time(lf_all) * LOG2E
            op = fox_prompt_attention(qp, kbp, vbp, cp, bp, nh)
            os_ = fox_sample_attention(qs, cache_fox_k, cache_fox_v, j, kbs, vbs, cs, bs, nh)
            xp = out_project(xp, op, fox_out, j)
            xs = out_project(xs, os_, fox_out, j)
            outs["p_fox_k"].append(heads(kp, bp, lp)); outs["p_fox_v"].append(heads(vp, bp, lp))
            outs["p_fox_logf"].append(lfp.transpose(0, 2, 1))
            outs["s_fox_k"].append(heads(ks, bs, ls)); outs["s_fox_v"].append(heads(vs, bs, ls))
            outs["s_fox_logf"].append(lfs.transpose(0, 2, 1))
        else:
            qp, kp, vp, kbp, vbp = qkv_project(xp, mix_norm[i], band_in, j, scale)
            qs, ks, vs, kbs, vbs = qkv_project(xs, mix_norm[i], band_in, j, scale)
            bw = BAND_PREV * CHUNK
            bias = band_bias_tile(band_rel_bias[j], BAND_TQ, bw, LOG2E)
            op = band_prompt_attention(qp, kbp, vbp, bias, bp, nh)
            os_ = band_sample_attention(qs, cache_band_k, cache_band_v, j, kbs, vbs,
                                        bias[:, :ls, bw - window:bw + ls], bs, nh)
            xp = out_project(xp, op, band_out, j)
            xs = out_project(xs, os_, band_out, j)
            keep = min(bw, lp)
            outs["p_band_k"].append(heads(kp, bp, lp)[:, lp - keep:])
            outs["p_band_v"].append(heads(vp, bp, lp)[:, lp - keep:])
            outs["s_band_k"].append(heads(ks, bs, ls)); outs["s_band_v"].append(heads(vs, bs, ls))
        xp, xs = ffn(xp, xs, ffn2_norm[i], ffn2_w_in, ffn2_w_out, i)

    y_prompt = rmsnorm(xp, final_norm).reshape(bp, lp, d)
    y_sample = rmsnorm(xs, final_norm).reshape(bs, ls, d)
    st = {k: jnp.stack(v) for k, v in outs.items()}
    return (y_prompt, y_sample, st["p_ssm_re"], st["p_ssm_im"], st["p_fox_k"], st["p_fox_v"],
            st["p_fox_logf"], st["p_band_k"], st["p_band_v"], st["s_ssm_re"], st["s_ssm_im"],
            st["s_fox_k"], st["s_fox_v"], st["s_fox_logf"], st["s_band_k"], st["s_band_v"])
```
